```python
import math
import jax, jax.numpy as jnp
from jax import lax
import numpy as np

D_MODEL = 1024
BATCH = 8
SEQ = 2048
DEPTH = 2

N_GROUPS = 4
GROUP_WIDTH = D_MODEL // N_GROUPS
D_MIX = N_GROUPS * GROUP_WIDTH
HEAD_DIM = 64
N_HEADS_G = GROUP_WIDTH // HEAD_DIM
RWKV_LORA = 32
RWKV_DECAY_SCALE = math.exp(-0.5)
RWKV_LNX_EPS = 64e-5
RWKV_IN = 3 * GROUP_WIDTH + 2 * RWKV_LORA
RWKV_COLS = RWKV_IN + GROUP_WIDTH
BRANCH_COLS = 4 * GROUP_WIDTH
W_IN_COLS = RWKV_COLS + 3 * BRANCH_COLS
CONV_WIDTH = 3
MOBA_BLOCK = 256
MOBA_TOPK = 3
MOBA_QBLOCK = 128
RET_CHUNK = 128
REL_BUCKETS = 32
REL_MAX_DIST = 128
NORM_EPS = 1e-6

kernel_name = "hymba_style_rwkv7_conv_moba_retnet_hybrid"


def rmsnorm(x, w):
    xf = x.astype(jnp.float32)
    y = xf * lax.rsqrt(jnp.mean(xf * xf, axis=-1, keepdims=True) + NORM_EPS)
    return (y * w.astype(jnp.float32)).astype(x.dtype)


def token_shift(u):
    return jnp.pad(u, ((0, 0), (1, 0), (0, 0)))[:, :-1]


def rwkv7_time_mix(p, mu, w0, w2, a0, a2, k_k, k_a, r_k, lnx_w, lnx_b):
    B, T, _ = p.shape
    H, N, G, L = N_HEADS_G, HEAD_DIM, GROUP_WIDTH, RWKV_LORA
    p = p + (token_shift(p) - p) * mu
    r, k, v, wl, al = jnp.split(p, [G, 2 * G, 3 * G, 3 * G + L], axis=-1)
    decay = jnp.exp(-RWKV_DECAY_SCALE * jax.nn.sigmoid(w0 + jnp.tanh(wl) @ w2))
    a = jax.nn.sigmoid(a0 + al @ a2)
    kk = (k * k_k).reshape(B, T, H, N).astype(jnp.float32)
    kk = kk / jnp.maximum(jnp.sqrt(jnp.sum(kk * kk, axis=-1, keepdims=True)), 1e-12)
    k = k * (1.0 + (a - 1.0) * k_a)

    def heads(t):
        return t.reshape(B, T, H, N).astype(jnp.float32)

    rh, kh, vh = heads(r), heads(k), heads(v)
    to_time = lambda t: t.transpose(1, 0, 2, 3)
    xs = (to_time(rh), to_time(heads(decay)), to_time(kh), to_time(vh),
          to_time(kk), to_time(heads(a)))

    def step(state, inp):
        r_t, w_t, k_t, v_t, kk_t, a_t = inp
        s_kk = jnp.einsum('bhij,bhj->bhi', state, kk_t)
        state = (state * w_t[:, :, None, :]
                 - s_kk[..., None] * (kk_t * a_t)[:, :, None, :]
                 + v_t[..., None] * k_t[:, :, None, :])
        y_t = jnp.einsum('bhij,bhj->bhi', state, r_t)
        return state, y_t

    state0 = jnp.zeros((B, H, N, N), jnp.float32)
    _, y = lax.scan(step, state0, xs)
    y = y.transpose(1, 0, 2, 3)
    mean = jnp.mean(y, axis=-1, keepdims=True)
    var = jnp.mean(jnp.square(y - mean), axis=-1, keepdims=True)
    y = ((y - mean) * lax.rsqrt(var + RWKV_LNX_EPS)).reshape(B, T, G)
    y = y * lnx_w.astype(jnp.float32) + lnx_b.astype(jnp.float32)
    bonus = jnp.sum(rh * kh * r_k.astype(jnp.float32), axis=-1, keepdims=True) * vh
    y = y + bonus.reshape(B, T, G)
    return y.astype(p.dtype)


def short_conv_mix(p, conv_w):
    T = p.shape[1]
    b_gate, c_gate, h = jnp.split(p, 3, axis=-1)
    u = c_gate * h
    up = jnp.pad(u, ((0, 0), (CONV_WIDTH - 1, 0), (0, 0)))
    y = up[:, 0:T] * conv_w[0]
    for j in range(1, CONV_WIDTH):
        y = y + up[:, j:j + T] * conv_w[j]
    return b_gate * y


def t5_bucket(dist):
    max_exact = REL_BUCKETS // 2
    d_f = jnp.maximum(dist, 1).astype(jnp.float32)
    large = max_exact + (jnp.log(d_f / max_exact) / math.log(REL_MAX_DIST / max_exact)
                         * (REL_BUCKETS - max_exact)).astype(jnp.int32)
    large = jnp.minimum(large, REL_BUCKETS - 1)
    return jnp.where(dist < max_exact, dist, large)


def moba_attention(q, k, v, rel_bias):
    B, T, _ = q.shape
    H, Dh, BLK, QB = N_HEADS_G, HEAD_DIM, MOBA_BLOCK, MOBA_QBLOCK
    to_heads = lambda t: t.reshape(B, T, H, Dh).transpose(0, 2, 1, 3)
    q, k, v = to_heads(q) * (Dh ** -0.5), to_heads(k), to_heads(v)
    nb = -(-T // BLK)
    pad = nb * BLK - T
    kp = jnp.pad(k, ((0, 0), (0, 0), (0, pad), (0, 0))).reshape(B, H, nb, BLK, Dh)
    vp = jnp.pad(v, ((0, 0), (0, 0), (0, pad), (0, 0))).reshape(B, H, nb, BLK, Dh)
    k_mean = jnp.mean(kp, axis=3)
    gate = jnp.einsum('bhtd,bhnd->bhtn', q, k_mean).astype(jnp.float32)
    q_block = jnp.arange(T) // BLK
    fully_past = jnp.arange(nb)[None, :] < q_block[:, None]
    gate = jnp.where(fully_past, gate, -jnp.inf)
    ksel = min(MOBA_TOPK, nb)
    top_val, top_idx = lax.top_k(gate, ksel)
    sel_ok = jnp.isfinite(top_val)

    nqb = T // QB
    def to_items(t):
        X = t.shape[-1]
        return t.reshape(B, H, nqb, QB, X).transpose(0, 2, 1, 3, 4).reshape(B * nqb, H, QB, X)

    items = jnp.arange(B * nqb)
    bias_tab = rel_bias.T
    offs = jnp.arange(BLK)

    def one_item(args):
        item, q_i, idx_i, ok_i = args
        b = item // nqb
        qb = item % nqb
        kb = lax.dynamic_index_in_dim(kp, b, 0, keepdims=False)
        vb = lax.dynamic_index_in_dim(vp, b, 0, keepdims=False)
        k_sel = jax.vmap(lambda kh, ih: kh[ih])(kb, idx_i)
        v_sel = jax.vmap(lambda vh, ih: vh[ih])(vb, idx_i)
        ib = (qb * QB) // BLK
        k_own = lax.dynamic_index_in_dim(kb, ib, 1, keepdims=False)
        v_own = lax.dynamic_index_in_dim(vb, ib, 1, keepdims=False)
        q_pos = qb * QB + jnp.arange(QB)
        sel_pos = idx_i[..., None] * BLK + offs
        own_pos = ib * BLK + offs
        bias_sel = jax.vmap(lambda tab, bk: tab[bk])(bias_tab, t5_bucket(q_pos[None, :, None, None] - sel_pos))
        own_dist = q_pos[:, None] - own_pos[None, :]
        bias_own = bias_tab[:, t5_bucket(jnp.maximum(own_dist, 0))]
        s_sel = jnp.einsum('hqd,hqnkd->hqnk', q_i, k_sel) + bias_sel
        s_sel = jnp.where(ok_i[..., None], s_sel, -jnp.inf)
        s_own = jnp.einsum('hqd,hkd->hqk', q_i, k_own) + bias_own
        s_own = jnp.where((own_dist >= 0)[None], s_own, -jnp.inf)
        logits = jnp.concatenate([s_sel.reshape(H, QB, ksel * BLK), s_own], axis=-1).astype(jnp.float32)
        prob = jax.nn.softmax(logits, axis=-1).astype(v.dtype)
        p_sel = prob[..., :ksel * BLK].reshape(H, QB, ksel, BLK)
        p_own = prob[..., ksel * BLK:]
        return (jnp.einsum('hqnk,hqnkd->hqd', p_sel, v_sel)
                + jnp.einsum('hqk,hkd->hqd', p_own, v_own))

    out = lax.map(one_item, (items, to_items(q), to_items(top_idx), to_items(sel_ok)))
    out = out.reshape(B, nqb, H, QB, Dh).transpose(0, 1, 3, 2, 4).reshape(B, T, H * Dh)
    return out


def rotate_every_two(x, pos):
    half = x.shape[-1] // 2
    theta = 1.0 / (10000.0 ** jnp.linspace(0.0, 1.0, half))
    ang = pos[:, None] * theta[None, :]
    sin, cos = jnp.sin(ang).astype(x.dtype), jnp.cos(ang).astype(x.dtype)
    x1, x2 = x[..., 0::2], x[..., 1::2]
    return jnp.stack([x1 * cos - x2 * sin, x1 * sin + x2 * cos], axis=-1).reshape(x.shape)


def retention_chunkwise(q, k, v):
    B, T, _ = q.shape
    H, Dh, C = N_HEADS_G, HEAD_DIM, RET_CHUNK
    to_heads = lambda t: t.reshape(B, T, H, Dh).transpose(0, 2, 1, 3).astype(jnp.float32)
    pos = jnp.arange(T, dtype=jnp.float32)
    q = rotate_every_two(to_heads(q), pos)
    k = rotate_every_two(to_heads(k), pos) * (Dh ** -0.5)
    v = to_heads(v)
    log_gamma = jnp.log(1.0 - 2.0 ** (-5.0 - jnp.arange(H, dtype=jnp.float32)))
    idx = jnp.arange(C, dtype=jnp.float32)
    diff = idx[:, None] - idx[None, :]
    decay_intra = jnp.where(diff >= 0, jnp.exp(log_gamma[:, None, None] * jnp.maximum(diff, 0.0)), 0.0)
    q_decay = jnp.exp(log_gamma[:, None] * (idx + 1.0))[..., None]
    k_decay = jnp.exp(log_gamma[:, None] * (C - 1.0 - idx))[..., None]
    chunk_decay = jnp.exp(log_gamma * C)[:, None, None]
    nc = T // C
    chunks = lambda t: t.reshape(B, H, nc, C, Dh).transpose(2, 0, 1, 3, 4)

    def step(R, inp):
        qc, kc, vc = inp
        inner = jnp.einsum('bhnd,bhmd->bhnm', qc, kc) * decay_intra
        y = (jnp.einsum('bhnm,bhmd->bhnd', inner, vc)
             + jnp.einsum('bhnd,bhde->bhne', qc * q_decay, R))
        R = R * chunk_decay + jnp.einsum('bhmd,bhme->bhde', kc * k_decay, vc)
        return R, y

    R0 = jnp.zeros((B, H, Dh, Dh), jnp.float32)
    _, y = lax.scan(step, R0, (chunks(q), chunks(k), chunks(v)))
    y = y.transpose(1, 0, 3, 2, 4).reshape(B, T, H, Dh)
    y = y * lax.rsqrt(jnp.mean(y * y, axis=-1, keepdims=True) + NORM_EPS)
    return y.reshape(B, T, H * Dh)


def hybrid_layer(x, norm_w, w_in, w_out, rwkv_mu, rwkv_w0, rwkv_w2, rwkv_a0, rwkv_a2,
                 rwkv_k_k, rwkv_k_a, rwkv_r_k, rwkv_lnx_w, rwkv_lnx_b, conv_w, rel_bias):
    G = GROUP_WIDTH
    h = rmsnorm(x, norm_w)
    proj = h @ w_in
    p_rwkv, p_conv, p_moba, p_ret = jnp.split(
        proj, [RWKV_COLS, RWKV_COLS + BRANCH_COLS, RWKV_COLS + 2 * BRANCH_COLS], axis=-1)
    dt = proj.dtype
    y_rwkv = rwkv7_time_mix(p_rwkv[..., :RWKV_IN], rwkv_mu, rwkv_w0, rwkv_w2, rwkv_a0, rwkv_a2,
                            rwkv_k_k, rwkv_k_a, rwkv_r_k, rwkv_lnx_w, rwkv_lnx_b)
    y_rwkv = y_rwkv.astype(dt) * jax.nn.silu(p_rwkv[..., RWKV_IN:])
    y_conv = short_conv_mix(p_conv[..., :3 * G], conv_w).astype(dt) * jax.nn.silu(p_conv[..., 3 * G:])
    y_moba = moba_attention(p_moba[..., :G], p_moba[..., G:2 * G], p_moba[..., 2 * G:3 * G], rel_bias)
    y_moba = y_moba.astype(dt) * jax.nn.silu(p_moba[..., 3 * G:])
    y_ret = retention_chunkwise(p_ret[..., :G], p_ret[..., G:2 * G], p_ret[..., 2 * G:3 * G])
    y_ret = y_ret.astype(dt) * jax.nn.silu(p_ret[..., 3 * G:])
    y = jnp.concatenate([y_rwkv, y_conv, y_moba, y_ret], axis=-1)
    return x + y @ w_out


def setup_inputs(seed: int = 0) -> dict:
    key = jax.random.key(seed)
    ks = jax.random.split(key, 17)
    f32 = jnp.float32
    nrm = lambda k, shape: jax.random.normal(k, shape, f32)
    G, L = GROUP_WIDTH, RWKV_LORA
    return {
        "x": nrm(ks[0], (BATCH, SEQ, D_MODEL)),
        "norm_w": 1.0 + 0.02 * nrm(ks[1], (DEPTH, D_MODEL)),
        "w_in": nrm(ks[2], (DEPTH, D_MODEL, W_IN_COLS)) * D_MODEL ** -0.5,
        "w_out": nrm(ks[3], (DEPTH, D_MIX, D_MODEL)) * D_MIX ** -0.5,
        "rwkv_mu": jax.random.uniform(ks[4], (DEPTH, RWKV_IN), f32),
        "rwkv_w0": 0.5 * nrm(ks[5], (DEPTH, G)),
        "rwkv_w2": nrm(ks[6], (DEPTH, L, G)) * 0.5 * L ** -0.5,
        "rwkv_a0": 0.5 * nrm(ks[7], (DEPTH, G)),
        "rwkv_a2": nrm(ks[8], (DEPTH, L, G)) * 0.5 * L ** -0.5,
        "rwkv_k_k": 0.85 + 0.05 * nrm(ks[9], (DEPTH, G)),
        "rwkv_k_a": 1.0 + 0.05 * nrm(ks[10], (DEPTH, G)),
        "rwkv_r_k": 0.1 * nrm(ks[11], (DEPTH, N_HEADS_G, HEAD_DIM)),
        "rwkv_lnx_w": 1.0 + 0.02 * nrm(ks[12], (DEPTH, G)),
        "rwkv_lnx_b": 0.02 * nrm(ks[13], (DEPTH, G)),
        "conv_w": nrm(ks[14], (DEPTH, CONV_WIDTH, G)) * CONV_WIDTH ** -0.5,
        "rel_bias": 0.5 * nrm(ks[15], (REL_BUCKETS, N_HEADS_G)),
        "final_norm_w": 1.0 + 0.02 * nrm(ks[16], (D_MODEL,)),
    }


def reference(x, norm_w, w_in, w_out, rwkv_mu, rwkv_w0, rwkv_w2, rwkv_a0, rwkv_a2,
              rwkv_k_k, rwkv_k_a, rwkv_r_k, rwkv_lnx_w, rwkv_lnx_b, conv_w, rel_bias,
              final_norm_w):
    for l in range(DEPTH):
        x = hybrid_layer(x, norm_w[l], w_in[l], w_out[l], rwkv_mu[l], rwkv_w0[l], rwkv_w2[l],
                         rwkv_a0[l], rwkv_a2[l], rwkv_k_k[l], rwkv_k_a[l], rwkv_r_k[l],
                         rwkv_lnx_w[l], rwkv_lnx_b[l], conv_w[l], rel_bias)
    return rmsnorm(x, final_norm_w)
```

```python
import functools
import math

import numpy as np
import jax
import jax.numpy as jnp
from jax import lax
from jax.experimental import pallas as pl
from jax.experimental.pallas import tpu as pltpu

F32 = jnp.float32
BF16 = jnp.bfloat16
HIGHEST = lax.Precision.HIGHEST

GROUP = 256
HEADS = 4
HDIM = 64
LORA = 32
DECAY_SCALE = math.exp(-0.5)
LNX_EPS = 64e-5
NORM_EPS = 1e-6
CONV_TAPS = 3
MOBA_BLOCK = 256
MOBA_TOPK = 3
MOBA_QBLOCK = 128
RET_CHUNK = 128
REL_BUCKETS = 32
REL_MAX_DIST = 128
RWKV_CHUNK = 64

P_COLS = 4 * 4 * GROUP + 128
LORA_COL = 4 * 4 * GROUP

VMEM_LIMIT_V7X = 48 * 1024 * 1024


def _nn(a, b, precision=None):
    return lax.dot_general(a, b, (((1,), (0,)), ((), ())), precision=precision,
                           preferred_element_type=F32)


def _nt(a, b, precision=None):
    return lax.dot_general(a, b, (((1,), (1,)), ((), ())), precision=precision,
                           preferred_element_type=F32)


def _tn(a, b, precision=None):
    return lax.dot_general(a, b, (((0,), (0,)), ((), ())), precision=precision,
                           preferred_element_type=F32)


def _iota(shape, dim):
    return lax.broadcasted_iota(jnp.int32, shape, dim)


def _head_sum_matrix():
    r = _iota((GROUP, GROUP), 0) // HDIM
    c = _iota((GROUP, GROUP), 1) // HDIM
    return jnp.where(r == c, 1.0, 0.0).astype(F32)


def _silu(x):
    return x * jax.nn.sigmoid(x)


def _inproj_kernel(x_ref, nw_ref, w_ref, o_ref):
    x = x_ref[...]
    h = x * lax.rsqrt(jnp.mean(x * x, axis=-1, keepdims=True) + NORM_EPS) * nw_ref[...]
    o_ref[...] = _nn(h.astype(BF16), w_ref[...])


def _inproj(x2d, norm_w, w_bf16, tm=256):
    m, d = x2d.shape
    n = w_bf16.shape[1]
    return pl.pallas_call(
        _inproj_kernel,
        out_shape=jax.ShapeDtypeStruct((m, n), F32),
        grid=(m // tm,),
        in_specs=[pl.BlockSpec((tm, d), lambda i: (i, 0)),
                  pl.BlockSpec((1, d), lambda i: (0, 0)),
                  pl.BlockSpec((d, n), lambda i: (0, 0))],
        out_specs=pl.BlockSpec((tm, n), lambda i: (i, 0)),
        compiler_params=pltpu.CompilerParams(
            dimension_semantics=("arbitrary",), vmem_limit_bytes=VMEM_LIMIT_V7X),
        name="inproj",
    )(x2d, norm_w.reshape(1, d), w_bf16)


def _outproj_kernel(x_ref, y0_ref, y1_ref, y2_ref, y3_ref, w_ref, fw_ref, o_ref, *, final):
    acc = x_ref[...]
    for g, y_ref in enumerate((y0_ref, y1_ref, y2_ref, y3_ref)):
        acc = acc + _nn(y_ref[...].astype(BF16), w_ref[g * GROUP:(g + 1) * GROUP, :])
    if final:
        acc = acc * lax.rsqrt(jnp.mean(acc * acc, axis=-1, keepdims=True) + NORM_EPS) * fw_ref[...]
    o_ref[...] = acc


def _outproj(x2d, ys, w_bf16, final_w, final, tm=512):
    m, d = x2d.shape
    yspec = pl.BlockSpec((tm, GROUP), lambda i: (i, 0))
    return pl.pallas_call(
        functools.partial(_outproj_kernel, final=final),
        out_shape=jax.ShapeDtypeStruct((m, d), F32),
        grid=(m // tm,),
        in_specs=[pl.BlockSpec((tm, d), lambda i: (i, 0)), yspec, yspec, yspec, yspec,
                  pl.BlockSpec((4 * GROUP, d), lambda i: (0, 0)),
                  pl.BlockSpec((1, d), lambda i: (0, 0))],
        out_specs=pl.BlockSpec((tm, d), lambda i: (i, 0)),
        compiler_params=pltpu.CompilerParams(
            dimension_semantics=("arbitrary",), vmem_limit_bytes=VMEM_LIMIT_V7X),
        name="outproj_final" if final else "outproj",
    )(x2d, *ys, w_bf16, final_w.reshape(1, d))


def _rwkv_kernel(rkv_ref, gate_ref, lora_ref, mu_ref, mul_ref, w0_ref, w2_ref, a0_ref, a2_ref,
                 kk_ref, ka_ref, rk_ref, lnw_ref, lnb_ref, o_ref,
                 state_ref, prev_ref, prevl_ref, y_ref, *, tt):
    C = RWKV_CHUNK
    t_idx = pl.program_id(1)

    @pl.when(t_idx == 0)
    def _():
        state_ref[...] = jnp.zeros_like(state_ref)
        prev_ref[...] = jnp.zeros_like(prev_ref)
        prevl_ref[...] = jnp.zeros_like(prevl_ref)

    p = rkv_ref[0]
    lo = lora_ref[0]
    row0 = _iota((tt, 1), 0) == 0
    p_sh = jnp.where(row0, prev_ref[...], pltpu.roll(p, 1, axis=0))
    lo_sh = jnp.where(row0, prevl_ref[...], pltpu.roll(lo, 1, axis=0))
    prev_ref[...] = p[tt - 1:tt, :]
    prevl_ref[...] = lo[tt - 1:tt, :]
    p = p + (p_sh - p) * mu_ref[...]
    lo = lo + (lo_sh - lo) * mul_ref[...]

    r = p[:, 0:GROUP]
    k = p[:, GROUP:2 * GROUP]
    v = p[:, 2 * GROUP:3 * GROUP]

    esum = _head_sum_matrix()
    lw = -DECAY_SCALE * jax.nn.sigmoid(w0_ref[...] + _nn(jnp.tanh(lo), w2_ref[...], HIGHEST))
    rate = jax.nn.sigmoid(a0_ref[...] + _nn(lo, a2_ref[...], HIGHEST))
    kk = k * kk_ref[...]
    kk = kk / jnp.maximum(jnp.sqrt(_nn(kk * kk, esum, HIGHEST)), 1e-12)
    k2 = k * (1.0 + (rate - 1.0) * ka_ref[...])
    bonus = _nn(r * k2 * rk_ref[...], esum, HIGHEST) * v
    bvec = kk * rate
    avec = -kk

    ti = _iota((tt, tt), 0)
    si = _iota((tt, tt), 1)
    same = (ti // C) == (si // C)
    cum = _nn(jnp.where(same & (si <= ti), 1.0, 0.0).astype(F32), lw, HIGHEST)
    tot = _nn(jnp.where(same, 1.0, 0.0).astype(F32), lw, HIGHEST)
    e_neg = jnp.exp(-cum)
    e_end = jnp.exp(tot - cum)
    r_t = r * jnp.exp(cum)
    a_t = avec * jnp.exp(cum - lw)
    k_t = k2 * e_neg
    b_t = bvec * e_neg
    k_h = k2 * e_end
    b_h = bvec * e_end
    w_c = jnp.exp(tot)

    ci = _iota((C, C), 0)
    cj = _iota((C, C), 1)
    strict = cj < ci
    incl = cj <= ci
    eye = jnp.where(ci == cj, 1.0, 0.0).astype(F32)

    for h in range(HEADS):
        hs = slice(h * HDIM, (h + 1) * HDIM)
        s_vk = state_ref[h]
        for c in range(tt // C):
            cs = slice(c * C, (c + 1) * C)
            at, bt, kt, rt = a_t[cs, hs], b_t[cs, hs], k_t[cs, hs], r_t[cs, hs]
            vv = v[cs, hs]
            a_ab = jnp.where(strict, _nt(at, bt, HIGHEST), 0.0)
            a_ak = jnp.where(strict, _nt(at, kt, HIGHEST), 0.0)
            a_rk = jnp.where(incl, _nt(rt, kt, HIGHEST), 0.0)
            a_rb = jnp.where(incl, _nt(rt, bt, HIGHEST), 0.0)
            inv = eye + a_ab
            pw = a_ab
            for _ in range(int(math.log2(C)) - 1):
                pw = _nn(pw, pw, HIGHEST)
                inv = inv + _nn(pw, inv, HIGHEST)
            u = _nn(inv, _nt(at, s_vk, HIGHEST) + _nn(a_ak, vv, HIGHEST), HIGHEST)
            y = _nt(rt, s_vk, HIGHEST) + _nn(a_rk, vv, HIGHEST) + _nn(a_rb, u, HIGHEST)
            y_ref[cs, hs] = y
            s_vk = (s_vk * w_c[c * C:c * C + 1, hs]
                    + _tn(vv, k_h[cs, hs], HIGHEST) + _tn(u, b_h[cs, hs], HIGHEST))
        state_ref[h] = s_vk

    y = y_ref[...]
    mean = _nn(y, esum, HIGHEST) * (1.0 / HDIM)
    yc = y - mean
    var = _nn(yc * yc, esum, HIGHEST) * (1.0 / HDIM)
    y = yc * lax.rsqrt(var + LNX_EPS) * lnw_ref[...] + lnb_ref[...] + bonus
    o_ref[0] = y * _silu(gate_ref[0])


def _rwkv(p3, mu, w0, w2, a0, a2, k_k, k_a, r_k, lnx_w, lnx_b, tt=256):
    b, t, _ = p3.shape
    mu_main = mu[:3 * GROUP].reshape(1, 3 * GROUP)
    mu_lora = jnp.concatenate([mu[3 * GROUP:], jnp.zeros((128 - 2 * LORA,), F32)]).reshape(1, 128)
    w2p = jnp.zeros((128, GROUP), F32).at[0:LORA].set(w2)
    a2p = jnp.zeros((128, GROUP), F32).at[LORA:2 * LORA].set(a2)
    row = lambda a: a.reshape(1, GROUP)
    vec = pl.BlockSpec((1, GROUP), lambda i, j: (0, 0))
    return pl.pallas_call(
        functools.partial(_rwkv_kernel, tt=tt),
        out_shape=jax.ShapeDtypeStruct((b, t, GROUP), F32),
        grid=(b, t // tt),
        in_specs=[pl.BlockSpec((1, tt, 3 * GROUP), lambda i, j: (i, j, 0)),
                  pl.BlockSpec((1, tt, GROUP), lambda i, j: (i, j, 3)),
                  pl.BlockSpec((1, tt, 128), lambda i, j: (i, j, LORA_COL // 128)),
                  pl.BlockSpec((1, 3 * GROUP), lambda i, j: (0, 0)),
                  pl.BlockSpec((1, 128), lambda i, j: (0, 0)),
                  vec,
                  pl.BlockSpec((128, GROUP), lambda i, j: (0, 0)),
                  vec,
                  pl.BlockSpec((128, GROUP), lambda i, j: (0, 0)),
                  vec, vec, vec, vec, vec],
        out_specs=pl.BlockSpec((1, tt, GROUP), lambda i, j: (i, j, 0)),
        scratch_shapes=[pltpu.VMEM((HEADS, HDIM, HDIM), F32),
                        pltpu.VMEM((1, 3 * GROUP), F32),
                        pltpu.VMEM((1, 128), F32),
                        pltpu.VMEM((tt, GROUP), F32)],
        compiler_params=pltpu.CompilerParams(
            dimension_semantics=("arbitrary", "arbitrary"), vmem_limit_bytes=VMEM_LIMIT_V7X),
        name="rwkv7",
    )(p3, p3, p3, mu_main, mu_lora, row(w0), w2p, row(a0), a2p,
      row(k_k), row(k_a), row(r_k), row(lnx_w), row(lnx_b))


def _conv_kernel(p_ref, cw_ref, o_ref, tail_ref, *, tt):
    @pl.when(pl.program_id(1) == 0)
    def _():
        tail_ref[...] = jnp.zeros_like(tail_ref)

    p = p_ref[0]
    u = p[:, GROUP:2 * GROUP] * p[:, 2 * GROUP:3 * GROUP]
    rows = _iota((tt, 1), 0)
    u1 = jnp.where(rows == 0, tail_ref[1:2, :], pltpu.roll(u, 1, axis=0))
    u2 = jnp.where(rows == 0, tail_ref[0:1, :],
                   jnp.where(rows == 1, tail_ref[1:2, :], pltpu.roll(u, 2, axis=0)))
    tail_ref[0:2, :] = u[tt - 2:tt, :]
    y = u2 * cw_ref[0:1, :] + u1 * cw_ref[1:2, :] + u * cw_ref[2:3, :]
    o_ref[0] = p[:, 0:GROUP] * y * _silu(p[:, 3 * GROUP:4 * GROUP])


def _conv(p3, conv_w, tt=512):
    b, t, _ = p3.shape
    return pl.pallas_call(
        functools.partial(_conv_kernel, tt=tt),
        out_shape=jax.ShapeDtypeStruct((b, t, GROUP), F32),
        grid=(b, t // tt),
        in_specs=[pl.BlockSpec((1, tt, 4 * GROUP), lambda i, j: (i, j, 1)),
                  pl.BlockSpec((CONV_TAPS, GROUP), lambda i, j: (0, 0))],
        out_specs=pl.BlockSpec((1, tt, GROUP), lambda i, j: (i, j, 0)),
        scratch_shapes=[pltpu.VMEM((8, GROUP), F32)],
        compiler_params=pltpu.CompilerParams(
            dimension_semantics=("arbitrary", "arbitrary"), vmem_limit_bytes=VMEM_LIMIT_V7X),
        name="sconv",
    )(p3, conv_w)


def _t5_bucket_np(dist):
    max_exact = REL_BUCKETS // 2
    d_f = np.maximum(dist, 1).astype(np.float32)
    large = max_exact + (np.log(d_f / np.float32(max_exact)) / np.float32(math.log(REL_MAX_DIST / max_exact))
                         * np.float32(REL_BUCKETS - max_exact)).astype(np.int32)
    large = np.minimum(large, REL_BUCKETS - 1)
    return np.where(dist < max_exact, dist, large).astype(np.int32)


def _moba_bucket_table():
    tab = np.zeros((2, MOBA_QBLOCK, 3 * MOBA_BLOCK), np.int32)
    rows = np.arange(MOBA_QBLOCK)[:, None]
    cols = np.arange(MOBA_BLOCK)[None, :]
    for half in range(2):
        for j, delta in enumerate((2, 1, 0)):
            dist = half * MOBA_QBLOCK + rows + delta * MOBA_BLOCK - cols
            tab[half, :, j * MOBA_BLOCK:(j + 1) * MOBA_BLOCK] = _t5_bucket_np(np.maximum(dist, 0))
    return tab


def _bias_kernel(idx_ref, rb_ref, o_ref):
    h = pl.program_id(0)
    idx = idx_ref[0]
    acc = jnp.zeros(idx.shape, F32)
    for bkt in range(REL_BUCKETS):
        acc = jnp.where(idx == bkt, rb_ref[bkt, h], acc)
    o_ref[0, 0] = acc


def _moba_bias_tiles(rel_bias):
    idx = jnp.asarray(_moba_bucket_table())
    shp = (MOBA_QBLOCK, 3 * MOBA_BLOCK)
    return pl.pallas_call(
        _bias_kernel,
        out_shape=jax.ShapeDtypeStruct((HEADS, 2) + shp, F32),
        grid=(HEADS, 2),
        in_specs=[pl.BlockSpec((1,) + shp, lambda h, j: (j, 0, 0)),
                  pl.BlockSpec(memory_space=pltpu.SMEM)],
        out_specs=pl.BlockSpec((1, 1) + shp, lambda h, j: (h, j, 0, 0)),
        name="moba_bias",
    )(idx, rel_bias)


def _moba_kernel(q_ref, k_ref, v_ref, gate_ref, bias_ref, o_ref, kmean_ref, s_ref, y_ref, *, nb):
    QB, BLK = MOBA_QBLOCK, MOBA_BLOCK
    qb = pl.program_id(1)
    ib = qb // (BLK // QB)

    @pl.when(qb == 0)
    def _():
        for n in range(nb):
            kmean_ref[n:n + 1, :] = jnp.mean(k_ref[0, n * BLK:(n + 1) * BLK, :], axis=0, keepdims=True)

    q = q_ref[0] * (HDIM ** -0.5)
    lane = _iota((QB, nb), 1)
    neg_inf = -jnp.inf
    ib_blk = jnp.full((QB, BLK), ib, jnp.int32)
    causal = _iota((QB, BLK), 1) <= (qb % (BLK // QB)) * QB + _iota((QB, BLK), 0)

    for h in range(HEADS):
        hs = slice(h * HDIM, (h + 1) * HDIM)
        qh = q[:, hs]
        g = _nt(qh, kmean_ref[:, hs], HIGHEST)
        g = jnp.where(lane < ib, g, neg_inf)
        sel = jnp.zeros((QB, nb), F32)
        for _ in range(min(MOBA_TOPK, nb)):
            m = jnp.max(g, axis=1, keepdims=True)
            hit = (g == m) & (m > neg_inf)
            first = jnp.min(jnp.where(hit, lane, nb), axis=1, keepdims=True)
            pick = lane == first
            sel = jnp.where(pick, 1.0, sel)
            g = jnp.where(pick, neg_inf, g)
        for n in range(nb):
            s = _nt(qh, k_ref[0, n * BLK:(n + 1) * BLK, hs])
            delta = ib_blk - n
            bias = jnp.where(delta == 0, bias_ref[h, 0, :, 2 * BLK:3 * BLK],
                             jnp.where(delta == 1, bias_ref[h, 0, :, BLK:2 * BLK],
                                       bias_ref[h, 0, :, 0:BLK]))
            picked = jnp.broadcast_to(sel[:, n:n + 1], (QB, BLK)) > 0.5
            ok = ((delta > 0) & picked) | ((delta == 0) & causal)
            s_ref[:, n * BLK:(n + 1) * BLK] = jnp.where(ok, s + bias, neg_inf)
        s = s_ref[...]
        e = jnp.exp(s - jnp.max(s, axis=1, keepdims=True))
        prob = e / jnp.sum(e, axis=1, keepdims=True)
        y_ref[:, hs] = _nn(prob, v_ref[0, :, hs])
    o_ref[0] = y_ref[...] * _silu(gate_ref[0])


def _moba(p3, bias_tiles):
    b, t, _ = p3.shape
    nb = t // MOBA_BLOCK
    nqb = t // MOBA_QBLOCK
    base = 2 * 4
    bias_shape = (MOBA_QBLOCK, 3 * MOBA_BLOCK)
    return pl.pallas_call(
        functools.partial(_moba_kernel, nb=nb),
        out_shape=jax.ShapeDtypeStruct((b, t, GROUP), F32),
        grid=(b, nqb),
        in_specs=[pl.BlockSpec((1, MOBA_QBLOCK, GROUP), lambda i, j: (i, j, base)),
                  pl.BlockSpec((1, t, GROUP), lambda i, j: (i, 0, base + 1)),
                  pl.BlockSpec((1, t, GROUP), lambda i, j: (i, 0, base + 2)),
                  pl.BlockSpec((1, MOBA_QBLOCK, GROUP), lambda i, j: (i, j, base + 3)),
                  pl.BlockSpec((HEADS, 1) + bias_shape, lambda i, j: (0, j % 2, 0, 0))],
        out_specs=pl.BlockSpec((1, MOBA_QBLOCK, GROUP), lambda i, j: (i, j, 0)),
        scratch_shapes=[pltpu.VMEM((nb, GROUP), F32),
                        pltpu.VMEM((MOBA_QBLOCK, t), F32),
                        pltpu.VMEM((MOBA_QBLOCK, GROUP), F32)],
        compiler_params=pltpu.CompilerParams(
            dimension_semantics=("arbitrary", "arbitrary"), vmem_limit_bytes=VMEM_LIMIT_V7X),
        name="moba",
    )(p3, p3, p3, p3, bias_tiles)


def _ret_tables(t):
    half = HDIM // 2
    theta = 1.0 / (10000.0 ** jnp.linspace(0.0, 1.0, half))
    pos = jnp.arange(t, dtype=F32)
    ang = pos[:, None] * theta[None, :]
    sin, cos = jnp.sin(ang), jnp.cos(ang)
    cos2 = jnp.tile(jnp.repeat(cos, 2, axis=1), (1, HEADS))
    sin2 = jnp.tile(jnp.stack([-sin, sin], axis=-1).reshape(t, HDIM), (1, HEADS))
    log_gamma = jnp.log(1.0 - 2.0 ** (-5.0 - jnp.arange(HEADS, dtype=F32)))
    idx = jnp.arange(RET_CHUNK, dtype=F32)
    diff = idx[:, None] - idx[None, :]
    decay_intra = jnp.where(diff >= 0, jnp.exp(log_gamma[:, None, None] * jnp.maximum(diff, 0.0)), 0.0)
    q_decay = jnp.exp(log_gamma[:, None] * (idx + 1.0))
    k_decay = jnp.exp(log_gamma[:, None] * (RET_CHUNK - 1.0 - idx))
    chunk_decay = jnp.exp(log_gamma * RET_CHUNK)
    lanes = lambda a: jnp.repeat(a.T, HDIM, axis=1)
    cd = jnp.repeat(chunk_decay, HDIM).reshape(1, GROUP)
    return cos2, sin2, decay_intra, lanes(q_decay), lanes(k_decay), cd


def _ret_kernel(q_ref, k_ref, v_ref, gate_ref, cos_ref, sin_ref, di_ref, qd_ref, kd_ref, cd_ref,
                o_ref, state_ref, y_ref):
    @pl.when(pl.program_id(1) == 0)
    def _():
        state_ref[...] = jnp.zeros_like(state_ref)

    C = RET_CHUNK
    even = (_iota((C, GROUP), 1) % 2) == 0

    def rotate(x):
        swapped = jnp.where(even, pltpu.roll(x, GROUP - 1, axis=1), pltpu.roll(x, 1, axis=1))
        return x * cos_ref[...] + swapped * sin_ref[...]

    q = rotate(q_ref[0])
    k = rotate(k_ref[0]) * (HDIM ** -0.5)
    v = v_ref[0]
    qd = q * qd_ref[...]
    kd = k * kd_ref[...]
    cd = cd_ref[...]
    for h in range(HEADS):
        hs = slice(h * HDIM, (h + 1) * HDIM)
        state = state_ref[h]
        inner = _nt(q[:, hs], k[:, hs], HIGHEST) * di_ref[h]
        y = _nn(inner, v[:, hs], HIGHEST) + _nn(qd[:, hs], state, HIGHEST)
        state_ref[h] = state * cd[0:1, h * HDIM:h * HDIM + 1] + _tn(kd[:, hs], v[:, hs], HIGHEST)
        y_ref[:, hs] = y * lax.rsqrt(jnp.mean(y * y, axis=-1, keepdims=True) + NORM_EPS)
    o_ref[0] = y_ref[...] * _silu(gate_ref[0])


def _ret(p3, tables):
    b, t, _ = p3.shape
    C = RET_CHUNK
    base = 3 * 4
    cos2, sin2, di, qd, kd, cd = tables
    blk = lambda c: pl.BlockSpec((1, C, GROUP), lambda i, j, c=c: (i, j, c))
    full2 = lambda a: pl.BlockSpec(a.shape, lambda i, j: (0, 0))
    return pl.pallas_call(
        _ret_kernel,
        out_shape=jax.ShapeDtypeStruct((b, t, GROUP), F32),
        grid=(b, t // C),
        in_specs=[blk(base), blk(base + 1), blk(base + 2), blk(base + 3),
                  pl.BlockSpec((C, GROUP), lambda i, j: (j, 0)),
                  pl.BlockSpec((C, GROUP), lambda i, j: (j, 0)),
                  pl.BlockSpec(di.shape, lambda i, j: (0, 0, 0)),
                  full2(qd), full2(kd), full2(cd)],
        out_specs=pl.BlockSpec((1, C, GROUP), lambda i, j: (i, j, 0)),
        scratch_shapes=[pltpu.VMEM((HEADS, HDIM, HDIM), F32),
                        pltpu.VMEM((C, GROUP), F32)],
        compiler_params=pltpu.CompilerParams(
            dimension_semantics=("arbitrary", "arbitrary"), vmem_limit_bytes=VMEM_LIMIT_V7X),
        name="retnet",
    )(p3, p3, p3, p3, cos2, sin2, di, qd, kd, cd)


def _reorder_w_in(w):
    g3 = 3 * GROUP
    lora = w[:, g3:g3 + 2 * LORA]
    pad = jnp.zeros((w.shape[0], 128 - 2 * LORA), w.dtype)
    return jnp.concatenate([w[:, :g3], w[:, g3 + 2 * LORA:], lora, pad], axis=1)


def kernel(x, norm_w, w_in, w_out, rwkv_mu, rwkv_w0, rwkv_w2, rwkv_a0, rwkv_a2, rwkv_k_k, rwkv_k_a,
           rwkv_r_k, rwkv_lnx_w, rwkv_lnx_b, conv_w, rel_bias, final_norm_w):
    b, t, d = x.shape
    depth = w_in.shape[0]
    bias_tiles = _moba_bias_tiles(rel_bias)
    ret_tables = _ret_tables(t)
    x2 = x.reshape(b * t, d)
    for l in range(depth):
        w_l = _reorder_w_in(w_in[l]).astype(BF16)
        p3 = _inproj(x2, norm_w[l], w_l).reshape(b, t, P_COLS)
        y_rwkv = _rwkv(p3, rwkv_mu[l], rwkv_w0[l], rwkv_w2[l], rwkv_a0[l], rwkv_a2[l], rwkv_k_k[l],
                       rwkv_k_a[l], rwkv_r_k[l].reshape(GROUP), rwkv_lnx_w[l], rwkv_lnx_b[l])
        y_conv = _conv(p3, conv_w[l])
        y_moba = _moba(p3, bias_tiles)
        y_ret = _ret(p3, ret_tables)
        ys = [y.reshape(b * t, GROUP) for y in (y_rwkv, y_conv, y_moba, y_ret)]
        x2 = _outproj(x2, ys, w_out[l].astype(BF16), final_norm_w, final=(l == depth - 1))
    return x2.reshape(b, t, d)
```

```python
import functools
import math

import numpy as np
import jax
import jax.numpy as jnp
from jax import lax
from jax.experimental import pallas as pl
from jax.experimental.pallas import tpu as pltpu

F32 = jnp.float32
BF16 = jnp.bfloat16
HIGHEST = lax.Precision.HIGHEST

GROUP = 256
HEADS = 4
HDIM = 64
LORA = 32
DECAY_SCALE = math.exp(-0.5)
LNX_EPS = 64e-5
NORM_EPS = 1e-6
CONV_TAPS = 3
MOBA_BLOCK = 256
MOBA_TOPK = 3
MOBA_QBLOCK = 128
RET_CHUNK = 128
REL_BUCKETS = 32
REL_MAX_DIST = 128
RWKV_CHUNK = 64

P_COLS = 4 * 4 * GROUP + 128
LORA_COL = 4 * 4 * GROUP

VMEM_LIMIT_V7X = 48 * 1024 * 1024


def _nn(a, b, precision=None):
    return lax.dot_general(a, b, (((1,), (0,)), ((), ())), precision=precision,
                           preferred_element_type=F32)


def _nt(a, b, precision=None):
    return lax.dot_general(a, b, (((1,), (1,)), ((), ())), precision=precision,
                           preferred_element_type=F32)


def _tn(a, b, precision=None):
    return lax.dot_general(a, b, (((0,), (0,)), ((), ())), precision=precision,
                           preferred_element_type=F32)


def _split(x):
    hi = x.astype(BF16)
    lo = (x - hi.astype(F32)).astype(BF16)
    return hi, lo


def _dot3(dot, a, b):
    return dot(a[0], b[0]) + (dot(a[0], b[1]) + dot(a[1], b[0]))


def _psl(pair, rows, cols):
    return pair[0][rows, cols], pair[1][rows, cols]


def _pcat(pairs, axis):
    return (jnp.concatenate([p[0] for p in pairs], axis=axis),
            jnp.concatenate([p[1] for p in pairs], axis=axis))


def _head_sum(x, esum_bf16):
    hi = x.astype(BF16)
    r1 = x - hi.astype(F32)
    mid = r1.astype(BF16)
    lo = (r1 - mid.astype(F32)).astype(BF16)
    return _nn(hi, esum_bf16) + (_nn(mid, esum_bf16) + _nn(lo, esum_bf16))


def _iota(shape, dim):
    return lax.broadcasted_iota(jnp.int32, shape, dim)


def _head_sum_matrix():
    r = _iota((GROUP, GROUP), 0) // HDIM
    c = _iota((GROUP, GROUP), 1) // HDIM
    return jnp.where(r == c, 1.0, 0.0).astype(F32)


def _silu(x):
    return x * jax.nn.sigmoid(x)


def _inproj_kernel(x_ref, nw_ref, w_ref, o_ref):
    x = x_ref[...]
    h = x * lax.rsqrt(jnp.mean(x * x, axis=-1, keepdims=True) + NORM_EPS) * nw_ref[...]
    o_ref[...] = _nn(h.astype(BF16), w_ref[...])


def _inproj(x2d, norm_w, w_bf16, tm=256):
    m, d = x2d.shape
    n = w_bf16.shape[1]
    return pl.pallas_call(
        _inproj_kernel,
        out_shape=jax.ShapeDtypeStruct((m, n), F32),
        grid=(m // tm,),
        in_specs=[pl.BlockSpec((tm, d), lambda i: (i, 0)),
                  pl.BlockSpec((1, d), lambda i: (0, 0)),
                  pl.BlockSpec((d, n), lambda i: (0, 0))],
        out_specs=pl.BlockSpec((tm, n), lambda i: (i, 0)),
        compiler_params=pltpu.CompilerParams(
            dimension_semantics=("arbitrary",), vmem_limit_bytes=VMEM_LIMIT_V7X),
        name="inproj",
    )(x2d, norm_w.reshape(1, d), w_bf16)


def _outproj_kernel(x_ref, y0_ref, y1_ref, y2_ref, y3_ref, w_ref, fw_ref, o_ref, *, final):
    acc = x_ref[...]
    for g, y_ref in enumerate((y0_ref, y1_ref, y2_ref, y3_ref)):
        acc = acc + _nn(y_ref[...].astype(BF16), w_ref[g * GROUP:(g + 1) * GROUP, :])
    if final:
        acc = acc * lax.rsqrt(jnp.mean(acc * acc, axis=-1, keepdims=True) + NORM_EPS) * fw_ref[...]
    o_ref[...] = acc


def _outproj(x2d, ys, w_bf16, final_w, final, tm=512):
    m, d = x2d.shape
    yspec = pl.BlockSpec((tm, GROUP), lambda i: (i, 0))
    return pl.pallas_call(
        functools.partial(_outproj_kernel, final=final),
        out_shape=jax.ShapeDtypeStruct((m, d), F32),
        grid=(m // tm,),
        in_specs=[pl.BlockSpec((tm, d), lambda i: (i, 0)), yspec, yspec, yspec, yspec,
                  pl.BlockSpec((4 * GROUP, d), lambda i: (0, 0)),
                  pl.BlockSpec((1, d), lambda i: (0, 0))],
        out_specs=pl.BlockSpec((tm, d), lambda i: (i, 0)),
        compiler_params=pltpu.CompilerParams(
            dimension_semantics=("arbitrary",), vmem_limit_bytes=VMEM_LIMIT_V7X),
        name="outproj_final" if final else "outproj",
    )(x2d, *ys, w_bf16, final_w.reshape(1, d))


def _rwkv_kernel(rkv_ref, gate_ref, lora_ref, mu_ref, mul_ref, w0_ref, w2_ref, a0_ref, a2_ref,
                 kk_ref, ka_ref, rk_ref, lnw_ref, lnb_ref, o_ref,
                 state_ref, prev_ref, prevl_ref, y_ref, *, tt):
    C = RWKV_CHUNK
    t_idx = pl.program_id(1)

    @pl.when(t_idx == 0)
    def _():
        state_ref[...] = jnp.zeros_like(state_ref)
        prev_ref[...] = jnp.zeros_like(prev_ref)
        prevl_ref[...] = jnp.zeros_like(prevl_ref)

    p = rkv_ref[0]
    lo = lora_ref[0]
    row0 = _iota((tt, 1), 0) == 0
    p_sh = jnp.where(row0, prev_ref[...], pltpu.roll(p, 1, axis=0))
    lo_sh = jnp.where(row0, prevl_ref[...], pltpu.roll(lo, 1, axis=0))
    prev_ref[...] = p[tt - 1:tt, :]
    prevl_ref[...] = lo[tt - 1:tt, :]
    p = p + (p_sh - p) * mu_ref[...]
    lo = lo + (lo_sh - lo) * mul_ref[...]

    r = p[:, 0:GROUP]
    k = p[:, GROUP:2 * GROUP]
    v = p[:, 2 * GROUP:3 * GROUP]

    esum = _head_sum_matrix().astype(BF16)
    lw = -DECAY_SCALE * jax.nn.sigmoid(
        w0_ref[...] + _dot3(_nn, _split(jnp.tanh(lo)), _split(w2_ref[...])))
    rate = jax.nn.sigmoid(a0_ref[...] + _dot3(_nn, _split(lo), _split(a2_ref[...])))
    kk = k * kk_ref[...]
    kk = kk / jnp.maximum(jnp.sqrt(_head_sum(kk * kk, esum)), 1e-12)
    k2 = k * (1.0 + (rate - 1.0) * ka_ref[...])
    bonus = _head_sum(r * k2 * rk_ref[...], esum) * v
    bvec = kk * rate
    avec = -kk

    rowmod = _iota((tt, 1), 0) % C
    cum = lw
    for sh in (1, 2, 4, 8, 16, 32):
        cum = cum + jnp.where(rowmod >= sh, pltpu.roll(cum, sh, axis=0), 0.0)
    tot_rows = [cum[c * C + C - 1:c * C + C, :] for c in range(tt // C)]
    tot = jnp.concatenate([jnp.broadcast_to(tr, (C, GROUP)) for tr in tot_rows], axis=0)
    e_neg = jnp.exp(-cum)
    e_end = jnp.exp(tot - cum)
    r_t = _split(r * jnp.exp(cum))
    a_t = _split(avec * jnp.exp(cum - lw))
    k_t = _split(k2 * e_neg)
    b_t = _split(bvec * e_neg)
    k_h = _split(k2 * e_end)
    b_h = _split(bvec * e_end)
    v_s = _split(v)

    ri = _iota((2 * C, 2 * C), 0)
    cj = _iota((2 * C, 2 * C), 1) % C
    aa_mask = ((ri < C) & (cj < ri)) | ((ri >= C) & (cj <= ri - C))
    ei = _iota((C, C), 0)
    ej = _iota((C, C), 1)
    eye = jnp.where(ei == ej, 1.0, 0.0).astype(F32)

    n_chunks = tt // C
    blocks = [(c, h) for c in range(n_chunks) for h in range(HEADS)]
    cs_of = lambda c: slice(c * C, (c + 1) * C)
    hs_of = lambda h: slice(h * HDIM, (h + 1) * HDIM)
    blk = lambda pr, c, h: _psl(pr, cs_of(c), hs_of(h))

    aa_s, inv, pw = {}, {}, {}
    for c, h in blocks:
        aa = _dot3(_nt, _pcat([blk(a_t, c, h), blk(r_t, c, h)], 0),
                   _pcat([blk(b_t, c, h), blk(k_t, c, h)], 0))
        aa = jnp.where(aa_mask, aa, 0.0)
        inv[c, h] = eye + aa[0:C, 0:C]
        aa_s[c, h] = _split(aa)
    for c, h in blocks:
        ab_s = _psl(aa_s[c, h], slice(0, C), slice(0, C))
        pw[c, h] = _dot3(_nn, ab_s, ab_s)
    n_sq = int(math.log2(C)) - 1
    for it in range(n_sq):
        for c, h in blocks:
            pw_s = _split(pw[c, h])
            if it < n_sq - 1:
                tq = _dot3(_nn, _pcat([_split(inv[c, h]), pw_s], 0), pw_s)
                inv[c, h] = inv[c, h] + tq[0:C]
                pw[c, h] = tq[C:2 * C]
            else:
                inv[c, h] = inv[c, h] + _dot3(_nn, _split(inv[c, h]), pw_s)
    ta_s = {}
    for c, h in blocks:
        ta_s[c, h] = _split(_dot3(_nn, _split(inv[c, h]),
                                  _pcat([_psl(aa_s[c, h], slice(0, C), slice(C, 2 * C)), blk(a_t, c, h)], 1)))
    s_kv = [state_ref[h] for h in range(HEADS)]
    for c in range(n_chunks):
        s_s = [_split(s_kv[h]) for h in range(HEADS)]
        u_s = [_split(_dot3(_nn, ta_s[c, h], _pcat([blk(v_s, c, h), s_s[h]], 0))) for h in range(HEADS)]
        for h in range(HEADS):
            vv = blk(v_s, c, h)
            w_col = jnp.sum(eye * jnp.exp(tot_rows[c][:, hs_of(h)]), axis=1, keepdims=True)
            s_kv[h] = s_kv[h] * w_col + _dot3(_tn, _pcat([blk(k_h, c, h), blk(b_h, c, h)], 0),
                                              _pcat([vv, u_s[h]], 0))
        for h in range(HEADS):
            y_ref[cs_of(c), hs_of(h)] = _dot3(
                _nn, _pcat([_psl(aa_s[c, h], slice(C, 2 * C), slice(0, 2 * C)), blk(r_t, c, h)], 1),
                _pcat([u_s[h], blk(v_s, c, h), s_s[h]], 0))
    for h in range(HEADS):
        state_ref[h] = s_kv[h]

    y = y_ref[...]
    mean = _head_sum(y, esum) * (1.0 / HDIM)
    yc = y - mean
    var = _head_sum(yc * yc, esum) * (1.0 / HDIM)
    y = yc * lax.rsqrt(var + LNX_EPS) * lnw_ref[...] + lnb_ref[...] + bonus
    o_ref[0] = y * _silu(gate_ref[0])


def _rwkv(p3, mu, w0, w2, a0, a2, k_k, k_a, r_k, lnx_w, lnx_b, tt=256):
    b, t, _ = p3.shape
    mu_main = mu[:3 * GROUP].reshape(1, 3 * GROUP)
    mu_lora = jnp.concatenate([mu[3 * GROUP:], jnp.zeros((128 - 2 * LORA,), F32)]).reshape(1, 128)
    w2p = jnp.zeros((128, GROUP), F32).at[0:LORA].set(w2)
    a2p = jnp.zeros((128, GROUP), F32).at[LORA:2 * LORA].set(a2)
    row = lambda a: a.reshape(1, GROUP)
    vec = pl.BlockSpec((1, GROUP), lambda i, j: (0, 0))
    return pl.pallas_call(
        functools.partial(_rwkv_kernel, tt=tt),
        out_shape=jax.ShapeDtypeStruct((b, t, GROUP), F32),
        grid=(b, t // tt),
        in_specs=[pl.BlockSpec((1, tt, 3 * GROUP), lambda i, j: (i, j, 0)),
                  pl.BlockSpec((1, tt, GROUP), lambda i, j: (i, j, 3)),
                  pl.BlockSpec((1, tt, 128), lambda i, j: (i, j, LORA_COL // 128)),
                  pl.BlockSpec((1, 3 * GROUP), lambda i, j: (0, 0)),
                  pl.BlockSpec((1, 128), lambda i, j: (0, 0)),
                  vec,
                  pl.BlockSpec((128, GROUP), lambda i, j: (0, 0)),
                  vec,
                  pl.BlockSpec((128, GROUP), lambda i, j: (0, 0)),
                  vec, vec, vec, vec, vec],
        out_specs=pl.BlockSpec((1, tt, GROUP), lambda i, j: (i, j, 0)),
        scratch_shapes=[pltpu.VMEM((HEADS, HDIM, HDIM), F32),
                        pltpu.VMEM((1, 3 * GROUP), F32),
                        pltpu.VMEM((1, 128), F32),
                        pltpu.VMEM((tt, GROUP), F32)],
        compiler_params=pltpu.CompilerParams(
            dimension_semantics=("arbitrary", "arbitrary"), vmem_limit_bytes=VMEM_LIMIT_V7X),
        name="rwkv7",
    )(p3, p3, p3, mu_main, mu_lora, row(w0), w2p, row(a0), a2p,
      row(k_k), row(k_a), row(r_k), row(lnx_w), row(lnx_b))


def _conv_kernel(p_ref, cw_ref, o_ref, tail_ref, *, tt):
    @pl.when(pl.program_id(1) == 0)
    def _():
        tail_ref[...] = jnp.zeros_like(tail_ref)

    p = p_ref[0]
    u = p[:, GROUP:2 * GROUP] * p[:, 2 * GROUP:3 * GROUP]
    rows = _iota((tt, 1), 0)
    u1 = jnp.where(rows == 0, tail_ref[1:2, :], pltpu.roll(u, 1, axis=0))
    u2 = jnp.where(rows == 0, tail_ref[0:1, :],
                   jnp.where(rows == 1, tail_ref[1:2, :], pltpu.roll(u, 2, axis=0)))
    tail_ref[0:2, :] = u[tt - 2:tt, :]
    y = u2 * cw_ref[0:1, :] + u1 * cw_ref[1:2, :] + u * cw_ref[2:3, :]
    o_ref[0] = p[:, 0:GROUP] * y * _silu(p[:, 3 * GROUP:4 * GROUP])


def _conv(p3, conv_w, tt=512):
    b, t, _ = p3.shape
    return pl.pallas_call(
        functools.partial(_conv_kernel, tt=tt),
        out_shape=jax.ShapeDtypeStruct((b, t, GROUP), F32),
        grid=(b, t // tt),
        in_specs=[pl.BlockSpec((1, tt, 4 * GROUP), lambda i, j: (i, j, 1)),
                  pl.BlockSpec((CONV_TAPS, GROUP), lambda i, j: (0, 0))],
        out_specs=pl.BlockSpec((1, tt, GROUP), lambda i, j: (i, j, 0)),
        scratch_shapes=[pltpu.VMEM((8, GROUP), F32)],
        compiler_params=pltpu.CompilerParams(
            dimension_semantics=("arbitrary", "arbitrary"), vmem_limit_bytes=VMEM_LIMIT_V7X),
        name="sconv",
    )(p3, conv_w)


def _t5_bucket_np(dist):
    max_exact = REL_BUCKETS // 2
    d_f = np.maximum(dist, 1).astype(np.float32)
    large = max_exact + (np.log(d_f / np.float32(max_exact)) / np.float32(math.log(REL_MAX_DIST / max_exact))
                         * np.float32(REL_BUCKETS - max_exact)).astype(np.int32)
    large = np.minimum(large, REL_BUCKETS - 1)
    return np.where(dist < max_exact, dist, large).astype(np.int32)


def _moba_bucket_table():
    tab = np.zeros((2, MOBA_QBLOCK, 3 * MOBA_BLOCK), np.int32)
    rows = np.arange(MOBA_QBLOCK)[:, None]
    cols = np.arange(MOBA_BLOCK)[None, :]
    for half in range(2):
        for j, delta in enumerate((2, 1, 0)):
            dist = half * MOBA_QBLOCK + rows + delta * MOBA_BLOCK - cols
            tab[half, :, j * MOBA_BLOCK:(j + 1) * MOBA_BLOCK] = _t5_bucket_np(np.maximum(dist, 0))
    return tab


def _bias_kernel(idx_ref, rb_ref, o_ref):
    h = pl.program_id(0)
    idx = idx_ref[0]
    acc = jnp.zeros(idx.shape, F32)
    for bkt in range(REL_BUCKETS):
        acc = jnp.where(idx == bkt, rb_ref[bkt, h], acc)
    o_ref[0, 0] = acc


def _moba_bias_tiles(rel_bias):
    idx = jnp.asarray(_moba_bucket_table())
    shp = (MOBA_QBLOCK, 3 * MOBA_BLOCK)
    return pl.pallas_call(
        _bias_kernel,
        out_shape=jax.ShapeDtypeStruct((HEADS, 2) + shp, F32),
        grid=(HEADS, 2),
        in_specs=[pl.BlockSpec((1,) + shp, lambda h, j: (j, 0, 0)),
                  pl.BlockSpec(memory_space=pltpu.SMEM)],
        out_specs=pl.BlockSpec((1, 1) + shp, lambda h, j: (h, j, 0, 0)),
        name="moba_bias",
    )(idx, rel_bias)


def _moba_kernel(q_ref, k_ref, v_ref, gate_ref, bias_ref, o_ref, kmean_ref, s_ref, y_ref, *, nb):
    QB, BLK = MOBA_QBLOCK, MOBA_BLOCK
    qb = pl.program_id(1)
    ib = qb // (BLK // QB)

    @pl.when(qb == 0)
    def _():
        for n in range(nb):
            kmean_ref[n:n + 1, :] = jnp.mean(k_ref[0, n * BLK:(n + 1) * BLK, :], axis=0, keepdims=True)

    q = q_ref[0] * (HDIM ** -0.5)
    lane = _iota((QB, nb), 1)
    neg_inf = -jnp.inf
    ib_blk = jnp.full((QB, BLK), ib, jnp.int32)
    causal = _iota((QB, BLK), 1) <= (qb % (BLK // QB)) * QB + _iota((QB, BLK), 0)

    for h in range(HEADS):
        hs = slice(h * HDIM, (h + 1) * HDIM)
        qh = q[:, hs]
        g = _nt(qh, kmean_ref[:, hs], HIGHEST)
        g = jnp.where(lane < ib, g, neg_inf)
        sel = jnp.zeros((QB, nb), F32)
        for _ in range(min(MOBA_TOPK, nb)):
            m = jnp.max(g, axis=1, keepdims=True)
            hit = (g == m) & (m > neg_inf)
            first = jnp.min(jnp.where(hit, lane, nb), axis=1, keepdims=True)
            pick = lane == first
            sel = jnp.where(pick, 1.0, sel)
            g = jnp.where(pick, neg_inf, g)
        for n in range(nb):
            s = _nt(qh, k_ref[0, n * BLK:(n + 1) * BLK, hs])
            delta = ib_blk - n
            bias = jnp.where(delta == 0, bias_ref[h, 0, :, 2 * BLK:3 * BLK],
                             jnp.where(delta == 1, bias_ref[h, 0, :, BLK:2 * BLK],
                                       bias_ref[h, 0, :, 0:BLK]))
            picked = jnp.broadcast_to(sel[:, n:n + 1], (QB, BLK)) > 0.5
            ok = ((delta > 0) & picked) | ((delta == 0) & causal)
            s_ref[:, n * BLK:(n + 1) * BLK] = jnp.where(ok, s + bias, neg_inf)
        s = s_ref[...]
        e = jnp.exp(s - jnp.max(s, axis=1, keepdims=True))
        prob = e / jnp.sum(e, axis=1, keepdims=True)
        y_ref[:, hs] = _nn(prob, v_ref[0, :, hs])
    o_ref[0] = y_ref[...] * _silu(gate_ref[0])


def _moba(p3, bias_tiles):
    b, t, _ = p3.shape
    nb = t // MOBA_BLOCK
    nqb = t // MOBA_QBLOCK
    base = 2 * 4
    bias_shape = (MOBA_QBLOCK, 3 * MOBA_BLOCK)
    return pl.pallas_call(
        functools.partial(_moba_kernel, nb=nb),
        out_shape=jax.ShapeDtypeStruct((b, t, GROUP), F32),
        grid=(b, nqb),
        in_specs=[pl.BlockSpec((1, MOBA_QBLOCK, GROUP), lambda i, j: (i, j, base)),
                  pl.BlockSpec((1, t, GROUP), lambda i, j: (i, 0, base + 1)),
                  pl.BlockSpec((1, t, GROUP), lambda i, j: (i, 0, base + 2)),
                  pl.BlockSpec((1, MOBA_QBLOCK, GROUP), lambda i, j: (i, j, base + 3)),
                  pl.BlockSpec((HEADS, 1) + bias_shape, lambda i, j: (0, j % 2, 0, 0))],
        out_specs=pl.BlockSpec((1, MOBA_QBLOCK, GROUP), lambda i, j: (i, j, 0)),
        scratch_shapes=[pltpu.VMEM((nb, GROUP), F32),
                        pltpu.VMEM((MOBA_QBLOCK, t), F32),
                        pltpu.VMEM((MOBA_QBLOCK, GROUP), F32)],
        compiler_params=pltpu.CompilerParams(
            dimension_semantics=("arbitrary", "arbitrary"), vmem_limit_bytes=VMEM_LIMIT_V7X),
        name="moba",
    )(p3, p3, p3, p3, bias_tiles)


def _ret_tables(t):
    half = HDIM // 2
    theta = 1.0 / (10000.0 ** jnp.linspace(0.0, 1.0, half))
    pos = jnp.arange(t, dtype=F32)
    ang = pos[:, None] * theta[None, :]
    sin, cos = jnp.sin(ang), jnp.cos(ang)
    cos2 = jnp.tile(jnp.repeat(cos, 2, axis=1), (1, HEADS))
    sin2 = jnp.tile(jnp.stack([-sin, sin], axis=-1).reshape(t, HDIM), (1, HEADS))
    log_gamma = jnp.log(1.0 - 2.0 ** (-5.0 - jnp.arange(HEADS, dtype=F32)))
    idx = jnp.arange(RET_CHUNK, dtype=F32)
    diff = idx[:, None] - idx[None, :]
    decay_intra = jnp.where(diff >= 0, jnp.exp(log_gamma[:, None, None] * jnp.maximum(diff, 0.0)), 0.0)
    q_decay = jnp.exp(log_gamma[:, None] * (idx + 1.0))
    k_decay = jnp.exp(log_gamma[:, None] * (RET_CHUNK - 1.0 - idx))
    chunk_decay = jnp.exp(log_gamma * RET_CHUNK)
    lanes = lambda a: jnp.repeat(a.T, HDIM, axis=1)
    cd = jnp.repeat(chunk_decay, HDIM).reshape(1, GROUP)
    return cos2, sin2, decay_intra, lanes(q_decay), lanes(k_decay), cd


def _ret_kernel(q_ref, k_ref, v_ref, gate_ref, cos_ref, sin_ref, di_ref, qd_ref, kd_ref, cd_ref,
                o_ref, state_ref, y_ref):
    @pl.when(pl.program_id(1) == 0)
    def _():
        state_ref[...] = jnp.zeros_like(state_ref)

    C = RET_CHUNK
    even = (_iota((C, GROUP), 1) % 2) == 0

    def rotate(x):
        swapped = jnp.where(even, pltpu.roll(x, GROUP - 1, axis=1), pltpu.roll(x, 1, axis=1))
        return x * cos_ref[...] + swapped * sin_ref[...]

    q = rotate(q_ref[0])
    k = rotate(k_ref[0]) * (HDIM ** -0.5)
    v = v_ref[0]
    qd = q * qd_ref[...]
    kd = k * kd_ref[...]
    cd = cd_ref[...]
    for h in range(HEADS):
        hs = slice(h * HDIM, (h + 1) * HDIM)
        state = state_ref[h]
        inner = _nt(q[:, hs], k[:, hs], HIGHEST) * di_ref[h]
        y = _nn(inner, v[:, hs], HIGHEST) + _nn(qd[:, hs], state, HIGHEST)
        state_ref[h] = state * cd[0:1, h * HDIM:h * HDIM + 1] + _tn(kd[:, hs], v[:, hs], HIGHEST)
        y_ref[:, hs] = y * lax.rsqrt(jnp.mean(y * y, axis=-1, keepdims=True) + NORM_EPS)
    o_ref[0] = y_ref[...] * _silu(gate_ref[0])


def _ret(p3, tables):
    b, t, _ = p3.shape
    C = RET_CHUNK
    base = 3 * 4
    cos2, sin2, di, qd, kd, cd = tables
    blk = lambda c: pl.BlockSpec((1, C, GROUP), lambda i, j, c=c: (i, j, c))
    full2 = lambda a: pl.BlockSpec(a.shape, lambda i, j: (0, 0))
    return pl.pallas_call(
        _ret_kernel,
        out_shape=jax.ShapeDtypeStruct((b, t, GROUP), F32),
        grid=(b, t // C),
        in_specs=[blk(base), blk(base + 1), blk(base + 2), blk(base + 3),
                  pl.BlockSpec((C, GROUP), lambda i, j: (j, 0)),
                  pl.BlockSpec((C, GROUP), lambda i, j: (j, 0)),
                  pl.BlockSpec(di.shape, lambda i, j: (0, 0, 0)),
                  full2(qd), full2(kd), full2(cd)],
        out_specs=pl.BlockSpec((1, C, GROUP), lambda i, j: (i, j, 0)),
        scratch_shapes=[pltpu.VMEM((HEADS, HDIM, HDIM), F32),
                        pltpu.VMEM((C, GROUP), F32)],
        compiler_params=pltpu.CompilerParams(
            dimension_semantics=("arbitrary", "arbitrary"), vmem_limit_bytes=VMEM_LIMIT_V7X),
        name="retnet",
    )(p3, p3, p3, p3, cos2, sin2, di, qd, kd, cd)


def _reorder_w_in(w):
    g3 = 3 * GROUP
    lora = w[:, g3:g3 + 2 * LORA]
    pad = jnp.zeros((w.shape[0], 128 - 2 * LORA), w.dtype)
    return jnp.concatenate([w[:, :g3], w[:, g3 + 2 * LORA:], lora, pad], axis=1)


def kernel(x, norm_w, w_in, w_out, rwkv_mu, rwkv_w0, rwkv_w2, rwkv_a0, rwkv_a2, rwkv_k_k, rwkv_k_a,
           rwkv_r_k, rwkv_lnx_w, rwkv_lnx_b, conv_w, rel_bias, final_norm_w):
    b, t, d = x.shape
    depth = w_in.shape[0]
    bias_tiles = _moba_bias_tiles(rel_bias)
    ret_tables = _ret_tables(t)
    x2 = x.reshape(b * t, d)
    for l in range(depth):
        w_l = _reorder_w_in(w_in[l]).astype(BF16)
        p3 = _inproj(x2, norm_w[l], w_l).reshape(b, t, P_COLS)
        y_rwkv = _rwkv(p3, rwkv_mu[l], rwkv_w0[l], rwkv_w2[l], rwkv_a0[l], rwkv_a2[l], rwkv_k_k[l],
                       rwkv_k_a[l], rwkv_r_k[l].reshape(GROUP), rwkv_lnx_w[l], rwkv_lnx_b[l])
        y_conv = _conv(p3, conv_w[l])
        y_moba = _moba(p3, bias_tiles)
        y_ret = _ret(p3, ret_tables)
        ys = [y.reshape(b * t, GROUP) for y in (y_rwkv, y_conv, y_moba, y_ret)]
        x2 = _outproj(x2, ys, w_out[l].astype(BF16), final_norm_w, final=(l == depth - 1))
    return x2.reshape(b, t, d)
```

```python
import functools
import math

import numpy as np
import jax
import jax.numpy as jnp
from jax import lax
from jax.experimental import pallas as pl
from jax.experimental.pallas import tpu as pltpu

F32 = jnp.float32
BF16 = jnp.bfloat16
HIGHEST = lax.Precision.HIGHEST

GROUP = 256
HEADS = 4
HDIM = 64
LORA = 32
DECAY_SCALE = math.exp(-0.5)
LNX_EPS = 64e-5
NORM_EPS = 1e-6
CONV_TAPS = 3
MOBA_BLOCK = 256
MOBA_TOPK = 3
MOBA_QBLOCK = 128
RET_CHUNK = 128
REL_BUCKETS = 32
REL_MAX_DIST = 128
RWKV_CHUNK = 64

P_COLS = 4 * 4 * GROUP + 128
LORA_COL = 4 * 4 * GROUP

VMEM_LIMIT_V7X = 48 * 1024 * 1024


def _nn(a, b, precision=None):
    return lax.dot_general(a, b, (((1,), (0,)), ((), ())), precision=precision,
                           preferred_element_type=F32)


def _nt(a, b, precision=None):
    return lax.dot_general(a, b, (((1,), (1,)), ((), ())), precision=precision,
                           preferred_element_type=F32)


def _tn(a, b, precision=None):
    return lax.dot_general(a, b, (((0,), (0,)), ((), ())), precision=precision,
                           preferred_element_type=F32)


def _split(x):
    hi = x.astype(BF16)
    lo = (x - hi.astype(F32)).astype(BF16)
    return hi, lo


def _dot3(dot, a, b):
    return dot(a[0], b[0]) + (dot(a[0], b[1]) + dot(a[1], b[0]))


def _psl(pair, rows, cols):
    return pair[0][rows, cols], pair[1][rows, cols]


def _pcat(pairs, axis):
    return (jnp.concatenate([p[0] for p in pairs], axis=axis),
            jnp.concatenate([p[1] for p in pairs], axis=axis))


def _head_sum(x, esum_bf16):
    hi = x.astype(BF16)
    r1 = x - hi.astype(F32)
    mid = r1.astype(BF16)
    lo = (r1 - mid.astype(F32)).astype(BF16)
    return _nn(hi, esum_bf16) + (_nn(mid, esum_bf16) + _nn(lo, esum_bf16))


def _iota(shape, dim):
    return lax.broadcasted_iota(jnp.int32, shape, dim)


def _head_sum_matrix():
    r = _iota((GROUP, GROUP), 0) // HDIM
    c = _iota((GROUP, GROUP), 1) // HDIM
    return jnp.where(r == c, 1.0, 0.0).astype(F32)


def _silu(x):
    return x * jax.nn.sigmoid(x)


def _inproj_kernel(x_ref, nw_ref, w_ref, o_ref):
    x = x_ref[...]
    h = x * lax.rsqrt(jnp.mean(x * x, axis=-1, keepdims=True) + NORM_EPS) * nw_ref[...]
    o_ref[...] = _nn(h.astype(BF16), w_ref[...])


def _inproj(x2d, norm_w, w_bf16, tm=256):
    m, d = x2d.shape
    n = w_bf16.shape[1]
    return pl.pallas_call(
        _inproj_kernel,
        out_shape=jax.ShapeDtypeStruct((m, n), F32),
        grid=(m // tm,),
        in_specs=[pl.BlockSpec((tm, d), lambda i: (i, 0)),
                  pl.BlockSpec((1, d), lambda i: (0, 0)),
                  pl.BlockSpec((d, n), lambda i: (0, 0))],
        out_specs=pl.BlockSpec((tm, n), lambda i: (i, 0)),
        compiler_params=pltpu.CompilerParams(
            dimension_semantics=("arbitrary",), vmem_limit_bytes=VMEM_LIMIT_V7X),
        name="inproj",
    )(x2d, norm_w.reshape(1, d), w_bf16)


def _outproj_kernel(x_ref, y0_ref, y1_ref, y2_ref, y3_ref, w_ref, fw_ref, o_ref, *, final):
    acc = x_ref[...]
    for g, y_ref in enumerate((y0_ref, y1_ref, y2_ref, y3_ref)):
        acc = acc + _nn(y_ref[...].astype(BF16), w_ref[g * GROUP:(g + 1) * GROUP, :])
    if final:
        acc = acc * lax.rsqrt(jnp.mean(acc * acc, axis=-1, keepdims=True) + NORM_EPS) * fw_ref[...]
    o_ref[...] = acc


def _outproj(x2d, ys, w_bf16, final_w, final, tm=512):
    m, d = x2d.shape
    yspec = pl.BlockSpec((tm, GROUP), lambda i: (i, 0))
    return pl.pallas_call(
        functools.partial(_outproj_kernel, final=final),
        out_shape=jax.ShapeDtypeStruct((m, d), F32),
        grid=(m // tm,),
        in_specs=[pl.BlockSpec((tm, d), lambda i: (i, 0)), yspec, yspec, yspec, yspec,
                  pl.BlockSpec((4 * GROUP, d), lambda i: (0, 0)),
                  pl.BlockSpec((1, d), lambda i: (0, 0))],
        out_specs=pl.BlockSpec((tm, d), lambda i: (i, 0)),
        compiler_params=pltpu.CompilerParams(
            dimension_semantics=("arbitrary",), vmem_limit_bytes=VMEM_LIMIT_V7X),
        name="outproj_final" if final else "outproj",
    )(x2d, *ys, w_bf16, final_w.reshape(1, d))


def _rwkv_kernel(rkv_ref, gate_ref, lora_ref, mu_ref, mul_ref, w0_ref, w2_ref, a0_ref, a2_ref,
                 kk_ref, ka_ref, rk_ref, lnw_ref, lnb_ref, o_ref,
                 state_ref, prev_ref, prevl_ref, y_ref, *, tt):
    C = RWKV_CHUNK
    t_idx = pl.program_id(1)

    @pl.when(t_idx == 0)
    def _():
        state_ref[...] = jnp.zeros_like(state_ref)
        prev_ref[...] = jnp.zeros_like(prev_ref)
        prevl_ref[...] = jnp.zeros_like(prevl_ref)

    p = rkv_ref[0]
    lo = lora_ref[0]
    row0 = _iota((tt, 1), 0) == 0
    p_sh = jnp.where(row0, prev_ref[...], pltpu.roll(p, 1, axis=0))
    lo_sh = jnp.where(row0, prevl_ref[...], pltpu.roll(lo, 1, axis=0))
    prev_ref[...] = p[tt - 1:tt, :]
    prevl_ref[...] = lo[tt - 1:tt, :]
    p = p + (p_sh - p) * mu_ref[...]
    lo = lo + (lo_sh - lo) * mul_ref[...]

    r = p[:, 0:GROUP]
    k = p[:, GROUP:2 * GROUP]
    v = p[:, 2 * GROUP:3 * GROUP]

    esum = _head_sum_matrix().astype(BF16)
    lw = -DECAY_SCALE * jax.nn.sigmoid(
        w0_ref[...] + _dot3(_nn, _split(jnp.tanh(lo)), _split(w2_ref[...])))
    rate = jax.nn.sigmoid(a0_ref[...] + _dot3(_nn, _split(lo), _split(a2_ref[...])))
    kk = k * kk_ref[...]
    kk = kk / jnp.maximum(jnp.sqrt(_head_sum(kk * kk, esum)), 1e-12)
    k2 = k * (1.0 + (rate - 1.0) * ka_ref[...])
    bonus = _head_sum(r * k2 * rk_ref[...], esum) * v
    bvec = kk * rate
    avec = -kk

    rowmod = _iota((tt, 1), 0) % C
    cum = lw
    for sh in (1, 2, 4, 8, 16, 32):
        cum = cum + jnp.where(rowmod >= sh, pltpu.roll(cum, sh, axis=0), 0.0)
    tot_rows = [cum[c * C + C - 1:c * C + C, :] for c in range(tt // C)]
    tot = jnp.concatenate([jnp.broadcast_to(tr, (C, GROUP)) for tr in tot_rows], axis=0)
    e_neg = jnp.exp(-cum)
    e_end = jnp.exp(tot - cum)
    r_t = _split(r * jnp.exp(cum))
    a_t = _split(avec * jnp.exp(cum - lw))
    k_t = _split(k2 * e_neg)
    b_t = _split(bvec * e_neg)
    k_h = _split(k2 * e_end)
    b_h = _split(bvec * e_end)
    v_s = _split(v)

    ri = _iota((2 * C, 2 * C), 0)
    cj = _iota((2 * C, 2 * C), 1) % C
    aa_mask = ((ri < C) & (cj < ri)) | ((ri >= C) & (cj <= ri - C))
    ei = _iota((C, C), 0)
    ej = _iota((C, C), 1)
    eye = jnp.where(ei == ej, 1.0, 0.0).astype(F32)

    n_chunks = tt // C
    blocks = [(c, h) for c in range(n_chunks) for h in range(HEADS)]
    cs_of = lambda c: slice(c * C, (c + 1) * C)
    hs_of = lambda h: slice(h * HDIM, (h + 1) * HDIM)
    blk = lambda pr, c, h: _psl(pr, cs_of(c), hs_of(h))

    aa_s, inv, pw = {}, {}, {}
    for c, h in blocks:
        aa = _dot3(_nt, _pcat([blk(a_t, c, h), blk(r_t, c, h)], 0),
                   _pcat([blk(b_t, c, h), blk(k_t, c, h)], 0))
        aa = jnp.where(aa_mask, aa, 0.0)
        inv[c, h] = eye + aa[0:C, 0:C]
        aa_s[c, h] = _split(aa)
    for c, h in blocks:
        ab_s = _psl(aa_s[c, h], slice(0, C), slice(0, C))
        pw[c, h] = _dot3(_nn, ab_s, ab_s)
    n_sq = int(math.log2(C)) - 1
    for it in range(n_sq):
        for c, h in blocks:
            pw_s = _split(pw[c, h])
            if it < n_sq - 1:
                tq = _dot3(_nn, _pcat([_split(inv[c, h]), pw_s], 0), pw_s)
                inv[c, h] = inv[c, h] + tq[0:C]
                pw[c, h] = tq[C:2 * C]
            else:
                inv[c, h] = inv[c, h] + _dot3(_nn, _split(inv[c, h]), pw_s)
    ta_s = {}
    for c, h in blocks:
        ta_s[c, h] = _split(_dot3(_nn, _split(inv[c, h]),
                                  _pcat([_psl(aa_s[c, h], slice(0, C), slice(C, 2 * C)), blk(a_t, c, h)], 1)))
    s_kv = [state_ref[h] for h in range(HEADS)]
    for c in range(n_chunks):
        s_s = [_split(s_kv[h]) for h in range(HEADS)]
        u_s = [_split(_dot3(_nn, ta_s[c, h], _pcat([blk(v_s, c, h), s_s[h]], 0))) for h in range(HEADS)]
        for h in range(HEADS):
            vv = blk(v_s, c, h)
            w_col = jnp.sum(eye * jnp.exp(tot_rows[c][:, hs_of(h)]), axis=1, keepdims=True)
            s_kv[h] = s_kv[h] * w_col + _dot3(_tn, _pcat([blk(k_h, c, h), blk(b_h, c, h)], 0),
                                              _pcat([vv, u_s[h]], 0))
        for h in range(HEADS):
            y_ref[cs_of(c), hs_of(h)] = _dot3(
                _nn, _pcat([_psl(aa_s[c, h], slice(C, 2 * C), slice(0, 2 * C)), blk(r_t, c, h)], 1),
                _pcat([u_s[h], blk(v_s, c, h), s_s[h]], 0))
    for h in range(HEADS):
        state_ref[h] = s_kv[h]

    y = y_ref[...]
    mean = _head_sum(y, esum) * (1.0 / HDIM)
    yc = y - mean
    var = _head_sum(yc * yc, esum) * (1.0 / HDIM)
    y = yc * lax.rsqrt(var + LNX_EPS) * lnw_ref[...] + lnb_ref[...] + bonus
    o_ref[0] = y * _silu(gate_ref[0])


def _rwkv(p3, mu, w0, w2, a0, a2, k_k, k_a, r_k, lnx_w, lnx_b, tt=256):
    b, t, _ = p3.shape
    mu_main = mu[:3 * GROUP].reshape(1, 3 * GROUP)
    mu_lora = jnp.concatenate([mu[3 * GROUP:], jnp.zeros((128 - 2 * LORA,), F32)]).reshape(1, 128)
    w2p = jnp.zeros((128, GROUP), F32).at[0:LORA].set(w2)
    a2p = jnp.zeros((128, GROUP), F32).at[LORA:2 * LORA].set(a2)
    row = lambda a: a.reshape(1, GROUP)
    vec = pl.BlockSpec((1, GROUP), lambda i, j: (0, 0))
    return pl.pallas_call(
        functools.partial(_rwkv_kernel, tt=tt),
        out_shape=jax.ShapeDtypeStruct((b, t, GROUP), F32),
        grid=(b, t // tt),
        in_specs=[pl.BlockSpec((1, tt, 3 * GROUP), lambda i, j: (i, j, 0)),
                  pl.BlockSpec((1, tt, GROUP), lambda i, j: (i, j, 3)),
                  pl.BlockSpec((1, tt, 128), lambda i, j: (i, j, LORA_COL // 128)),
                  pl.BlockSpec((1, 3 * GROUP), lambda i, j: (0, 0)),
                  pl.BlockSpec((1, 128), lambda i, j: (0, 0)),
                  vec,
                  pl.BlockSpec((128, GROUP), lambda i, j: (0, 0)),
                  vec,
                  pl.BlockSpec((128, GROUP), lambda i, j: (0, 0)),
                  vec, vec, vec, vec, vec],
        out_specs=pl.BlockSpec((1, tt, GROUP), lambda i, j: (i, j, 0)),
        scratch_shapes=[pltpu.VMEM((HEADS, HDIM, HDIM), F32),
                        pltpu.VMEM((1, 3 * GROUP), F32),
                        pltpu.VMEM((1, 128), F32),
                        pltpu.VMEM((tt, GROUP), F32)],
        compiler_params=pltpu.CompilerParams(
            dimension_semantics=("arbitrary", "arbitrary"), vmem_limit_bytes=VMEM_LIMIT_V7X),
        name="rwkv7",
    )(p3, p3, p3, mu_main, mu_lora, row(w0), w2p, row(a0), a2p,
      row(k_k), row(k_a), row(r_k), row(lnx_w), row(lnx_b))


def _conv_kernel(p_ref, cw_ref, o_ref, tail_ref, *, tt):
    @pl.when(pl.program_id(1) == 0)
    def _():
        tail_ref[...] = jnp.zeros_like(tail_ref)

    p = p_ref[0]
    u = p[:, GROUP:2 * GROUP] * p[:, 2 * GROUP:3 * GROUP]
    rows = _iota((tt, 1), 0)
    u1 = jnp.where(rows == 0, tail_ref[1:2, :], pltpu.roll(u, 1, axis=0))
    u2 = jnp.where(rows == 0, tail_ref[0:1, :],
                   jnp.where(rows == 1, tail_ref[1:2, :], pltpu.roll(u, 2, axis=0)))
    tail_ref[0:2, :] = u[tt - 2:tt, :]
    y = u2 * cw_ref[0:1, :] + u1 * cw_ref[1:2, :] + u * cw_ref[2:3, :]
    o_ref[0] = p[:, 0:GROUP] * y * _silu(p[:, 3 * GROUP:4 * GROUP])


def _conv(p3, conv_w, tt=512):
    b, t, _ = p3.shape
    return pl.pallas_call(
        functools.partial(_conv_kernel, tt=tt),
        out_shape=jax.ShapeDtypeStruct((b, t, GROUP), F32),
        grid=(b, t // tt),
        in_specs=[pl.BlockSpec((1, tt, 4 * GROUP), lambda i, j: (i, j, 1)),
                  pl.BlockSpec((CONV_TAPS, GROUP), lambda i, j: (0, 0))],
        out_specs=pl.BlockSpec((1, tt, GROUP), lambda i, j: (i, j, 0)),
        scratch_shapes=[pltpu.VMEM((8, GROUP), F32)],
        compiler_params=pltpu.CompilerParams(
            dimension_semantics=("arbitrary", "arbitrary"), vmem_limit_bytes=VMEM_LIMIT_V7X),
        name="sconv",
    )(p3, conv_w)


def _t5_bucket_np(dist):
    max_exact = REL_BUCKETS // 2
    d_f = np.maximum(dist, 1).astype(np.float32)
    large = max_exact + (np.log(d_f / np.float32(max_exact)) / np.float32(math.log(REL_MAX_DIST / max_exact))
                         * np.float32(REL_BUCKETS - max_exact)).astype(np.int32)
    large = np.minimum(large, REL_BUCKETS - 1)
    return np.where(dist < max_exact, dist, large).astype(np.int32)


MASKED_BUCKET = REL_BUCKETS


def _moba_bucket_table():
    keys = np.arange(MOBA_BLOCK)[:, None]
    queries = np.arange(MOBA_BLOCK)[None, :]
    prev = _t5_bucket_np(queries + MOBA_BLOCK - keys)
    own = np.where(keys <= queries, _t5_bucket_np(np.maximum(queries - keys, 0)), MASKED_BUCKET)
    return np.stack([prev, own]).astype(np.int32)


def _bias_kernel(idx_ref, rb_ref, o_ref):
    h = pl.program_id(0)
    idx = idx_ref[...]
    acc = jnp.full(idx.shape, -jnp.inf, F32)
    for bkt in range(REL_BUCKETS):
        acc = jnp.where(idx == bkt, rb_ref[bkt, h], acc)
    o_ref[0] = acc


def _moba_bias_tiles(rel_bias):
    idx = jnp.asarray(_moba_bucket_table())
    shp = (2, MOBA_BLOCK, MOBA_BLOCK)
    return pl.pallas_call(
        _bias_kernel,
        out_shape=jax.ShapeDtypeStruct((HEADS,) + shp, F32),
        grid=(HEADS,),
        in_specs=[pl.BlockSpec(shp, lambda h: (0, 0, 0)),
                  pl.BlockSpec(memory_space=pltpu.SMEM)],
        out_specs=pl.BlockSpec((1,) + shp, lambda h: (h, 0, 0, 0)),
        name="moba_bias",
    )(idx, rel_bias)


def _moba_kernel(q_ref, k_ref, v_ref, gate_ref, bias_ref, rb_ref, o_ref,
                 kmean_ref, kbf_ref, vt_ref, sel_ref, m_ref, l_ref, acc_ref, *, nb):
    BLK = MOBA_BLOCK
    ib = pl.program_id(1)
    heads = range(HEADS)
    rows_of = lambda h: slice(h * HDIM, (h + 1) * HDIM)
    neg_inf = -jnp.inf
    not_selected = -1e30

    @pl.when(ib == 0)
    def _():
        kmean_ref[...] = jnp.zeros_like(kmean_ref)
        for n in range(nb):
            blk = slice(n * BLK, (n + 1) * BLK)
            kblk = k_ref[0, blk, :]
            kmean_ref[n:n + 1, :] = jnp.mean(kblk, axis=0, keepdims=True)
            kbf_ref[n] = kblk.astype(BF16)
            vt_ref[n, 0:GROUP, :] = v_ref[0, blk, :].T.astype(BF16)
            vt_ref[n, GROUP:GROUP + 16, :] = jnp.ones((16, BLK), BF16)

    q = q_ref[0] * (HDIM ** -0.5)
    q_bf = q.astype(BF16)
    q_s = _split(q)
    lane_head = _iota((1, GROUP), 1) // HDIM
    q_heads = [jnp.where(lane_head == h, q_bf, jnp.zeros_like(q_bf)) for h in heads]

    km = kmean_ref[...]
    gates = [_dot3(_nt, _split(jnp.where(lane_head == h, km, 0.0)), q_s) for h in heads]
    blk_id = _iota((16, BLK), 0)
    for h in heads:
        g = jnp.where(blk_id < ib, gates[h], neg_inf)
        sel = jnp.zeros((16, BLK), F32)
        for _ in range(min(MOBA_TOPK, nb)):
            m = jnp.max(g, axis=0, keepdims=True)
            hit = (g == m) & (m > neg_inf)
            first = jnp.min(jnp.where(hit, blk_id, 16), axis=0, keepdims=True)
            pick = blk_id == first
            sel = jnp.where(pick, 1.0, sel)
            g = jnp.where(pick, neg_inf, g)
        sel_ref[h] = sel

    def attend(k_blk, vt_blk, extra, first):
        scores = [_nt(k_blk, q_heads[h]) for h in heads]
        probs, alphas = [], []
        for h in heads:
            s = scores[h] + extra[h]
            m_blk = jnp.max(s, axis=0, keepdims=True)
            if first:
                m_new = m_blk
            else:
                m_old = m_ref[h]
                m_new = jnp.maximum(m_old, m_blk)
                alphas.append(jnp.exp(m_old - m_new))
            m_ref[h] = m_new
            probs.append(jnp.exp(s - m_new).astype(BF16))
        for h in heads:
            lhs = jnp.concatenate([vt_blk[rows_of(h), :], vt_blk[GROUP:GROUP + 16, :]], axis=0)
            pv = _nn(lhs, probs[h])
            if first:
                acc_ref[rows_of(h), :] = pv[0:HDIM]
                l_ref[h] = pv[HDIM:HDIM + 1]
            else:
                acc_ref[rows_of(h), :] = alphas[h] * acc_ref[rows_of(h), :] + pv[0:HDIM]
                l_ref[h] = alphas[h] * l_ref[h] + pv[HDIM:HDIM + 1]

    attend(kbf_ref[ib], vt_ref[ib], [bias_ref[h, 1] for h in heads], first=True)

    def selected_row(h, n):
        return jnp.where(sel_ref[h, pl.ds(n, 1), :] > 0.5, 0.0, not_selected)

    @pl.when(ib >= 1)
    def _():
        attend(kbf_ref[ib - 1], vt_ref[ib - 1],
               [bias_ref[h, 0] + selected_row(h, ib - 1) for h in heads], first=False)

    for n in range(nb - 2):
        @pl.when(n < ib - 1)
        def _(n=n):
            attend(kbf_ref[n], vt_ref[n],
                   [selected_row(h, n) + rb_ref[REL_BUCKETS - 1, h] for h in heads], first=False)

    for h in heads:
        acc_ref[rows_of(h), :] = acc_ref[rows_of(h), :] / l_ref[h]
    o_ref[0] = acc_ref[...].T * _silu(gate_ref[0])


def _moba(p3, bias_tiles, rel_bias):
    b, t, _ = p3.shape
    BLK = MOBA_BLOCK
    nb = t // BLK
    assert 2 <= nb <= 16 and t % BLK == 0
    base = 2 * 4
    tile = lambda c: pl.BlockSpec((1, BLK, GROUP), lambda i, j, c=c: (i, j, c))
    seq = lambda c: pl.BlockSpec((1, t, GROUP), lambda i, j, c=c: (i, 0, c))
    return pl.pallas_call(
        functools.partial(_moba_kernel, nb=nb),
        out_shape=jax.ShapeDtypeStruct((b, t, GROUP), F32),
        grid=(b, nb),
        in_specs=[tile(base), seq(base + 1), seq(base + 2), tile(base + 3),
                  pl.BlockSpec(bias_tiles.shape, lambda i, j: (0, 0, 0, 0)),
                  pl.BlockSpec(memory_space=pltpu.SMEM)],
        out_specs=pl.BlockSpec((1, BLK, GROUP), lambda i, j: (i, j, 0)),
        scratch_shapes=[pltpu.VMEM((16, GROUP), F32),
                        pltpu.VMEM((nb, BLK, GROUP), BF16),
                        pltpu.VMEM((nb, GROUP + 16, BLK), BF16),
                        pltpu.VMEM((HEADS, 16, BLK), F32),
                        pltpu.VMEM((HEADS, 1, BLK), F32),
                        pltpu.VMEM((HEADS, 1, BLK), F32),
                        pltpu.VMEM((GROUP, BLK), F32)],
        compiler_params=pltpu.CompilerParams(
            dimension_semantics=("arbitrary", "arbitrary"), vmem_limit_bytes=VMEM_LIMIT_V7X),
        name="moba",
    )(p3, p3, p3, p3, bias_tiles, rel_bias)


def _ret_tables(t):
    half = HDIM // 2
    theta = 1.0 / (10000.0 ** jnp.linspace(0.0, 1.0, half))
    pos = jnp.arange(t, dtype=F32)
    ang = pos[:, None] * theta[None, :]
    sin, cos = jnp.sin(ang), jnp.cos(ang)
    cos2 = jnp.tile(jnp.repeat(cos, 2, axis=1), (1, HEADS))
    sin2 = jnp.tile(jnp.stack([-sin, sin], axis=-1).reshape(t, HDIM), (1, HEADS))
    log_gamma = jnp.log(1.0 - 2.0 ** (-5.0 - jnp.arange(HEADS, dtype=F32)))
    idx = jnp.arange(RET_CHUNK, dtype=F32)
    diff = idx[:, None] - idx[None, :]
    decay_intra = jnp.where(diff >= 0, jnp.exp(log_gamma[:, None, None] * jnp.maximum(diff, 0.0)), 0.0)
    q_decay = jnp.exp(log_gamma[:, None] * (idx + 1.0))
    k_decay = jnp.exp(log_gamma[:, None] * (RET_CHUNK - 1.0 - idx))
    chunk_decay = jnp.exp(log_gamma * RET_CHUNK)
    lanes = lambda a: jnp.repeat(a.T, HDIM, axis=1)
    cd = jnp.repeat(chunk_decay, HDIM).reshape(1, GROUP)
    return cos2, sin2, decay_intra, lanes(q_decay), lanes(k_decay), cd


def _ret_kernel(q_ref, k_ref, v_ref, gate_ref, cos_ref, sin_ref, di_ref, qd_ref, kd_ref, cd_ref,
                o_ref, state_ref, y_ref):
    @pl.when(pl.program_id(1) == 0)
    def _():
        state_ref[...] = jnp.zeros_like(state_ref)

    C = RET_CHUNK
    even = (_iota((C, GROUP), 1) % 2) == 0

    def rotate(x):
        swapped = jnp.where(even, pltpu.roll(x, GROUP - 1, axis=1), pltpu.roll(x, 1, axis=1))
        return x * cos_ref[...] + swapped * sin_ref[...]

    q = rotate(q_ref[0])
    k = rotate(k_ref[0]) * (HDIM ** -0.5)
    v = v_ref[0]
    qd = q * qd_ref[...]
    kd = k * kd_ref[...]
    cd = cd_ref[...]
    for h in range(HEADS):
        hs = slice(h * HDIM, (h + 1) * HDIM)
        state = state_ref[h]
        inner = _nt(q[:, hs], k[:, hs], HIGHEST) * di_ref[h]
        y = _nn(inner, v[:, hs], HIGHEST) + _nn(qd[:, hs], state, HIGHEST)
        state_ref[h] = state * cd[0:1, h * HDIM:h * HDIM + 1] + _tn(kd[:, hs], v[:, hs], HIGHEST)
        y_ref[:, hs] = y * lax.rsqrt(jnp.mean(y * y, axis=-1, keepdims=True) + NORM_EPS)
    o_ref[0] = y_ref[...] * _silu(gate_ref[0])


def _ret(p3, tables):
    b, t, _ = p3.shape
    C = RET_CHUNK
    base = 3 * 4
    cos2, sin2, di, qd, kd, cd = tables
    blk = lambda c: pl.BlockSpec((1, C, GROUP), lambda i, j, c=c: (i, j, c))
    full2 = lambda a: pl.BlockSpec(a.shape, lambda i, j: (0, 0))
    return pl.pallas_call(
        _ret_kernel,
        out_shape=jax.ShapeDtypeStruct((b, t, GROUP), F32),
        grid=(b, t // C),
        in_specs=[blk(base), blk(base + 1), blk(base + 2), blk(base + 3),
                  pl.BlockSpec((C, GROUP), lambda i, j: (j, 0)),
                  pl.BlockSpec((C, GROUP), lambda i, j: (j, 0)),
                  pl.BlockSpec(di.shape, lambda i, j: (0, 0, 0)),
                  full2(qd), full2(kd), full2(cd)],
        out_specs=pl.BlockSpec((1, C, GROUP), lambda i, j: (i, j, 0)),
        scratch_shapes=[pltpu.VMEM((HEADS, HDIM, HDIM), F32),
                        pltpu.VMEM((C, GROUP), F32)],
        compiler_params=pltpu.CompilerParams(
            dimension_semantics=("arbitrary", "arbitrary"), vmem_limit_bytes=VMEM_LIMIT_V7X),
        name="retnet",
    )(p3, p3, p3, p3, cos2, sin2, di, qd, kd, cd)


def _reorder_w_in(w):
    g3 = 3 * GROUP
    lora = w[:, g3:g3 + 2 * LORA]
    pad = jnp.zeros((w.shape[0], 128 - 2 * LORA), w.dtype)
    return jnp.concatenate([w[:, :g3], w[:, g3 + 2 * LORA:], lora, pad], axis=1)


def kernel(x, norm_w, w_in, w_out, rwkv_mu, rwkv_w0, rwkv_w2, rwkv_a0, rwkv_a2, rwkv_k_k, rwkv_k_a,
           rwkv_r_k, rwkv_lnx_w, rwkv_lnx_b, conv_w, rel_bias, final_norm_w):
    b, t, d = x.shape
    depth = w_in.shape[0]
    bias_tiles = _moba_bias_tiles(rel_bias)
    ret_tables = _ret_tables(t)
    x2 = x.reshape(b * t, d)
    for l in range(depth):
        w_l = _reorder_w_in(w_in[l]).astype(BF16)
        p3 = _inproj(x2, norm_w[l], w_l).reshape(b, t, P_COLS)
        y_rwkv = _rwkv(p3, rwkv_mu[l], rwkv_w0[l], rwkv_w2[l], rwkv_a0[l], rwkv_a2[l], rwkv_k_k[l],
                       rwkv_k_a[l], rwkv_r_k[l].reshape(GROUP), rwkv_lnx_w[l], rwkv_lnx_b[l])
        y_conv = _conv(p3, conv_w[l])
        y_moba = _moba(p3, bias_tiles, rel_bias)
        y_ret = _ret(p3, ret_tables)
        ys = [y.reshape(b * t, GROUP) for y in (y_rwkv, y_conv, y_moba, y_ret)]
        x2 = _outproj(x2, ys, w_out[l].astype(BF16), final_norm_w, final=(l == depth - 1))
    return x2.reshape(b, t, d)
```

```python
import functools
import math

import numpy as np
import jax
import jax.numpy as jnp
from jax import lax
from jax.experimental import pallas as pl
from jax.experimental.pallas import tpu as pltpu

F32 = jnp.float32
BF16 = jnp.bfloat16
HIGHEST = lax.Precision.HIGHEST

GROUP = 256
HEADS = 4
HDIM = 64
LORA = 32
DECAY_SCALE = math.exp(-0.5)
LNX_EPS = 64e-5
NORM_EPS = 1e-6
CONV_TAPS = 3
MOBA_BLOCK = 256
MOBA_TOPK = 3
MOBA_QBLOCK = 128
RET_CHUNK = 128
REL_BUCKETS = 32
REL_MAX_DIST = 128
RWKV_CHUNK = 64

P_COLS = 4 * 4 * GROUP + 128
LORA_COL = 4 * 4 * GROUP

VMEM_LIMIT_V7X = 48 * 1024 * 1024


def _nn(a, b, precision=None):
    return lax.dot_general(a, b, (((1,), (0,)), ((), ())), precision=precision,
                           preferred_element_type=F32)


def _nt(a, b, precision=None):
    return lax.dot_general(a, b, (((1,), (1,)), ((), ())), precision=precision,
                           preferred_element_type=F32)


def _tn(a, b, precision=None):
    return lax.dot_general(a, b, (((0,), (0,)), ((), ())), precision=precision,
                           preferred_element_type=F32)


def _split(x):
    hi = x.astype(BF16)
    lo = (x - hi.astype(F32)).astype(BF16)
    return hi, lo


def _dot3(dot, a, b):
    return dot(a[0], b[0]) + (dot(a[0], b[1]) + dot(a[1], b[0]))


def _psl(pair, rows, cols):
    return pair[0][rows, cols], pair[1][rows, cols]


def _pcat(pairs, axis):
    return (jnp.concatenate([p[0] for p in pairs], axis=axis),
            jnp.concatenate([p[1] for p in pairs], axis=axis))


def _head_sum(x, esum_bf16):
    hi = x.astype(BF16)
    r1 = x - hi.astype(F32)
    mid = r1.astype(BF16)
    lo = (r1 - mid.astype(F32)).astype(BF16)
    return _nn(hi, esum_bf16) + (_nn(mid, esum_bf16) + _nn(lo, esum_bf16))


def _iota(shape, dim):
    return lax.broadcasted_iota(jnp.int32, shape, dim)


def _head_sum_matrix():
    r = _iota((GROUP, GROUP), 0) // HDIM
    c = _iota((GROUP, GROUP), 1) // HDIM
    return jnp.where(r == c, 1.0, 0.0).astype(F32)


def _silu(x):
    return x * jax.nn.sigmoid(x)


def _inproj_kernel(x_ref, nw_ref, w_ref, o_ref):
    x = x_ref[...]
    h = x * lax.rsqrt(jnp.mean(x * x, axis=-1, keepdims=True) + NORM_EPS) * nw_ref[...]
    o_ref[...] = _nn(h.astype(BF16), w_ref[...])


def _inproj(x2d, norm_w, w_bf16, tm=256):
    m, d = x2d.shape
    n = w_bf16.shape[1]
    return pl.pallas_call(
        _inproj_kernel,
        out_shape=jax.ShapeDtypeStruct((m, n), F32),
        grid=(m // tm,),
        in_specs=[pl.BlockSpec((tm, d), lambda i: (i, 0)),
                  pl.BlockSpec((1, d), lambda i: (0, 0)),
                  pl.BlockSpec((d, n), lambda i: (0, 0))],
        out_specs=pl.BlockSpec((tm, n), lambda i: (i, 0)),
        compiler_params=pltpu.CompilerParams(
            dimension_semantics=("arbitrary",), vmem_limit_bytes=VMEM_LIMIT_V7X),
        name="inproj",
    )(x2d, norm_w.reshape(1, d), w_bf16)


def _outproj_kernel(x_ref, y0_ref, y1_ref, y2_ref, y3_ref, w_ref, fw_ref, o_ref, *, final):
    acc = x_ref[...]
    for g, y_ref in enumerate((y0_ref, y1_ref, y2_ref, y3_ref)):
        acc = acc + _nn(y_ref[...].astype(BF16), w_ref[g * GROUP:(g + 1) * GROUP, :])
    if final:
        acc = acc * lax.rsqrt(jnp.mean(acc * acc, axis=-1, keepdims=True) + NORM_EPS) * fw_ref[...]
    o_ref[...] = acc


def _outproj(x2d, ys, w_bf16, final_w, final, tm=512):
    m, d = x2d.shape
    yspec = pl.BlockSpec((tm, GROUP), lambda i: (i, 0))
    return pl.pallas_call(
        functools.partial(_outproj_kernel, final=final),
        out_shape=jax.ShapeDtypeStruct((m, d), F32),
        grid=(m // tm,),
        in_specs=[pl.BlockSpec((tm, d), lambda i: (i, 0)), yspec, yspec, yspec, yspec,
                  pl.BlockSpec((4 * GROUP, d), lambda i: (0, 0)),
                  pl.BlockSpec((1, d), lambda i: (0, 0))],
        out_specs=pl.BlockSpec((tm, d), lambda i: (i, 0)),
        compiler_params=pltpu.CompilerParams(
            dimension_semantics=("arbitrary",), vmem_limit_bytes=VMEM_LIMIT_V7X),
        name="outproj_final" if final else "outproj",
    )(x2d, *ys, w_bf16, final_w.reshape(1, d))


def _rwkv_kernel(rkv_ref, gate_ref, lora_ref, mu_ref, mul_ref, w0_ref, w2_ref, a0_ref, a2_ref,
                 kk_ref, ka_ref, rk_ref, lnw_ref, lnb_ref, o_ref,
                 state_ref, prev_ref, prevl_ref, y_ref, *, tt):
    C = RWKV_CHUNK
    t_idx = pl.program_id(1)

    @pl.when(t_idx == 0)
    def _():
        state_ref[...] = jnp.zeros_like(state_ref)
        prev_ref[...] = jnp.zeros_like(prev_ref)
        prevl_ref[...] = jnp.zeros_like(prevl_ref)

    p = rkv_ref[0]
    lo = lora_ref[0]
    row0 = _iota((tt, 1), 0) == 0
    p_sh = jnp.where(row0, prev_ref[...], pltpu.roll(p, 1, axis=0))
    lo_sh = jnp.where(row0, prevl_ref[...], pltpu.roll(lo, 1, axis=0))
    prev_ref[...] = p[tt - 1:tt, :]
    prevl_ref[...] = lo[tt - 1:tt, :]
    p = p + (p_sh - p) * mu_ref[...]
    lo = lo + (lo_sh - lo) * mul_ref[...]

    r = p[:, 0:GROUP]
    k = p[:, GROUP:2 * GROUP]
    v = p[:, 2 * GROUP:3 * GROUP]

    esum = _head_sum_matrix().astype(BF16)
    lw = -DECAY_SCALE * jax.nn.sigmoid(
        w0_ref[...] + _dot3(_nn, _split(jnp.tanh(lo)), _split(w2_ref[...])))
    rate = jax.nn.sigmoid(a0_ref[...] + _dot3(_nn, _split(lo), _split(a2_ref[...])))
    kk = k * kk_ref[...]
    kk = kk / jnp.maximum(jnp.sqrt(_head_sum(kk * kk, esum)), 1e-12)
    k2 = k * (1.0 + (rate - 1.0) * ka_ref[...])
    bonus = _head_sum(r * k2 * rk_ref[...], esum) * v
    bvec = kk * rate
    avec = -kk

    rowmod = _iota((tt, 1), 0) % C
    cum = lw
    for sh in (1, 2, 4, 8, 16, 32):
        cum = cum + jnp.where(rowmod >= sh, pltpu.roll(cum, sh, axis=0), 0.0)
    tot_rows = [cum[c * C + C - 1:c * C + C, :] for c in range(tt // C)]
    tot = jnp.concatenate([jnp.broadcast_to(tr, (C, GROUP)) for tr in tot_rows], axis=0)
    e_neg = jnp.exp(-cum)
    e_end = jnp.exp(tot - cum)
    r_t = _split(r * jnp.exp(cum))
    a_t = _split(avec * jnp.exp(cum - lw))
    k_t = _split(k2 * e_neg)
    b_t = _split(bvec * e_neg)
    k_h = _split(k2 * e_end)
    b_h = _split(bvec * e_end)
    v_s = _split(v)

    ri = _iota((2 * C, 2 * C), 0)
    cj = _iota((2 * C, 2 * C), 1) % C
    aa_mask = ((ri < C) & (cj < ri)) | ((ri >= C) & (cj <= ri - C))
    ei = _iota((C, C), 0)
    ej = _iota((C, C), 1)
    eye = jnp.where(ei == ej, 1.0, 0.0).astype(F32)

    n_chunks = tt // C
    blocks = [(c, h) for c in range(n_chunks) for h in range(HEADS)]
    cs_of = lambda c: slice(c * C, (c + 1) * C)
    hs_of = lambda h: slice(h * HDIM, (h + 1) * HDIM)
    blk = lambda pr, c, h: _psl(pr, cs_of(c), hs_of(h))

    aa_s, inv, pw = {}, {}, {}
    for c, h in blocks:
        aa = _dot3(_nt, _pcat([blk(a_t, c, h), blk(r_t, c, h)], 0),
                   _pcat([blk(b_t, c, h), blk(k_t, c, h)], 0))
        aa = jnp.where(aa_mask, aa, 0.0)
        inv[c, h] = eye + aa[0:C, 0:C]
        aa_s[c, h] = _split(aa)
    for c, h in blocks:
        ab_s = _psl(aa_s[c, h], slice(0, C), slice(0, C))
        pw[c, h] = _dot3(_nn, ab_s, ab_s)
    n_sq = int(math.log2(C)) - 1
    for it in range(n_sq):
        for c, h in blocks:
            pw_s = _split(pw[c, h])
            if it < n_sq - 1:
                tq = _dot3(_nn, _pcat([_split(inv[c, h]), pw_s], 0), pw_s)
                inv[c, h] = inv[c, h] + tq[0:C]
                pw[c, h] = tq[C:2 * C]
            else:
                inv[c, h] = inv[c, h] + _dot3(_nn, _split(inv[c, h]), pw_s)
    ta_s = {}
    for c, h in blocks:
        ta_s[c, h] = _split(_dot3(_nn, _split(inv[c, h]),
                                  _pcat([_psl(aa_s[c, h], slice(0, C), slice(C, 2 * C)), blk(a_t, c, h)], 1)))
    s_kv = [state_ref[h] for h in range(HEADS)]
    for c in range(n_chunks):
        s_s = [_split(s_kv[h]) for h in range(HEADS)]
        u_s = [_split(_dot3(_nn, ta_s[c, h], _pcat([blk(v_s, c, h), s_s[h]], 0))) for h in range(HEADS)]
        for h in range(HEADS):
            vv = blk(v_s, c, h)
            w_col = jnp.sum(eye * jnp.exp(tot_rows[c][:, hs_of(h)]), axis=1, keepdims=True)
            s_kv[h] = s_kv[h] * w_col + _dot3(_tn, _pcat([blk(k_h, c, h), blk(b_h, c, h)], 0),
                                              _pcat([vv, u_s[h]], 0))
        for h in range(HEADS):
            y_ref[cs_of(c), hs_of(h)] = _dot3(
                _nn, _pcat([_psl(aa_s[c, h], slice(C, 2 * C), slice(0, 2 * C)), blk(r_t, c, h)], 1),
                _pcat([u_s[h], blk(v_s, c, h), s_s[h]], 0))
    for h in range(HEADS):
        state_ref[h] = s_kv[h]

    y = y_ref[...]
    mean = _head_sum(y, esum) * (1.0 / HDIM)
    yc = y - mean
    var = _head_sum(yc * yc, esum) * (1.0 / HDIM)
    y = yc * lax.rsqrt(var + LNX_EPS) * lnw_ref[...] + lnb_ref[...] + bonus
    o_ref[0] = y * _silu(gate_ref[0])


def _rwkv(p3, mu, w0, w2, a0, a2, k_k, k_a, r_k, lnx_w, lnx_b, tt=256):
    b, t, _ = p3.shape
    mu_main = mu[:3 * GROUP].reshape(1, 3 * GROUP)
    mu_lora = jnp.concatenate([mu[3 * GROUP:], jnp.zeros((128 - 2 * LORA,), F32)]).reshape(1, 128)
    w2p = jnp.zeros((128, GROUP), F32).at[0:LORA].set(w2)
    a2p = jnp.zeros((128, GROUP), F32).at[LORA:2 * LORA].set(a2)
    row = lambda a: a.reshape(1, GROUP)
    vec = pl.BlockSpec((1, GROUP), lambda i, j: (0, 0))
    return pl.pallas_call(
        functools.partial(_rwkv_kernel, tt=tt),
        out_shape=jax.ShapeDtypeStruct((b, t, GROUP), F32),
        grid=(b, t // tt),
        in_specs=[pl.BlockSpec((1, tt, 3 * GROUP), lambda i, j: (i, j, 0)),
                  pl.BlockSpec((1, tt, GROUP), lambda i, j: (i, j, 3)),
                  pl.BlockSpec((1, tt, 128), lambda i, j: (i, j, LORA_COL // 128)),
                  pl.BlockSpec((1, 3 * GROUP), lambda i, j: (0, 0)),
                  pl.BlockSpec((1, 128), lambda i, j: (0, 0)),
                  vec,
                  pl.BlockSpec((128, GROUP), lambda i, j: (0, 0)),
                  vec,
                  pl.BlockSpec((128, GROUP), lambda i, j: (0, 0)),
                  vec, vec, vec, vec, vec],
        out_specs=pl.BlockSpec((1, tt, GROUP), lambda i, j: (i, j, 0)),
        scratch_shapes=[pltpu.VMEM((HEADS, HDIM, HDIM), F32),
                        pltpu.VMEM((1, 3 * GROUP), F32),
                        pltpu.VMEM((1, 128), F32),
                        pltpu.VMEM((tt, GROUP), F32)],
        compiler_params=pltpu.CompilerParams(
            dimension_semantics=("arbitrary", "arbitrary"), vmem_limit_bytes=VMEM_LIMIT_V7X),
        name="rwkv7",
    )(p3, p3, p3, mu_main, mu_lora, row(w0), w2p, row(a0), a2p,
      row(k_k), row(k_a), row(r_k), row(lnx_w), row(lnx_b))


def _conv_kernel(p_ref, cw_ref, o_ref, tail_ref, *, tt):
    @pl.when(pl.program_id(1) == 0)
    def _():
        tail_ref[...] = jnp.zeros_like(tail_ref)

    p = p_ref[0]
    u = p[:, GROUP:2 * GROUP] * p[:, 2 * GROUP:3 * GROUP]
    rows = _iota((tt, 1), 0)
    u1 = jnp.where(rows == 0, tail_ref[1:2, :], pltpu.roll(u, 1, axis=0))
    u2 = jnp.where(rows == 0, tail_ref[0:1, :],
                   jnp.where(rows == 1, tail_ref[1:2, :], pltpu.roll(u, 2, axis=0)))
    tail_ref[0:2, :] = u[tt - 2:tt, :]
    y = u2 * cw_ref[0:1, :] + u1 * cw_ref[1:2, :] + u * cw_ref[2:3, :]
    o_ref[0] = p[:, 0:GROUP] * y * _silu(p[:, 3 * GROUP:4 * GROUP])


def _conv(p3, conv_w, tt=512):
    b, t, _ = p3.shape
    return pl.pallas_call(
        functools.partial(_conv_kernel, tt=tt),
        out_shape=jax.ShapeDtypeStruct((b, t, GROUP), F32),
        grid=(b, t // tt),
        in_specs=[pl.BlockSpec((1, tt, 4 * GROUP), lambda i, j: (i, j, 1)),
                  pl.BlockSpec((CONV_TAPS, GROUP), lambda i, j: (0, 0))],
        out_specs=pl.BlockSpec((1, tt, GROUP), lambda i, j: (i, j, 0)),
        scratch_shapes=[pltpu.VMEM((8, GROUP), F32)],
        compiler_params=pltpu.CompilerParams(
            dimension_semantics=("arbitrary", "arbitrary"), vmem_limit_bytes=VMEM_LIMIT_V7X),
        name="sconv",
    )(p3, conv_w)


def _t5_bucket_np(dist):
    max_exact = REL_BUCKETS // 2
    d_f = np.maximum(dist, 1).astype(np.float32)
    large = max_exact + (np.log(d_f / np.float32(max_exact)) / np.float32(math.log(REL_MAX_DIST / max_exact))
                         * np.float32(REL_BUCKETS - max_exact)).astype(np.int32)
    large = np.minimum(large, REL_BUCKETS - 1)
    return np.where(dist < max_exact, dist, large).astype(np.int32)


MASKED_BUCKET = REL_BUCKETS


def _moba_bucket_table():
    keys = np.arange(MOBA_BLOCK)[:, None]
    queries = np.arange(MOBA_BLOCK)[None, :]
    prev = _t5_bucket_np(queries + MOBA_BLOCK - keys)
    own = np.where(keys <= queries, _t5_bucket_np(np.maximum(queries - keys, 0)), MASKED_BUCKET)
    return np.stack([prev, own]).astype(np.int32)


def _bias_kernel(idx_ref, rb_ref, o_ref):
    h = pl.program_id(0)
    idx = idx_ref[...]
    acc = jnp.full(idx.shape, -jnp.inf, F32)
    for bkt in range(REL_BUCKETS):
        acc = jnp.where(idx == bkt, rb_ref[bkt, h], acc)
    o_ref[0] = acc


def _moba_bias_tiles(rel_bias):
    idx = jnp.asarray(_moba_bucket_table())
    shp = (2, MOBA_BLOCK, MOBA_BLOCK)
    return pl.pallas_call(
        _bias_kernel,
        out_shape=jax.ShapeDtypeStruct((HEADS,) + shp, F32),
        grid=(HEADS,),
        in_specs=[pl.BlockSpec(shp, lambda h: (0, 0, 0)),
                  pl.BlockSpec(memory_space=pltpu.SMEM)],
        out_specs=pl.BlockSpec((1,) + shp, lambda h: (h, 0, 0, 0)),
        name="moba_bias",
    )(idx, rel_bias)


def _moba_kernel(q_ref, k_ref, v_ref, gate_ref, bias_ref, rb_ref, o_ref,
                 kmean_ref, kbf_ref, vt_ref, sel_ref, m_ref, l_ref, acc_ref, *, nb):
    BLK = MOBA_BLOCK
    ib = pl.program_id(1)
    heads = range(HEADS)
    rows_of = lambda h: slice(h * HDIM, (h + 1) * HDIM)
    neg_inf = -jnp.inf
    not_selected = -1e30

    @pl.when(ib == 0)
    def _():
        kmean_ref[...] = jnp.zeros_like(kmean_ref)
        for n in range(nb):
            blk = slice(n * BLK, (n + 1) * BLK)
            kblk = k_ref[0, blk, :]
            kmean_ref[n:n + 1, :] = jnp.mean(kblk, axis=0, keepdims=True)
            kbf_ref[n] = kblk.astype(BF16)
            vt_ref[n, 0:GROUP, :] = v_ref[0, blk, :].T.astype(BF16)
            vt_ref[n, GROUP:GROUP + 16, :] = jnp.ones((16, BLK), BF16)

    q = q_ref[0] * (HDIM ** -0.5)
    q_bf = q.astype(BF16)
    q_s = _split(q)
    lane_head = _iota((1, GROUP), 1) // HDIM
    q_heads = [jnp.where(lane_head == h, q_bf, jnp.zeros_like(q_bf)) for h in heads]

    km = kmean_ref[...]
    gates = [_dot3(_nt, _split(jnp.where(lane_head == h, km, 0.0)), q_s) for h in heads]
    blk_id = _iota((16, BLK), 0)
    for h in heads:
        g = jnp.where(blk_id < ib, gates[h], neg_inf)
        sel = jnp.zeros((16, BLK), F32)
        for _ in range(min(MOBA_TOPK, nb)):
            m = jnp.max(g, axis=0, keepdims=True)
            hit = (g == m) & (m > neg_inf)
            first = jnp.min(jnp.where(hit, blk_id, 16), axis=0, keepdims=True)
            pick = blk_id == first
            sel = jnp.where(pick, 1.0, sel)
            g = jnp.where(pick, neg_inf, g)
        sel_ref[h] = sel

    def attend(k_blk, vt_blk, extra, first):
        scores = [_nt(k_blk, q_heads[h]) for h in heads]
        probs, alphas = [], []
        for h in heads:
            s = scores[h] + extra[h]
            m_blk = jnp.max(s, axis=0, keepdims=True)
            if first:
                m_new = m_blk
            else:
                m_old = m_ref[h]
                m_new = jnp.maximum(m_old, m_blk)
                alphas.append(jnp.exp(m_old - m_new))
            m_ref[h] = m_new
            probs.append(jnp.exp(s - m_new).astype(BF16))
        for h in heads:
            lhs = jnp.concatenate([vt_blk[rows_of(h), :], vt_blk[GROUP:GROUP + 16, :]], axis=0)
            pv = _nn(lhs, probs[h])
            if first:
                acc_ref[rows_of(h), :] = pv[0:HDIM]
                l_ref[h] = pv[HDIM:HDIM + 1]
            else:
                acc_ref[rows_of(h), :] = alphas[h] * acc_ref[rows_of(h), :] + pv[0:HDIM]
                l_ref[h] = alphas[h] * l_ref[h] + pv[HDIM:HDIM + 1]

    attend(kbf_ref[ib], vt_ref[ib], [bias_ref[h, 1] for h in heads], first=True)

    def selected_row(h, n):
        return jnp.where(sel_ref[h, pl.ds(n, 1), :] > 0.5, 0.0, not_selected)

    @pl.when(ib >= 1)
    def _():
        attend(kbf_ref[ib - 1], vt_ref[ib - 1],
               [bias_ref[h, 0] + selected_row(h, ib - 1) for h in heads], first=False)

    for n in range(nb - 2):
        @pl.when(n < ib - 1)
        def _(n=n):
            attend(kbf_ref[n], vt_ref[n],
                   [selected_row(h, n) + rb_ref[REL_BUCKETS - 1, h] for h in heads], first=False)

    for h in heads:
        acc_ref[rows_of(h), :] = acc_ref[rows_of(h), :] / l_ref[h]
    o_ref[0] = acc_ref[...].T * _silu(gate_ref[0])


def _moba(p3, bias_tiles, rel_bias):
    b, t, _ = p3.shape
    BLK = MOBA_BLOCK
    nb = t // BLK
    assert 2 <= nb <= 16 and t % BLK == 0
    base = 2 * 4
    tile = lambda c: pl.BlockSpec((1, BLK, GROUP), lambda i, j, c=c: (i, j, c))
    seq = lambda c: pl.BlockSpec((1, t, GROUP), lambda i, j, c=c: (i, 0, c))
    return pl.pallas_call(
        functools.partial(_moba_kernel, nb=nb),
        out_shape=jax.ShapeDtypeStruct((b, t, GROUP), F32),
        grid=(b, nb),
        in_specs=[tile(base), seq(base + 1), seq(base + 2), tile(base + 3),
                  pl.BlockSpec(bias_tiles.shape, lambda i, j: (0, 0, 0, 0)),
                  pl.BlockSpec(memory_space=pltpu.SMEM)],
        out_specs=pl.BlockSpec((1, BLK, GROUP), lambda i, j: (i, j, 0)),
        scratch_shapes=[pltpu.VMEM((16, GROUP), F32),
                        pltpu.VMEM((nb, BLK, GROUP), BF16),
                        pltpu.VMEM((nb, GROUP + 16, BLK), BF16),
                        pltpu.VMEM((HEADS, 16, BLK), F32),
                        pltpu.VMEM((HEADS, 1, BLK), F32),
                        pltpu.VMEM((HEADS, 1, BLK), F32),
                        pltpu.VMEM((GROUP, BLK), F32)],
        compiler_params=pltpu.CompilerParams(
            dimension_semantics=("arbitrary", "arbitrary"), vmem_limit_bytes=VMEM_LIMIT_V7X),
        name="moba",
    )(p3, p3, p3, p3, bias_tiles, rel_bias)


def _ret_tables(t):
    half = HDIM // 2
    theta = 1.0 / (10000.0 ** jnp.linspace(0.0, 1.0, half))
    pos = jnp.arange(t, dtype=F32)
    ang = pos[:, None] * theta[None, :]
    sin, cos = jnp.sin(ang), jnp.cos(ang)
    cos2 = jnp.tile(jnp.repeat(cos, 2, axis=1), (1, HEADS))
    sin2 = jnp.tile(jnp.stack([-sin, sin], axis=-1).reshape(t, HDIM), (1, HEADS))
    log_gamma = jnp.log(1.0 - 2.0 ** (-5.0 - jnp.arange(HEADS, dtype=F32)))
    idx = jnp.arange(RET_CHUNK, dtype=F32)
    diff = idx[:, None] - idx[None, :]
    decay_intra = jnp.where(diff >= 0, jnp.exp(log_gamma[:, None, None] * jnp.maximum(diff, 0.0)), 0.0)
    q_decay = jnp.exp(log_gamma[:, None] * (idx + 1.0))
    k_decay = jnp.exp(log_gamma[:, None] * (RET_CHUNK - 1.0 - idx))
    chunk_decay = jnp.exp(log_gamma * RET_CHUNK)
    lanes = lambda a: jnp.repeat(a.T, HDIM, axis=1)
    cd = jnp.repeat(chunk_decay, HDIM).reshape(1, GROUP)
    return cos2, sin2, decay_intra, lanes(q_decay), lanes(k_decay), cd


def _ret_kernel(q_ref, k_ref, v_ref, gate_ref, cos_ref, sin_ref, di_ref, qd_ref, kd_ref, cd_ref,
                o_ref, state_ref, *, tt):
    C = RET_CHUNK
    n_chunks = tt // C
    chunks = range(n_chunks)
    heads = range(HEADS)
    b = pl.program_id(1)

    @pl.when(pl.program_id(0) == 0)
    def _():
        state_ref[b] = jnp.zeros((GROUP, GROUP), F32)

    even = (_iota((tt, GROUP), 1) % 2) == 0

    def rotate(x):
        swapped = jnp.where(even, pltpu.roll(x, GROUP - 1, axis=1), pltpu.roll(x, 1, axis=1))
        return x * cos_ref[...] + swapped * sin_ref[...]

    q = rotate(q_ref[0])
    k = rotate(k_ref[0]) * (HDIM ** -0.5)
    lane_head = _iota((1, GROUP), 1) // HDIM
    same_head = (_iota((GROUP, GROUP), 0) // HDIM) == (_iota((GROUP, GROUP), 1) // HDIM)
    rows = lambda c: slice(c * C, (c + 1) * C)
    mask_pair = lambda pr, h: (jnp.where(lane_head == h, pr[0], jnp.zeros_like(pr[0])),
                               jnp.where(lane_head == h, pr[1], jnp.zeros_like(pr[1])))

    q_s = [_split(q[rows(c)]) for c in chunks]
    k_s = [_split(k[rows(c)]) for c in chunks]
    v_s = [_split(v_ref[0, rows(c), :]) for c in chunks]
    kd_s = [_split(k[rows(c)] * kd_ref[...]) for c in chunks]
    qd_s = [_split(q[rows(c)] * qd_ref[...]) for c in chunks]

    inner_s = [_split(_dot3(_nt, _pcat([mask_pair(q_s[c], h) for h in heads], 0), k_s[c]) * di_ref[...])
               for c in chunks]
    kv = [jnp.where(same_head, _dot3(_tn, kd_s[c], v_s[c]), 0.0) for c in chunks]
    intra = []
    for c in chunks:
        stacked = _dot3(_nn, inner_s[c], v_s[c])
        acc = jnp.where(lane_head == 0, stacked[0:C], 0.0)
        for h in range(1, HEADS):
            acc = jnp.where(lane_head == h, stacked[h * C:(h + 1) * C], acc)
        intra.append(acc)
    state = state_ref[b]
    states = []
    for c in chunks:
        states.append(_split(state))
        state = state * cd_ref[...] + kv[c]
    state_ref[b] = state
    y = jnp.concatenate([intra[c] + _dot3(_nn, qd_s[c], states[c]) for c in chunks], axis=0)
    mean_sq = _head_sum(y * y, _head_sum_matrix().astype(BF16)) * (1.0 / HDIM)
    o_ref[0] = y * lax.rsqrt(mean_sq + NORM_EPS) * _silu(gate_ref[0])


def _ret(p3, tables, tt=512):
    b, t, _ = p3.shape
    base = 3 * 4
    cos2, sin2, di, qd, kd, cd = tables
    di = di.reshape(HEADS * RET_CHUNK, RET_CHUNK)
    blk = lambda c: pl.BlockSpec((1, tt, GROUP), lambda j, i, c=c: (i, j, c))
    full2 = lambda a: pl.BlockSpec(a.shape, lambda j, i: (0, 0))
    return pl.pallas_call(
        functools.partial(_ret_kernel, tt=tt),
        out_shape=jax.ShapeDtypeStruct((b, t, GROUP), F32),
        grid=(t // tt, b),
        in_specs=[blk(base), blk(base + 1), blk(base + 2), blk(base + 3),
                  pl.BlockSpec((tt, GROUP), lambda j, i: (j, 0)),
                  pl.BlockSpec((tt, GROUP), lambda j, i: (j, 0)),
                  full2(di), full2(qd), full2(kd), full2(cd)],
        out_specs=pl.BlockSpec((1, tt, GROUP), lambda j, i: (i, j, 0)),
        scratch_shapes=[pltpu.VMEM((b, GROUP, GROUP), F32)],
        compiler_params=pltpu.CompilerParams(
            dimension_semantics=("arbitrary", "arbitrary"), vmem_limit_bytes=VMEM_LIMIT_V7X),
        name="retnet",
    )(p3, p3, p3, p3, cos2, sin2, di, qd, kd, cd)


def _reorder_w_in(w):
    g3 = 3 * GROUP
    lora = w[:, g3:g3 + 2 * LORA]
    pad = jnp.zeros((w.shape[0], 128 - 2 * LORA), w.dtype)
    return jnp.concatenate([w[:, :g3], w[:, g3 + 2 * LORA:], lora, pad], axis=1)


def kernel(x, norm_w, w_in, w_out, rwkv_mu, rwkv_w0, rwkv_w2, rwkv_a0, rwkv_a2, rwkv_k_k, rwkv_k_a,
           rwkv_r_k, rwkv_lnx_w, rwkv_lnx_b, conv_w, rel_bias, final_norm_w):
    b, t, d = x.shape
    depth = w_in.shape[0]
    bias_tiles = _moba_bias_tiles(rel_bias)
    ret_tables = _ret_tables(t)
    x2 = x.reshape(b * t, d)
    for l in range(depth):
        w_l = _reorder_w_in(w_in[l]).astype(BF16)
        p3 = _inproj(x2, norm_w[l], w_l).reshape(b, t, P_COLS)
        y_rwkv = _rwkv(p3, rwkv_mu[l], rwkv_w0[l], rwkv_w2[l], rwkv_a0[l], rwkv_a2[l], rwkv_k_k[l],
                       rwkv_k_a[l], rwkv_r_k[l].reshape(GROUP), rwkv_lnx_w[l], rwkv_lnx_b[l])
        y_conv = _conv(p3, conv_w[l])
        y_moba = _moba(p3, bias_tiles, rel_bias)
        y_ret = _ret(p3, ret_tables)
        ys = [y.reshape(b * t, GROUP) for y in (y_rwkv, y_conv, y_moba, y_ret)]
        x2 = _outproj(x2, ys, w_out[l].astype(BF16), final_norm_w, final=(l == depth - 1))
    return x2.reshape(b, t, d)
```

```python
import functools
import math

import numpy as np
import jax
import jax.numpy as jnp
from jax import lax
from jax.experimental import pallas as pl
from jax.experimental.pallas import tpu as pltpu

F32 = jnp.float32
BF16 = jnp.bfloat16
HIGHEST = lax.Precision.HIGHEST

GROUP = 256
HEADS = 4
HDIM = 64
LORA = 32
DECAY_SCALE = math.exp(-0.5)
LNX_EPS = 64e-5
NORM_EPS = 1e-6
CONV_TAPS = 3
MOBA_BLOCK = 256
MOBA_TOPK = 3
MOBA_QBLOCK = 128
RET_CHUNK = 128
REL_BUCKETS = 32
REL_MAX_DIST = 128
RWKV_CHUNK = 64

P_COLS = 4 * 4 * GROUP + 128
LORA_COL = 4 * 4 * GROUP

VMEM_LIMIT_V7X = 48 * 1024 * 1024


def _nn(a, b, precision=None):
    return lax.dot_general(a, b, (((1,), (0,)), ((), ())), precision=precision,
                           preferred_element_type=F32)


def _nt(a, b, precision=None):
    return lax.dot_general(a, b, (((1,), (1,)), ((), ())), precision=precision,
                           preferred_element_type=F32)


def _tn(a, b, precision=None):
    return lax.dot_general(a, b, (((0,), (0,)), ((), ())), precision=precision,
                           preferred_element_type=F32)


def _split(x):
    hi = x.astype(BF16)
    lo = (x - hi.astype(F32)).astype(BF16)
    return hi, lo


def _dot3(dot, a, b):
    out_axis = 1 if dot is _tn else 0
    m = a[0].shape[out_axis]
    both = dot(jnp.concatenate([a[0], a[1]], axis=out_axis), b[0])
    return both[:m] + both[m:] + dot(a[0], b[1])


def _psl(pair, rows, cols):
    return pair[0][rows, cols], pair[1][rows, cols]


def _pcat(pairs, axis):
    return (jnp.concatenate([p[0] for p in pairs], axis=axis),
            jnp.concatenate([p[1] for p in pairs], axis=axis))


def _head_sum(x, esum_bf16):
    hi = x.astype(BF16)
    r1 = x - hi.astype(F32)
    mid = r1.astype(BF16)
    lo = (r1 - mid.astype(F32)).astype(BF16)
    return _nn(hi, esum_bf16) + (_nn(mid, esum_bf16) + _nn(lo, esum_bf16))


def _iota(shape, dim):
    return lax.broadcasted_iota(jnp.int32, shape, dim)


def _head_sum_matrix():
    r = _iota((GROUP, GROUP), 0) // HDIM
    c = _iota((GROUP, GROUP), 1) // HDIM
    return jnp.where(r == c, 1.0, 0.0).astype(F32)


def _silu(x):
    return x * jax.nn.sigmoid(x)


def _inproj_kernel(x_ref, nw_ref, w_ref, o_ref):
    x = x_ref[...]
    h = x * lax.rsqrt(jnp.mean(x * x, axis=-1, keepdims=True) + NORM_EPS) * nw_ref[...]
    o_ref[...] = _nn(h.astype(BF16), w_ref[...])


def _inproj(x2d, norm_w, w_bf16, tm=256):
    m, d = x2d.shape
    n = w_bf16.shape[1]
    return pl.pallas_call(
        _inproj_kernel,
        out_shape=jax.ShapeDtypeStruct((m, n), F32),
        grid=(m // tm,),
        in_specs=[pl.BlockSpec((tm, d), lambda i: (i, 0)),
                  pl.BlockSpec((1, d), lambda i: (0, 0)),
                  pl.BlockSpec((d, n), lambda i: (0, 0))],
        out_specs=pl.BlockSpec((tm, n), lambda i: (i, 0)),
        compiler_params=pltpu.CompilerParams(
            dimension_semantics=("arbitrary",), vmem_limit_bytes=VMEM_LIMIT_V7X),
        name="inproj",
    )(x2d, norm_w.reshape(1, d), w_bf16)


def _outproj_kernel(x_ref, y0_ref, y1_ref, y2_ref, y3_ref, w_ref, fw_ref, o_ref, *, final):
    acc = x_ref[...]
    for g, y_ref in enumerate((y0_ref, y1_ref, y2_ref, y3_ref)):
        acc = acc + _nn(y_ref[...].astype(BF16), w_ref[g * GROUP:(g + 1) * GROUP, :])
    if final:
        acc = acc * lax.rsqrt(jnp.mean(acc * acc, axis=-1, keepdims=True) + NORM_EPS) * fw_ref[...]
    o_ref[...] = acc


def _outproj(x2d, ys, w_bf16, final_w, final, tm=512):
    m, d = x2d.shape
    yspec = pl.BlockSpec((tm, GROUP), lambda i: (i, 0))
    return pl.pallas_call(
        functools.partial(_outproj_kernel, final=final),
        out_shape=jax.ShapeDtypeStruct((m, d), F32),
        grid=(m // tm,),
        in_specs=[pl.BlockSpec((tm, d), lambda i: (i, 0)), yspec, yspec, yspec, yspec,
                  pl.BlockSpec((4 * GROUP, d), lambda i: (0, 0)),
                  pl.BlockSpec((1, d), lambda i: (0, 0))],
        out_specs=pl.BlockSpec((tm, d), lambda i: (i, 0)),
        compiler_params=pltpu.CompilerParams(
            dimension_semantics=("arbitrary",), vmem_limit_bytes=VMEM_LIMIT_V7X),
        name="outproj_final" if final else "outproj",
    )(x2d, *ys, w_bf16, final_w.reshape(1, d))


def _rwkv_kernel(rkv_ref, gate_ref, lora_ref, mu_ref, mul_ref, w0_ref, w2_ref, a0_ref, a2_ref,
                 kk_ref, ka_ref, rk_ref, lnw_ref, lnb_ref, o_ref,
                 state_ref, prev_ref, prevl_ref, *, tt):
    C = RWKV_CHUNK
    t_idx = pl.program_id(1)

    @pl.when(t_idx == 0)
    def _():
        state_ref[...] = jnp.zeros_like(state_ref)
        prev_ref[...] = jnp.zeros_like(prev_ref)
        prevl_ref[...] = jnp.zeros_like(prevl_ref)

    p = rkv_ref[0]
    lo = lora_ref[0]
    row0 = _iota((tt, 1), 0) == 0
    p_sh = jnp.where(row0, prev_ref[...], pltpu.roll(p, 1, axis=0))
    lo_sh = jnp.where(row0, prevl_ref[...], pltpu.roll(lo, 1, axis=0))
    prev_ref[...] = p[tt - 1:tt, :]
    prevl_ref[...] = lo[tt - 1:tt, :]
    p = p + (p_sh - p) * mu_ref[...]
    lo = lo + (lo_sh - lo) * mul_ref[...]

    r = p[:, 0:GROUP]
    k = p[:, GROUP:2 * GROUP]
    v = p[:, 2 * GROUP:3 * GROUP]

    esum = _head_sum_matrix().astype(BF16)
    lw = -DECAY_SCALE * jax.nn.sigmoid(
        w0_ref[...] + _dot3(_nn, _split(jnp.tanh(lo)), _split(w2_ref[...])))
    rate = jax.nn.sigmoid(a0_ref[...] + _dot3(_nn, _split(lo), _split(a2_ref[...])))
    kk = k * kk_ref[...]
    kk = kk / jnp.maximum(jnp.sqrt(_head_sum(kk * kk, esum)), 1e-12)
    k2 = k * (1.0 + (rate - 1.0) * ka_ref[...])
    bonus = _head_sum(r * k2 * rk_ref[...], esum) * v
    bvec = kk * rate
    avec = -kk

    rowmod = _iota((tt, 1), 0) % C
    cum = lw
    for sh in (1, 2, 4, 8, 16, 32):
        cum = cum + jnp.where(rowmod >= sh, pltpu.roll(cum, sh, axis=0), 0.0)
    tot_rows = [cum[c * C + C - 1:c * C + C, :] for c in range(tt // C)]
    tot = jnp.concatenate([jnp.broadcast_to(tr, (C, GROUP)) for tr in tot_rows], axis=0)
    e_neg = jnp.exp(-cum)
    e_end = jnp.exp(tot - cum)
    r_t = _split(r * jnp.exp(cum))
    a_t = _split(avec * jnp.exp(cum - lw))
    k_t = _split(k2 * e_neg)
    b_t = _split(bvec * e_neg)
    k_h = _split(k2 * e_end)
    b_h = _split(bvec * e_end)
    v_s = _split(v)

    same_head = (_iota((GROUP, GROUP), 0) // HDIM) == (_iota((GROUP, GROUP), 1) // HDIM)

    def bd(pr):
        return tuple(jnp.where(same_head, jnp.concatenate([part] * HEADS, axis=0), jnp.zeros((), part.dtype))
                     for part in pr)

    t_i = _iota((2 * C, GROUP), 0)
    s_i = _iota((2 * C, GROUP), 1) % C
    tri = ((t_i < C) & (s_i < t_i)) | ((t_i >= C) & (s_i <= t_i - C))
    eye = jnp.where(_iota((C, GROUP), 0) == _iota((C, GROUP), 1) % C, 1.0, 0.0).astype(F32)

    chunks = range(tt // C)
    every = slice(None)
    rows = lambda pr, c: _psl(pr, slice(c * C, (c + 1) * C), every)
    top = lambda pr: _psl(pr, slice(0, C), every)
    bottom = lambda pr: _psl(pr, slice(C, 2 * C), every)

    ar = [_pcat([rows(a_t, c), rows(r_t, c)], 0) for c in chunks]
    bd_v = [bd(rows(v_s, c)) for c in chunks]
    ab_rb = [jnp.where(tri, _dot3(_nt, ar[c], bd(rows(b_t, c))), 0.0) for c in chunks]
    ak_rk = [_split(jnp.where(tri, _dot3(_nt, ar[c], bd(rows(k_t, c))), 0.0)) for c in chunks]
    inv = [eye + ab_rb[c][0:C] for c in chunks]
    ab_rb = [_split(x) for x in ab_rb]
    pw = [_dot3(_nn, top(ab_rb[c]), bd(top(ab_rb[c]))) for c in chunks]
    n_sq = int(math.log2(C)) - 1
    for it in range(n_sq):
        for c in chunks:
            pw_s = _split(pw[c])
            if it < n_sq - 1:
                tq = _dot3(_nn, _pcat([_split(inv[c]), pw_s], 0), bd(pw_s))
                inv[c] = inv[c] + tq[0:C]
                pw[c] = tq[C:2 * C]
            else:
                inv[c] = inv[c] + _dot3(_nn, _split(inv[c]), bd(pw_s))
    inv = [_split(x) for x in inv]
    t_ak = [_split(_dot3(_nn, inv[c], bd(top(ak_rk[c])))) for c in chunks]
    t_a = [_split(_dot3(_nn, inv[c], bd(rows(a_t, c)))) for c in chunks]
    free = [_dot3(_nn, _pcat([t_ak[c], bottom(ak_rk[c])], 0), bd_v[c]) for c in chunks]
    kb_hat = [_pcat([rows(k_h, c), rows(b_h, c)], 0) for c in chunks]
    m_s = [_split(jnp.where(same_head, _dot3(_tn, t_a[c], rows(b_h, c)), 0.0)) for c in chunks]
    g = []
    for c in chunks:
        full = jnp.where(same_head, _dot3(_tn, _pcat([rows(v_s, c), _split(free[c][0:C])], 0), kb_hat[c]), 0.0)
        g.append((full[0:HDIM] + full[HDIM:2 * HDIM]) + (full[2 * HDIM:3 * HDIM] + full[3 * HDIM:4 * HDIM]))
    state = state_ref[...]
    states = []
    for c in chunks:
        st_s = _split(state)
        states.append(bd(st_s))
        state = state * jnp.exp(tot_rows[c]) + _dot3(_nn, st_s, m_s[c]) + g[c]
    state_ref[...] = state
    from_state = [_dot3(_nt, _pcat([t_a[c], rows(r_t, c)], 0), states[c]) for c in chunks]
    y = jnp.concatenate(
        [free[c][C:2 * C] + from_state[c][C:2 * C]
         + _dot3(_nn, bottom(ab_rb[c]), bd(_split(free[c][0:C] + from_state[c][0:C]))) for c in chunks], axis=0)
    mean = _head_sum(y, esum) * (1.0 / HDIM)
    yc = y - mean
    var = _head_sum(yc * yc, esum) * (1.0 / HDIM)
    y = yc * lax.rsqrt(var + LNX_EPS) * lnw_ref[...] + lnb_ref[...] + bonus
    o_ref[0] = y * _silu(gate_ref[0])


def _rwkv(p3, mu, w0, w2, a0, a2, k_k, k_a, r_k, lnx_w, lnx_b, tt=256):
    b, t, _ = p3.shape
    mu_main = mu[:3 * GROUP].reshape(1, 3 * GROUP)
    mu_lora = jnp.concatenate([mu[3 * GROUP:], jnp.zeros((128 - 2 * LORA,), F32)]).reshape(1, 128)
    w2p = jnp.zeros((128, GROUP), F32).at[0:LORA].set(w2)
    a2p = jnp.zeros((128, GROUP), F32).at[LORA:2 * LORA].set(a2)
    row = lambda a: a.reshape(1, GROUP)
    vec = pl.BlockSpec((1, GROUP), lambda i, j: (0, 0))
    return pl.pallas_call(
        functools.partial(_rwkv_kernel, tt=tt),
        out_shape=jax.ShapeDtypeStruct((b, t, GROUP), F32),
        grid=(b, t // tt),
        in_specs=[pl.BlockSpec((1, tt, 3 * GROUP), lambda i, j: (i, j, 0)),
                  pl.BlockSpec((1, tt, GROUP), lambda i, j: (i, j, 3)),
                  pl.BlockSpec((1, tt, 128), lambda i, j: (i, j, LORA_COL // 128)),
                  pl.BlockSpec((1, 3 * GROUP), lambda i, j: (0, 0)),
                  pl.BlockSpec((1, 128), lambda i, j: (0, 0)),
                  vec,
                  pl.BlockSpec((128, GROUP), lambda i, j: (0, 0)),
                  vec,
                  pl.BlockSpec((128, GROUP), lambda i, j: (0, 0)),
                  vec, vec, vec, vec, vec],
        out_specs=pl.BlockSpec((1, tt, GROUP), lambda i, j: (i, j, 0)),
        scratch_shapes=[pltpu.VMEM((HDIM, GROUP), F32),
                        pltpu.VMEM((1, 3 * GROUP), F32),
                        pltpu.VMEM((1, 128), F32)],
        compiler_params=pltpu.CompilerParams(
            dimension_semantics=("arbitrary", "arbitrary"), vmem_limit_bytes=VMEM_LIMIT_V7X),
        name="rwkv7",
    )(p3, p3, p3, mu_main, mu_lora, row(w0), w2p, row(a0), a2p,
      row(k_k), row(k_a), row(r_k), row(lnx_w), row(lnx_b))


def _conv_kernel(p_ref, cw_ref, o_ref, tail_ref, *, tt):
    @pl.when(pl.program_id(1) == 0)
    def _():
        tail_ref[...] = jnp.zeros_like(tail_ref)

    p = p_ref[0]
    u = p[:, GROUP:2 * GROUP] * p[:, 2 * GROUP:3 * GROUP]
    rows = _iota((tt, 1), 0)
    u1 = jnp.where(rows == 0, tail_ref[1:2, :], pltpu.roll(u, 1, axis=0))
    u2 = jnp.where(rows == 0, tail_ref[0:1, :],
                   jnp.where(rows == 1, tail_ref[1:2, :], pltpu.roll(u, 2, axis=0)))
    tail_ref[0:2, :] = u[tt - 2:tt, :]
    y = u2 * cw_ref[0:1, :] + u1 * cw_ref[1:2, :] + u * cw_ref[2:3, :]
    o_ref[0] = p[:, 0:GROUP] * y * _silu(p[:, 3 * GROUP:4 * GROUP])


def _conv(p3, conv_w, tt=512):
    b, t, _ = p3.shape
    return pl.pallas_call(
        functools.partial(_conv_kernel, tt=tt),
        out_shape=jax.ShapeDtypeStruct((b, t, GROUP), F32),
        grid=(b, t // tt),
        in_specs=[pl.BlockSpec((1, tt, 4 * GROUP), lambda i, j: (i, j, 1)),
                  pl.BlockSpec((CONV_TAPS, GROUP), lambda i, j: (0, 0))],
        out_specs=pl.BlockSpec((1, tt, GROUP), lambda i, j: (i, j, 0)),
        scratch_shapes=[pltpu.VMEM((8, GROUP), F32)],
        compiler_params=pltpu.CompilerParams(
            dimension_semantics=("arbitrary", "arbitrary"), vmem_limit_bytes=VMEM_LIMIT_V7X),
        name="sconv",
    )(p3, conv_w)


def _t5_bucket_np(dist):
    max_exact = REL_BUCKETS // 2
    d_f = np.maximum(dist, 1).astype(np.float32)
    large = max_exact + (np.log(d_f / np.float32(max_exact)) / np.float32(math.log(REL_MAX_DIST / max_exact))
                         * np.float32(REL_BUCKETS - max_exact)).astype(np.int32)
    large = np.minimum(large, REL_BUCKETS - 1)
    return np.where(dist < max_exact, dist, large).astype(np.int32)


MASKED_BUCKET = REL_BUCKETS


def _moba_bucket_table():
    keys = np.arange(MOBA_BLOCK)[:, None]
    queries = np.arange(MOBA_BLOCK)[None, :]
    prev = _t5_bucket_np(queries + MOBA_BLOCK - keys)
    own = np.where(keys <= queries, _t5_bucket_np(np.maximum(queries - keys, 0)), MASKED_BUCKET)
    return np.stack([prev, own]).astype(np.int32)


def _bias_kernel(idx_ref, rb_ref, o_ref):
    h = pl.program_id(0)
    idx = idx_ref[...]
    acc = jnp.full(idx.shape, -jnp.inf, F32)
    for bkt in range(REL_BUCKETS):
        acc = jnp.where(idx == bkt, rb_ref[bkt, h], acc)
    o_ref[0] = acc


def _moba_bias_tiles(rel_bias):
    idx = jnp.asarray(_moba_bucket_table())
    shp = (2, MOBA_BLOCK, MOBA_BLOCK)
    return pl.pallas_call(
        _bias_kernel,
        out_shape=jax.ShapeDtypeStruct((HEADS,) + shp, F32),
        grid=(HEADS,),
        in_specs=[pl.BlockSpec(shp, lambda h: (0, 0, 0)),
                  pl.BlockSpec(memory_space=pltpu.SMEM)],
        out_specs=pl.BlockSpec((1,) + shp, lambda h: (h, 0, 0, 0)),
        name="moba_bias",
    )(idx, rel_bias)


def _moba_kernel(q_ref, k_ref, v_ref, gate_ref, bias_ref, rb_ref, o_ref,
                 kmean_ref, kbf_ref, vt_ref, sel_ref, m_ref, l_ref, acc_ref, *, nb):
    BLK = MOBA_BLOCK
    ib = pl.program_id(1)
    heads = range(HEADS)
    rows_of = lambda h: slice(h * HDIM, (h + 1) * HDIM)
    neg_inf = -jnp.inf
    not_selected = -1e30

    @pl.when(ib == 0)
    def _():
        kmean_ref[...] = jnp.zeros_like(kmean_ref)
        for n in range(nb):
            blk = slice(n * BLK, (n + 1) * BLK)
            kblk = k_ref[0, blk, :]
            kmean_ref[n:n + 1, :] = jnp.mean(kblk, axis=0, keepdims=True)
            kbf_ref[n] = kblk.astype(BF16)
            vt_ref[n, 0:GROUP, :] = v_ref[0, blk, :].T.astype(BF16)
            vt_ref[n, GROUP:GROUP + 16, :] = jnp.ones((16, BLK), BF16)

    q = q_ref[0] * (HDIM ** -0.5)
    q_bf = q.astype(BF16)
    q_s = _split(q)
    lane_head = _iota((1, GROUP), 1) // HDIM
    q_heads = [jnp.where(lane_head == h, q_bf, jnp.zeros_like(q_bf)) for h in heads]

    km = kmean_ref[...]
    gates = [_dot3(_nt, _split(jnp.where(lane_head == h, km, 0.0)), q_s) for h in heads]
    blk_id = _iota((16, BLK), 0)
    for h in heads:
        g = jnp.where(blk_id < ib, gates[h], neg_inf)
        sel = jnp.zeros((16, BLK), F32)
        for _ in range(min(MOBA_TOPK, nb)):
            m = jnp.max(g, axis=0, keepdims=True)
            hit = (g == m) & (m > neg_inf)
            first = jnp.min(jnp.where(hit, blk_id, 16), axis=0, keepdims=True)
            pick = blk_id == first
            sel = jnp.where(pick, 1.0, sel)
            g = jnp.where(pick, neg_inf, g)
        sel_ref[h] = sel

    def attend(k_blk, vt_blk, extra, first):
        scores = [_nt(k_blk, q_heads[h]) for h in heads]
        probs, alphas = [], []
        for h in heads:
            s = scores[h] + extra[h]
            m_blk = jnp.max(s, axis=0, keepdims=True)
            if first:
                m_new = m_blk
            else:
                m_old = m_ref[h]
                m_new = jnp.maximum(m_old, m_blk)
                alphas.append(jnp.exp(m_old - m_new))
            m_ref[h] = m_new
            probs.append(jnp.exp(s - m_new).astype(BF16))
        for h in heads:
            lhs = jnp.concatenate([vt_blk[rows_of(h), :], vt_blk[GROUP:GROUP + 16, :]], axis=0)
            pv = _nn(lhs, probs[h])
            if first:
                acc_ref[rows_of(h), :] = pv[0:HDIM]
                l_ref[h] = pv[HDIM:HDIM + 1]
            else:
                acc_ref[rows_of(h), :] = alphas[h] * acc_ref[rows_of(h), :] + pv[0:HDIM]
                l_ref[h] = alphas[h] * l_ref[h] + pv[HDIM:HDIM + 1]

    attend(kbf_ref[ib], vt_ref[ib], [bias_ref[h, 1] for h in heads], first=True)

    def selected_row(h, n):
        return jnp.where(sel_ref[h, pl.ds(n, 1), :] > 0.5, 0.0, not_selected)

    @pl.when(ib >= 1)
    def _():
        attend(kbf_ref[ib - 1], vt_ref[ib - 1],
               [bias_ref[h, 0] + selected_row(h, ib - 1) for h in heads], first=False)

    for n in range(nb - 2):
        @pl.when(n < ib - 1)
        def _(n=n):
            attend(kbf_ref[n], vt_ref[n],
                   [selected_row(h, n) + rb_ref[REL_BUCKETS - 1, h] for h in heads], first=False)

    for h in heads:
        acc_ref[rows_of(h), :] = acc_ref[rows_of(h), :] / l_ref[h]
    o_ref[0] = acc_ref[...].T * _silu(gate_ref[0])


def _moba(p3, bias_tiles, rel_bias):
    b, t, _ = p3.shape
    BLK = MOBA_BLOCK
    nb = t // BLK
    assert 2 <= nb <= 16 and t % BLK == 0
    base = 2 * 4
    tile = lambda c: pl.BlockSpec((1, BLK, GROUP), lambda i, j, c=c: (i, j, c))
    seq = lambda c: pl.BlockSpec((1, t, GROUP), lambda i, j, c=c: (i, 0, c))
    return pl.pallas_call(
        functools.partial(_moba_kernel, nb=nb),
        out_shape=jax.ShapeDtypeStruct((b, t, GROUP), F32),
        grid=(b, nb),
        in_specs=[tile(base), seq(base + 1), seq(base + 2), tile(base + 3),
                  pl.BlockSpec(bias_tiles.shape, lambda i, j: (0, 0, 0, 0)),
                  pl.BlockSpec(memory_space=pltpu.SMEM)],
        out_specs=pl.BlockSpec((1, BLK, GROUP), lambda i, j: (i, j, 0)),
        scratch_shapes=[pltpu.VMEM((16, GROUP), F32),
                        pltpu.VMEM((nb, BLK, GROUP), BF16),
                        pltpu.VMEM((nb, GROUP + 16, BLK), BF16),
                        pltpu.VMEM((HEADS, 16, BLK), F32),
                        pltpu.VMEM((HEADS, 1, BLK), F32),
                        pltpu.VMEM((HEADS, 1, BLK), F32),
                        pltpu.VMEM((GROUP, BLK), F32)],
        compiler_params=pltpu.CompilerParams(
            dimension_semantics=("arbitrary", "arbitrary"), vmem_limit_bytes=VMEM_LIMIT_V7X),
        name="moba",
    )(p3, p3, p3, p3, bias_tiles, rel_bias)


def _ret_tables(t):
    half = HDIM // 2
    theta = 1.0 / (10000.0 ** jnp.linspace(0.0, 1.0, half))
    pos = jnp.arange(t, dtype=F32)
    ang = pos[:, None] * theta[None, :]
    sin, cos = jnp.sin(ang), jnp.cos(ang)
    cos2 = jnp.tile(jnp.repeat(cos, 2, axis=1), (1, HEADS))
    sin2 = jnp.tile(jnp.stack([-sin, sin], axis=-1).reshape(t, HDIM), (1, HEADS))
    log_gamma = jnp.log(1.0 - 2.0 ** (-5.0 - jnp.arange(HEADS, dtype=F32)))
    idx = jnp.arange(RET_CHUNK, dtype=F32)
    diff = idx[:, None] - idx[None, :]
    decay_intra = jnp.where(diff >= 0, jnp.exp(log_gamma[:, None, None] * jnp.maximum(diff, 0.0)), 0.0)
    q_decay = jnp.exp(log_gamma[:, None] * (idx + 1.0))
    k_decay = jnp.exp(log_gamma[:, None] * (RET_CHUNK - 1.0 - idx))
    chunk_decay = jnp.exp(log_gamma * RET_CHUNK)
    lanes = lambda a: jnp.repeat(a.T, HDIM, axis=1)
    cd = jnp.repeat(chunk_decay, HDIM).reshape(1, GROUP)
    return cos2, sin2, decay_intra, lanes(q_decay), lanes(k_decay), cd


def _ret_kernel(q_ref, k_ref, v_ref, gate_ref, cos_ref, sin_ref, di_ref, qd_ref, kd_ref, cd_ref,
                o_ref, state_ref, *, tt):
    C = RET_CHUNK
    n_chunks = tt // C
    chunks = range(n_chunks)
    heads = range(HEADS)
    b = pl.program_id(1)

    @pl.when(pl.program_id(0) == 0)
    def _():
        state_ref[b] = jnp.zeros((GROUP, GROUP), F32)

    even = (_iota((tt, GROUP), 1) % 2) == 0

    def rotate(x):
        swapped = jnp.where(even, pltpu.roll(x, GROUP - 1, axis=1), pltpu.roll(x, 1, axis=1))
        return x * cos_ref[...] + swapped * sin_ref[...]

    q = rotate(q_ref[0])
    k = rotate(k_ref[0]) * (HDIM ** -0.5)
    lane_head = _iota((1, GROUP), 1) // HDIM
    same_head = (_iota((GROUP, GROUP), 0) // HDIM) == (_iota((GROUP, GROUP), 1) // HDIM)
    rows = lambda c: slice(c * C, (c + 1) * C)
    mask_pair = lambda pr, h: (jnp.where(lane_head == h, pr[0], jnp.zeros_like(pr[0])),
                               jnp.where(lane_head == h, pr[1], jnp.zeros_like(pr[1])))

    q_s = [_split(q[rows(c)]) for c in chunks]
    k_s = [_split(k[rows(c)]) for c in chunks]
    v_s = [_split(v_ref[0, rows(c), :]) for c in chunks]
    kd_s = [_split(k[rows(c)] * kd_ref[...]) for c in chunks]
    qd_s = [_split(q[rows(c)] * qd_ref[...]) for c in chunks]

    inner_s = [_split(_dot3(_nt, _pcat([mask_pair(q_s[c], h) for h in heads], 0), k_s[c]) * di_ref[...])
               for c in chunks]
    kv = [jnp.where(same_head, _dot3(_tn, kd_s[c], v_s[c]), 0.0) for c in chunks]
    intra = []
    for c in chunks:
        stacked = _dot3(_nn, inner_s[c], v_s[c])
        acc = jnp.where(lane_head == 0, stacked[0:C], 0.0)
        for h in range(1, HEADS):
            acc = jnp.where(lane_head == h, stacked[h * C:(h + 1) * C], acc)
        intra.append(acc)
    state = state_ref[b]
    states = []
    for c in chunks:
        states.append(_split(state))
        state = state * cd_ref[...] + kv[c]
    state_ref[b] = state
    y = jnp.concatenate([intra[c] + _dot3(_nn, qd_s[c], states[c]) for c in chunks], axis=0)
    mean_sq = _head_sum(y * y, _head_sum_matrix().astype(BF16)) * (1.0 / HDIM)
    o_ref[0] = y * lax.rsqrt(mean_sq + NORM_EPS) * _silu(gate_ref[0])


def _ret(p3, tables, tt=512):
    b, t, _ = p3.shape
    base = 3 * 4
    cos2, sin2, di, qd, kd, cd = tables
    di = di.reshape(HEADS * RET_CHUNK, RET_CHUNK)
    blk = lambda c: pl.BlockSpec((1, tt, GROUP), lambda j, i, c=c: (i, j, c))
    full2 = lambda a: pl.BlockSpec(a.shape, lambda j, i: (0, 0))
    return pl.pallas_call(
        functools.partial(_ret_kernel, tt=tt),
        out_shape=jax.ShapeDtypeStruct((b, t, GROUP), F32),
        grid=(t // tt, b),
        in_specs=[blk(base), blk(base + 1), blk(base + 2), blk(base + 3),
                  pl.BlockSpec((tt, GROUP), lambda j, i: (j, 0)),
                  pl.BlockSpec((tt, GROUP), lambda j, i: (j, 0)),
                  full2(di), full2(qd), full2(kd), full2(cd)],
        out_specs=pl.BlockSpec((1, tt, GROUP), lambda j, i: (i, j, 0)),
        scratch_shapes=[pltpu.VMEM((b, GROUP, GROUP), F32)],
        compiler_params=pltpu.CompilerParams(
            dimension_semantics=("arbitrary", "arbitrary"), vmem_limit_bytes=VMEM_LIMIT_V7X),
        name="retnet",
    )(p3, p3, p3, p3, cos2, sin2, di, qd, kd, cd)


def _reorder_w_in(w):
    g3 = 3 * GROUP
    lora = w[:, g3:g3 + 2 * LORA]
    pad = jnp.zeros((w.shape[0], 128 - 2 * LORA), w.dtype)
    return jnp.concatenate([w[:, :g3], w[:, g3 + 2 * LORA:], lora, pad], axis=1)


def kernel(x, norm_w, w_in, w_out, rwkv_mu, rwkv_w0, rwkv_w2, rwkv_a0, rwkv_a2, rwkv_k_k, rwkv_k_a,
           rwkv_r_k, rwkv_lnx_w, rwkv_lnx_b, conv_w, rel_bias, final_norm_w):
    b, t, d = x.shape
    depth = w_in.shape[0]
    bias_tiles = _moba_bias_tiles(rel_bias)
    ret_tables = _ret_tables(t)
    x2 = x.reshape(b * t, d)
    for l in range(depth):
        w_l = _reorder_w_in(w_in[l]).astype(BF16)
        p3 = _inproj(x2, norm_w[l], w_l).reshape(b, t, P_COLS)
        y_rwkv = _rwkv(p3, rwkv_mu[l], rwkv_w0[l], rwkv_w2[l], rwkv_a0[l], rwkv_a2[l], rwkv_k_k[l],
                       rwkv_k_a[l], rwkv_r_k[l].reshape(GROUP), rwkv_lnx_w[l], rwkv_lnx_b[l])
        y_conv = _conv(p3, conv_w[l])
        y_moba = _moba(p3, bias_tiles, rel_bias)
        y_ret = _ret(p3, ret_tables)
        ys = [y.reshape(b * t, GROUP) for y in (y_rwkv, y_conv, y_moba, y_ret)]
        x2 = _outproj(x2, ys, w_out[l].astype(BF16), final_norm_w, final=(l == depth - 1))
    return x2.reshape(b, t, d)
```

```python
import functools
import math

import numpy as np
import jax
import jax.numpy as jnp
from jax import lax
from jax.experimental import pallas as pl
from jax.experimental.pallas import tpu as pltpu

F32 = jnp.float32
BF16 = jnp.bfloat16
HIGHEST = lax.Precision.HIGHEST

GROUP = 256
HEADS = 4
HDIM = 64
LORA = 32
DECAY_SCALE = math.exp(-0.5)
LNX_EPS = 64e-5
NORM_EPS = 1e-6
CONV_TAPS = 3
MOBA_BLOCK = 256
MOBA_TOPK = 3
MOBA_QBLOCK = 128
RET_CHUNK = 128
REL_BUCKETS = 32
REL_MAX_DIST = 128
RWKV_CHUNK = 64

P_COLS = 4 * 4 * GROUP + 128
LORA_COL = 4 * 4 * GROUP

VMEM_LIMIT_V7X = 48 * 1024 * 1024


def _nn(a, b, precision=None):
    return lax.dot_general(a, b, (((1,), (0,)), ((), ())), precision=precision,
                           preferred_element_type=F32)


def _nt(a, b, precision=None):
    return lax.dot_general(a, b, (((1,), (1,)), ((), ())), precision=precision,
                           preferred_element_type=F32)


def _tn(a, b, precision=None):
    return lax.dot_general(a, b, (((0,), (0,)), ((), ())), precision=precision,
                           preferred_element_type=F32)


def _split(x):
    hi = x.astype(BF16)
    lo = (x - hi.astype(F32)).astype(BF16)
    return hi, lo


def _dot3(dot, a, b):
    out_axis = 1 if dot is _tn else 0
    m = a[0].shape[out_axis]
    both = dot(jnp.concatenate([a[0], a[1]], axis=out_axis), b[0])
    return both[:m] + both[m:] + dot(a[0], b[1])


def _psl(pair, rows, cols):
    return pair[0][rows, cols], pair[1][rows, cols]


def _pcat(pairs, axis):
    return (jnp.concatenate([p[0] for p in pairs], axis=axis),
            jnp.concatenate([p[1] for p in pairs], axis=axis))


def _head_sum(x, esum_bf16):
    hi = x.astype(BF16)
    r1 = x - hi.astype(F32)
    mid = r1.astype(BF16)
    lo = (r1 - mid.astype(F32)).astype(BF16)
    return _nn(hi, esum_bf16) + (_nn(mid, esum_bf16) + _nn(lo, esum_bf16))


def _iota(shape, dim):
    return lax.broadcasted_iota(jnp.int32, shape, dim)


def _head_sum_matrix():
    r = _iota((GROUP, GROUP), 0) // HDIM
    c = _iota((GROUP, GROUP), 1) // HDIM
    return jnp.where(r == c, 1.0, 0.0).astype(F32)


def _silu(x):
    return x * jax.nn.sigmoid(x)


def _inproj_kernel(x_ref, nw_ref, w_ref, o_ref):
    x = x_ref[...]
    h = x * lax.rsqrt(jnp.mean(x * x, axis=-1, keepdims=True) + NORM_EPS) * nw_ref[...]
    o_ref[...] = _nn(h.astype(BF16), w_ref[...])


def _inproj(x2d, norm_w, w_bf16, tm=256):
    m, d = x2d.shape
    n = w_bf16.shape[1]
    return pl.pallas_call(
        _inproj_kernel,
        out_shape=jax.ShapeDtypeStruct((m, n), F32),
        grid=(m // tm,),
        in_specs=[pl.BlockSpec((tm, d), lambda i: (i, 0)),
                  pl.BlockSpec((1, d), lambda i: (0, 0)),
                  pl.BlockSpec((d, n), lambda i: (0, 0))],
        out_specs=pl.BlockSpec((tm, n), lambda i: (i, 0)),
        compiler_params=pltpu.CompilerParams(
            dimension_semantics=("arbitrary",), vmem_limit_bytes=VMEM_LIMIT_V7X),
        name="inproj",
    )(x2d, norm_w.reshape(1, d), w_bf16)


def _outproj_kernel(x_ref, y0_ref, y1_ref, y2_ref, y3_ref, w_ref, fw_ref, o_ref, *, final):
    acc = x_ref[...]
    for g, y_ref in enumerate((y0_ref, y1_ref, y2_ref, y3_ref)):
        acc = acc + _nn(y_ref[...].astype(BF16), w_ref[g * GROUP:(g + 1) * GROUP, :])
    if final:
        acc = acc * lax.rsqrt(jnp.mean(acc * acc, axis=-1, keepdims=True) + NORM_EPS) * fw_ref[...]
    o_ref[...] = acc


def _outproj(x2d, ys, w_bf16, final_w, final, tm=512):
    m, d = x2d.shape
    yspec = pl.BlockSpec((tm, GROUP), lambda i: (i, 0))
    return pl.pallas_call(
        functools.partial(_outproj_kernel, final=final),
        out_shape=jax.ShapeDtypeStruct((m, d), F32),
        grid=(m // tm,),
        in_specs=[pl.BlockSpec((tm, d), lambda i: (i, 0)), yspec, yspec, yspec, yspec,
                  pl.BlockSpec((4 * GROUP, d), lambda i: (0, 0)),
                  pl.BlockSpec((1, d), lambda i: (0, 0))],
        out_specs=pl.BlockSpec((tm, d), lambda i: (i, 0)),
        compiler_params=pltpu.CompilerParams(
            dimension_semantics=("arbitrary",), vmem_limit_bytes=VMEM_LIMIT_V7X),
        name="outproj_final" if final else "outproj",
    )(x2d, *ys, w_bf16, final_w.reshape(1, d))


def _rwkv_kernel(rkv_ref, gate_ref, lora_ref, mu_ref, mul_ref, w0_ref, w2_ref, a0_ref, a2_ref,
                 kk_ref, ka_ref, rk_ref, lnw_ref, lnb_ref, o_ref,
                 state_ref, prev_ref, prevl_ref, *, tt):
    C = RWKV_CHUNK
    t_idx = pl.program_id(1)

    @pl.when(t_idx == 0)
    def _():
        state_ref[...] = jnp.zeros_like(state_ref)
        prev_ref[...] = jnp.zeros_like(prev_ref)
        prevl_ref[...] = jnp.zeros_like(prevl_ref)

    p = rkv_ref[0]
    lo = lora_ref[0]
    row0 = _iota((tt, 1), 0) == 0
    p_sh = jnp.where(row0, prev_ref[...], pltpu.roll(p, 1, axis=0))
    lo_sh = jnp.where(row0, prevl_ref[...], pltpu.roll(lo, 1, axis=0))
    prev_ref[...] = p[tt - 1:tt, :]
    prevl_ref[...] = lo[tt - 1:tt, :]
    p = p + (p_sh - p) * mu_ref[...]
    lo = lo + (lo_sh - lo) * mul_ref[...]

    r = p[:, 0:GROUP]
    k = p[:, GROUP:2 * GROUP]
    v = p[:, 2 * GROUP:3 * GROUP]

    esum = _head_sum_matrix().astype(BF16)
    lw = -DECAY_SCALE * jax.nn.sigmoid(
        w0_ref[...] + _dot3(_nn, _split(jnp.tanh(lo)), _split(w2_ref[...])))
    rate = jax.nn.sigmoid(a0_ref[...] + _dot3(_nn, _split(lo), _split(a2_ref[...])))
    kk = k * kk_ref[...]
    kk = kk / jnp.maximum(jnp.sqrt(_head_sum(kk * kk, esum)), 1e-12)
    k2 = k * (1.0 + (rate - 1.0) * ka_ref[...])
    bonus = _head_sum(r * k2 * rk_ref[...], esum) * v
    bvec = kk * rate
    avec = -kk

    rowmod = _iota((tt, 1), 0) % C
    cum = lw
    for sh in (1, 2, 4, 8, 16, 32):
        cum = cum + jnp.where(rowmod >= sh, pltpu.roll(cum, sh, axis=0), 0.0)
    tot_rows = [cum[c * C + C - 1:c * C + C, :] for c in range(tt // C)]
    tot = jnp.concatenate([jnp.broadcast_to(tr, (C, GROUP)) for tr in tot_rows], axis=0)
    e_neg = jnp.exp(-cum)
    e_end = jnp.exp(tot - cum)
    r_t = _split(r * jnp.exp(cum))
    a_t = _split(avec * jnp.exp(cum - lw))
    k_t = _split(k2 * e_neg)
    b_t = _split(bvec * e_neg)
    k_h = _split(k2 * e_end)
    b_h = _split(bvec * e_end)
    v_s = _split(v)

    same_head = (_iota((GROUP, GROUP), 0) // HDIM) == (_iota((GROUP, GROUP), 1) // HDIM)

    def bd(pr):
        return tuple(jnp.where(same_head, jnp.concatenate([part] * HEADS, axis=0), jnp.zeros((), part.dtype))
                     for part in pr)

    t_i = _iota((2 * C, GROUP), 0)
    s_i = _iota((2 * C, GROUP), 1) % C
    tri = ((t_i < C) & (s_i < t_i)) | ((t_i >= C) & (s_i <= t_i - C))
    eye = jnp.where(_iota((C, GROUP), 0) == _iota((C, GROUP), 1) % C, 1.0, 0.0).astype(F32)

    chunks = range(tt // C)
    every = slice(None)
    rows = lambda pr, c: _psl(pr, slice(c * C, (c + 1) * C), every)
    top = lambda pr: _psl(pr, slice(0, C), every)
    bottom = lambda pr: _psl(pr, slice(C, 2 * C), every)

    ar = [_pcat([rows(a_t, c), rows(r_t, c)], 0) for c in chunks]
    bd_v = [bd(rows(v_s, c)) for c in chunks]
    ab_rb = [jnp.where(tri, _dot3(_nt, ar[c], bd(rows(b_t, c))), 0.0) for c in chunks]
    ak_rk = [_split(jnp.where(tri, _dot3(_nt, ar[c], bd(rows(k_t, c))), 0.0)) for c in chunks]
    a_ab = [ab_rb[c][0:C] for c in chunks]
    ab_rb = [_split(x) for x in ab_rb]
    t_row = _iota((C, GROUP), 0)
    s_col = _iota((C, GROUP), 1) % C

    def below_diagonal(n):
        return ((t_row // (2 * n)) == (s_col // (2 * n))) & ((t_row % (2 * n)) >= n) & ((s_col % (2 * n)) < n)

    inv = [eye + jnp.where(below_diagonal(1), a_ab[c], 0.0) for c in chunks]
    n = 2
    while n < C:
        off = below_diagonal(n)
        inv_s = [_split(inv[c]) for c in chunks]
        left = [_split(_dot3(_nn, inv_s[c], bd(_split(jnp.where(off, a_ab[c], 0.0))))) for c in chunks]
        inv = [inv[c] + _dot3(_nn, left[c], bd(inv_s[c])) for c in chunks]
        n *= 2
    inv = [_split(x) for x in inv]
    t_ak = [_split(_dot3(_nn, inv[c], bd(top(ak_rk[c])))) for c in chunks]
    t_a = [_split(_dot3(_nn, inv[c], bd(rows(a_t, c)))) for c in chunks]
    free = [_dot3(_nn, _pcat([t_ak[c], bottom(ak_rk[c])], 0), bd_v[c]) for c in chunks]
    kb_hat = [_pcat([rows(k_h, c), rows(b_h, c)], 0) for c in chunks]
    m_s = [_split(jnp.where(same_head, _dot3(_tn, t_a[c], rows(b_h, c)), 0.0)) for c in chunks]
    g = []
    for c in chunks:
        full = jnp.where(same_head, _dot3(_tn, _pcat([rows(v_s, c), _split(free[c][0:C])], 0), kb_hat[c]), 0.0)
        g.append((full[0:HDIM] + full[HDIM:2 * HDIM]) + (full[2 * HDIM:3 * HDIM] + full[3 * HDIM:4 * HDIM]))
    state = state_ref[...]
    states = []
    for c in chunks:
        st_s = _split(state)
        states.append(bd(st_s))
        state = state * jnp.exp(tot_rows[c]) + _dot3(_nn, st_s, m_s[c]) + g[c]
    state_ref[...] = state
    from_state = [_dot3(_nt, _pcat([t_a[c], rows(r_t, c)], 0), states[c]) for c in chunks]
    y = jnp.concatenate(
        [free[c][C:2 * C] + from_state[c][C:2 * C]
         + _dot3(_nn, bottom(ab_rb[c]), bd(_split(free[c][0:C] + from_state[c][0:C]))) for c in chunks], axis=0)
    mean = _head_sum(y, esum) * (1.0 / HDIM)
    yc = y - mean
    var = _head_sum(yc * yc, esum) * (1.0 / HDIM)
    y = yc * lax.rsqrt(var + LNX_EPS) * lnw_ref[...] + lnb_ref[...] + bonus
    o_ref[0] = y * _silu(gate_ref[0])


def _rwkv(p3, mu, w0, w2, a0, a2, k_k, k_a, r_k, lnx_w, lnx_b, tt=256):
    b, t, _ = p3.shape
    mu_main = mu[:3 * GROUP].reshape(1, 3 * GROUP)
    mu_lora = jnp.concatenate([mu[3 * GROUP:], jnp.zeros((128 - 2 * LORA,), F32)]).reshape(1, 128)
    w2p = jnp.zeros((128, GROUP), F32).at[0:LORA].set(w2)
    a2p = jnp.zeros((128, GROUP), F32).at[LORA:2 * LORA].set(a2)
    row = lambda a: a.reshape(1, GROUP)
    vec = pl.BlockSpec((1, GROUP), lambda i, j: (0, 0))
    return pl.pallas_call(
        functools.partial(_rwkv_kernel, tt=tt),
        out_shape=jax.ShapeDtypeStruct((b, t, GROUP), F32),
        grid=(b, t // tt),
        in_specs=[pl.BlockSpec((1, tt, 3 * GROUP), lambda i, j: (i, j, 0)),
                  pl.BlockSpec((1, tt, GROUP), lambda i, j: (i, j, 3)),
                  pl.BlockSpec((1, tt, 128), lambda i, j: (i, j, LORA_COL // 128)),
                  pl.BlockSpec((1, 3 * GROUP), lambda i, j: (0, 0)),
                  pl.BlockSpec((1, 128), lambda i, j: (0, 0)),
                  vec,
                  pl.BlockSpec((128, GROUP), lambda i, j: (0, 0)),
                  vec,
                  pl.BlockSpec((128, GROUP), lambda i, j: (0, 0)),
                  vec, vec, vec, vec, vec],
        out_specs=pl.BlockSpec((1, tt, GROUP), lambda i, j: (i, j, 0)),
        scratch_shapes=[pltpu.VMEM((HDIM, GROUP), F32),
                        pltpu.VMEM((1, 3 * GROUP), F32),
                        pltpu.VMEM((1, 128), F32)],
        compiler_params=pltpu.CompilerParams(
            dimension_semantics=("arbitrary", "arbitrary"), vmem_limit_bytes=VMEM_LIMIT_V7X),
        name="rwkv7",
    )(p3, p3, p3, mu_main, mu_lora, row(w0), w2p, row(a0), a2p,
      row(k_k), row(k_a), row(r_k), row(lnx_w), row(lnx_b))


def _conv_kernel(p_ref, cw_ref, o_ref, tail_ref, *, tt):
    @pl.when(pl.program_id(1) == 0)
    def _():
        tail_ref[...] = jnp.zeros_like(tail_ref)

    p = p_ref[0]
    u = p[:, GROUP:2 * GROUP] * p[:, 2 * GROUP:3 * GROUP]
    rows = _iota((tt, 1), 0)
    u1 = jnp.where(rows == 0, tail_ref[1:2, :], pltpu.roll(u, 1, axis=0))
    u2 = jnp.where(rows == 0, tail_ref[0:1, :],
                   jnp.where(rows == 1, tail_ref[1:2, :], pltpu.roll(u, 2, axis=0)))
    tail_ref[0:2, :] = u[tt - 2:tt, :]
    y = u2 * cw_ref[0:1, :] + u1 * cw_ref[1:2, :] + u * cw_ref[2:3, :]
    o_ref[0] = p[:, 0:GROUP] * y * _silu(p[:, 3 * GROUP:4 * GROUP])


def _conv(p3, conv_w, tt=512):
    b, t, _ = p3.shape
    return pl.pallas_call(
        functools.partial(_conv_kernel, tt=tt),
        out_shape=jax.ShapeDtypeStruct((b, t, GROUP), F32),
        grid=(b, t // tt),
        in_specs=[pl.BlockSpec((1, tt, 4 * GROUP), lambda i, j: (i, j, 1)),
                  pl.BlockSpec((CONV_TAPS, GROUP), lambda i, j: (0, 0))],
        out_specs=pl.BlockSpec((1, tt, GROUP), lambda i, j: (i, j, 0)),
        scratch_shapes=[pltpu.VMEM((8, GROUP), F32)],
        compiler_params=pltpu.CompilerParams(
            dimension_semantics=("arbitrary", "arbitrary"), vmem_limit_bytes=VMEM_LIMIT_V7X),
        name="sconv",
    )(p3, conv_w)


def _t5_bucket_np(dist):
    max_exact = REL_BUCKETS // 2
    d_f = np.maximum(dist, 1).astype(np.float32)
    large = max_exact + (np.log(d_f / np.float32(max_exact)) / np.float32(math.log(REL_MAX_DIST / max_exact))
                         * np.float32(REL_BUCKETS - max_exact)).astype(np.int32)
    large = np.minimum(large, REL_BUCKETS - 1)
    return np.where(dist < max_exact, dist, large).astype(np.int32)


MASKED_BUCKET = REL_BUCKETS


def _moba_bucket_table():
    keys = np.arange(MOBA_BLOCK)[:, None]
    queries = np.arange(MOBA_BLOCK)[None, :]
    prev = _t5_bucket_np(queries + MOBA_BLOCK - keys)
    own = np.where(keys <= queries, _t5_bucket_np(np.maximum(queries - keys, 0)), MASKED_BUCKET)
    return np.stack([prev, own]).astype(np.int32)


def _bias_kernel(idx_ref, rb_ref, o_ref):
    h = pl.program_id(0)
    idx = idx_ref[...]
    acc = jnp.full(idx.shape, -jnp.inf, F32)
    for bkt in range(REL_BUCKETS):
        acc = jnp.where(idx == bkt, rb_ref[bkt, h], acc)
    o_ref[0] = acc


def _moba_bias_tiles(rel_bias):
    idx = jnp.asarray(_moba_bucket_table())
    shp = (2, MOBA_BLOCK, MOBA_BLOCK)
    return pl.pallas_call(
        _bias_kernel,
        out_shape=jax.ShapeDtypeStruct((HEADS,) + shp, F32),
        grid=(HEADS,),
        in_specs=[pl.BlockSpec(shp, lambda h: (0, 0, 0)),
                  pl.BlockSpec(memory_space=pltpu.SMEM)],
        out_specs=pl.BlockSpec((1,) + shp, lambda h: (h, 0, 0, 0)),
        name="moba_bias",
    )(idx, rel_bias)


def _moba_kernel(q_ref, k_ref, v_ref, gate_ref, bias_ref, rb_ref, o_ref,
                 kmean_ref, kbf_ref, vt_ref, sel_ref, m_ref, l_ref, acc_ref, *, nb):
    BLK = MOBA_BLOCK
    ib = pl.program_id(1)
    heads = range(HEADS)
    rows_of = lambda h: slice(h * HDIM, (h + 1) * HDIM)
    neg_inf = -jnp.inf
    not_selected = -1e30

    @pl.when(ib == 0)
    def _():
        kmean_ref[...] = jnp.zeros_like(kmean_ref)
        for n in range(nb):
            blk = slice(n * BLK, (n + 1) * BLK)
            kblk = k_ref[0, blk, :]
            kmean_ref[n:n + 1, :] = jnp.mean(kblk, axis=0, keepdims=True)
            kbf_ref[n] = kblk.astype(BF16)
            vt_ref[n, 0:GROUP, :] = v_ref[0, blk, :].T.astype(BF16)
            vt_ref[n, GROUP:GROUP + 16, :] = jnp.ones((16, BLK), BF16)

    q = q_ref[0] * (HDIM ** -0.5)
    q_bf = q.astype(BF16)
    q_s = _split(q)
    lane_head = _iota((1, GROUP), 1) // HDIM
    q_heads = [jnp.where(lane_head == h, q_bf, jnp.zeros_like(q_bf)) for h in heads]

    km = kmean_ref[...]
    gates = [_dot3(_nt, _split(jnp.where(lane_head == h, km, 0.0)), q_s) for h in heads]
    blk_id = _iota((16, BLK), 0)
    for h in heads:
        g = jnp.where(blk_id < ib, gates[h], neg_inf)
        sel = jnp.zeros((16, BLK), F32)
        for _ in range(min(MOBA_TOPK, nb)):
            m = jnp.max(g, axis=0, keepdims=True)
            hit = (g == m) & (m > neg_inf)
            first = jnp.min(jnp.where(hit, blk_id, 16), axis=0, keepdims=True)
            pick = blk_id == first
            sel = jnp.where(pick, 1.0, sel)
            g = jnp.where(pick, neg_inf, g)
        sel_ref[h] = sel

    def attend(blocks, first):
        scores = [[_nt(kbf_ref[n], q_heads[h]) for h in heads] for n, _ in blocks]
        probs, alphas = [], []
        for h in heads:
            ss = [scores[j][h] + extra[h] for j, (_, extra) in enumerate(blocks)]
            m_new = functools.reduce(jnp.maximum, [jnp.max(s, axis=0, keepdims=True) for s in ss])
            if not first:
                m_old = m_ref[h]
                m_new = jnp.maximum(m_old, m_new)
                alphas.append(jnp.exp(m_old - m_new))
            m_ref[h] = m_new
            probs.append([jnp.exp(s - m_new).astype(BF16) for s in ss])
        for h in heads:
            pv = None
            for j, (n, _) in enumerate(blocks):
                lhs = jnp.concatenate([vt_ref[n, rows_of(h), :], vt_ref[n, GROUP:GROUP + 16, :]], axis=0)
                part = _nn(lhs, probs[h][j])
                pv = part if pv is None else pv + part
            if first:
                acc_ref[rows_of(h), :] = pv[0:HDIM]
                l_ref[h] = pv[HDIM:HDIM + 1]
            else:
                acc_ref[rows_of(h), :] = alphas[h] * acc_ref[rows_of(h), :] + pv[0:HDIM]
                l_ref[h] = alphas[h] * l_ref[h] + pv[HDIM:HDIM + 1]

    def selected_row(h, n, valid):
        row = jnp.where(sel_ref[h, pl.ds(n, 1), :] > 0.5, 0.0, not_selected)
        return jnp.where(valid, row, not_selected)

    prev = jnp.maximum(ib - 1, 0)
    attend([(ib, [bias_ref[h, 1] for h in heads]),
            (prev, [bias_ref[h, 0] + selected_row(h, prev, ib >= 1) for h in heads])], first=True)

    n_far = ib - 1
    for n in range(0, nb - 2, 2):
        @pl.when(n < n_far)
        def _(n=n):
            pair = [n] if n + 1 >= nb - 2 else [n, n + 1]
            attend([(j, [selected_row(h, j, j < n_far) + rb_ref[REL_BUCKETS - 1, h] for h in heads])
                    for j in pair], first=False)

    for h in heads:
        acc_ref[rows_of(h), :] = acc_ref[rows_of(h), :] / l_ref[h]
    o_ref[0] = acc_ref[...].T * _silu(gate_ref[0])


def _moba(p3, bias_tiles, rel_bias):
    b, t, _ = p3.shape
    BLK = MOBA_BLOCK
    nb = t // BLK
    assert 2 <= nb <= 16 and t % BLK == 0
    base = 2 * 4
    tile = lambda c: pl.BlockSpec((1, BLK, GROUP), lambda i, j, c=c: (i, j, c))
    seq = lambda c: pl.BlockSpec((1, t, GROUP), lambda i, j, c=c: (i, 0, c))
    return pl.pallas_call(
        functools.partial(_moba_kernel, nb=nb),
        out_shape=jax.ShapeDtypeStruct((b, t, GROUP), F32),
        grid=(b, nb),
        in_specs=[tile(base), seq(base + 1), seq(base + 2), tile(base + 3),
                  pl.BlockSpec(bias_tiles.shape, lambda i, j: (0, 0, 0, 0)),
                  pl.BlockSpec(memory_space=pltpu.SMEM)],
        out_specs=pl.BlockSpec((1, BLK, GROUP), lambda i, j: (i, j, 0)),
        scratch_shapes=[pltpu.VMEM((16, GROUP), F32),
                        pltpu.VMEM((nb, BLK, GROUP), BF16),
                        pltpu.VMEM((nb, GROUP + 16, BLK), BF16),
                        pltpu.VMEM((HEADS, 16, BLK), F32),
                        pltpu.VMEM((HEADS, 1, BLK), F32),
                        pltpu.VMEM((HEADS, 1, BLK), F32),
                        pltpu.VMEM((GROUP, BLK), F32)],
        compiler_params=pltpu.CompilerParams(
            dimension_semantics=("arbitrary", "arbitrary"), vmem_limit_bytes=VMEM_LIMIT_V7X),
        name="moba",
    )(p3, p3, p3, p3, bias_tiles, rel_bias)


def _ret_tables(t):
    half = HDIM // 2
    theta = 1.0 / (10000.0 ** jnp.linspace(0.0, 1.0, half))
    pos = jnp.arange(t, dtype=F32)
    ang = pos[:, None] * theta[None, :]
    sin, cos = jnp.sin(ang), jnp.cos(ang)
    cos2 = jnp.tile(jnp.repeat(cos, 2, axis=1), (1, HEADS))
    sin2 = jnp.tile(jnp.stack([-sin, sin], axis=-1).reshape(t, HDIM), (1, HEADS))
    log_gamma = jnp.log(1.0 - 2.0 ** (-5.0 - jnp.arange(HEADS, dtype=F32)))
    idx = jnp.arange(RET_CHUNK, dtype=F32)
    diff = idx[:, None] - idx[None, :]
    decay_intra = jnp.where(diff >= 0, jnp.exp(log_gamma[:, None, None] * jnp.maximum(diff, 0.0)), 0.0)
    q_decay = jnp.exp(log_gamma[:, None] * (idx + 1.0))
    k_decay = jnp.exp(log_gamma[:, None] * (RET_CHUNK - 1.0 - idx))
    chunk_decay = jnp.exp(log_gamma * RET_CHUNK)
    lanes = lambda a: jnp.repeat(a.T, HDIM, axis=1)
    cd = jnp.repeat(chunk_decay, HDIM).reshape(1, GROUP)
    return cos2, sin2, decay_intra, lanes(q_decay), lanes(k_decay), cd


def _ret_kernel(q_ref, k_ref, v_ref, gate_ref, cos_ref, sin_ref, di_ref, qd_ref, kd_ref, cd_ref,
                o_ref, state_ref, *, tt):
    C = RET_CHUNK
    n_chunks = tt // C
    chunks = range(n_chunks)
    heads = range(HEADS)
    b = pl.program_id(1)

    @pl.when(pl.program_id(0) == 0)
    def _():
        state_ref[b] = jnp.zeros((GROUP, GROUP), F32)

    even = (_iota((tt, GROUP), 1) % 2) == 0

    def rotate(x):
        swapped = jnp.where(even, pltpu.roll(x, GROUP - 1, axis=1), pltpu.roll(x, 1, axis=1))
        return x * cos_ref[...] + swapped * sin_ref[...]

    q = rotate(q_ref[0])
    k = rotate(k_ref[0]) * (HDIM ** -0.5)
    lane_head = _iota((1, GROUP), 1) // HDIM
    same_head = (_iota((GROUP, GROUP), 0) // HDIM) == (_iota((GROUP, GROUP), 1) // HDIM)
    rows = lambda c: slice(c * C, (c + 1) * C)
    mask_pair = lambda pr, h: (jnp.where(lane_head == h, pr[0], jnp.zeros_like(pr[0])),
                               jnp.where(lane_head == h, pr[1], jnp.zeros_like(pr[1])))

    q_s = [_split(q[rows(c)]) for c in chunks]
    k_s = [_split(k[rows(c)]) for c in chunks]
    v_s = [_split(v_ref[0, rows(c), :]) for c in chunks]
    kd_s = [_split(k[rows(c)] * kd_ref[...]) for c in chunks]
    qd_s = [_split(q[rows(c)] * qd_ref[...]) for c in chunks]

    inner_s = [_split(_dot3(_nt, _pcat([mask_pair(q_s[c], h) for h in heads], 0), k_s[c]) * di_ref[...])
               for c in chunks]
    kv = [jnp.where(same_head, _dot3(_tn, kd_s[c], v_s[c]), 0.0) for c in chunks]
    intra = []
    for c in chunks:
        stacked = _dot3(_nn, inner_s[c], v_s[c])
        acc = jnp.where(lane_head == 0, stacked[0:C], 0.0)
        for h in range(1, HEADS):
            acc = jnp.where(lane_head == h, stacked[h * C:(h + 1) * C], acc)
        intra.append(acc)
    state = state_ref[b]
    states = []
    for c in chunks:
        states.append(_split(state))
        state = state * cd_ref[...] + kv[c]
    state_ref[b] = state
    y = jnp.concatenate([intra[c] + _dot3(_nn, qd_s[c], states[c]) for c in chunks], axis=0)
    mean_sq = _head_sum(y * y, _head_sum_matrix().astype(BF16)) * (1.0 / HDIM)
    o_ref[0] = y * lax.rsqrt(mean_sq + NORM_EPS) * _silu(gate_ref[0])


def _ret(p3, tables, tt=512):
    b, t, _ = p3.shape
    base = 3 * 4
    cos2, sin2, di, qd, kd, cd = tables
    di = di.reshape(HEADS * RET_CHUNK, RET_CHUNK)
    blk = lambda c: pl.BlockSpec((1, tt, GROUP), lambda j, i, c=c: (i, j, c))
    full2 = lambda a: pl.BlockSpec(a.shape, lambda j, i: (0, 0))
    return pl.pallas_call(
        functools.partial(_ret_kernel, tt=tt),
        out_shape=jax.ShapeDtypeStruct((b, t, GROUP), F32),
        grid=(t // tt, b),
        in_specs=[blk(base), blk(base + 1), blk(base + 2), blk(base + 3),
                  pl.BlockSpec((tt, GROUP), lambda j, i: (j, 0)),
                  pl.BlockSpec((tt, GROUP), lambda j, i: (j, 0)),
                  full2(di), full2(qd), full2(kd), full2(cd)],
        out_specs=pl.BlockSpec((1, tt, GROUP), lambda j, i: (i, j, 0)),
        scratch_shapes=[pltpu.VMEM((b, GROUP, GROUP), F32)],
        compiler_params=pltpu.CompilerParams(
            dimension_semantics=("arbitrary", "arbitrary"), vmem_limit_bytes=VMEM_LIMIT_V7X),
        name="retnet",
    )(p3, p3, p3, p3, cos2, sin2, di, qd, kd, cd)


def _reorder_w_in(w):
    g3 = 3 * GROUP
    lora = w[:, g3:g3 + 2 * LORA]
    pad = jnp.zeros((w.shape[0], 128 - 2 * LORA), w.dtype)
    return jnp.concatenate([w[:, :g3], w[:, g3 + 2 * LORA:], lora, pad], axis=1)


def kernel(x, norm_w, w_in, w_out, rwkv_mu, rwkv_w0, rwkv_w2, rwkv_a0, rwkv_a2, rwkv_k_k, rwkv_k_a,
           rwkv_r_k, rwkv_lnx_w, rwkv_lnx_b, conv_w, rel_bias, final_norm_w):
    b, t, d = x.shape
    depth = w_in.shape[0]
    bias_tiles = _moba_bias_tiles(rel_bias)
    ret_tables = _ret_tables(t)
    x2 = x.reshape(b * t, d)
    for l in range(depth):
        w_l = _reorder_w_in(w_in[l]).astype(BF16)
        p3 = _inproj(x2, norm_w[l], w_l).reshape(b, t, P_COLS)
        y_rwkv = _rwkv(p3, rwkv_mu[l], rwkv_w0[l], rwkv_w2[l], rwkv_a0[l], rwkv_a2[l], rwkv_k_k[l],
                       rwkv_k_a[l], rwkv_r_k[l].reshape(GROUP), rwkv_lnx_w[l], rwkv_lnx_b[l])
        y_conv = _conv(p3, conv_w[l])
        y_moba = _moba(p3, bias_tiles, rel_bias)
        y_ret = _ret(p3, ret_tables)
        ys = [y.reshape(b * t, GROUP) for y in (y_rwkv, y_conv, y_moba, y_ret)]
        x2 = _outproj(x2, ys, w_out[l].astype(BF16), final_norm_w, final=(l == depth - 1))
    return x2.reshape(b, t, d)
```

```python
import functools
import math

import numpy as np
import jax
import jax.numpy as jnp
from jax import lax
from jax.experimental import pallas as pl
from jax.experimental.pallas import tpu as pltpu

F32 = jnp.float32
BF16 = jnp.bfloat16
HIGHEST = lax.Precision.HIGHEST

GROUP = 256
HEADS = 4
HDIM = 64
LORA = 32
DECAY_SCALE = math.exp(-0.5)
LNX_EPS = 64e-5
NORM_EPS = 1e-6
CONV_TAPS = 3
MOBA_BLOCK = 256
MOBA_TOPK = 3
MOBA_QBLOCK = 128
RET_CHUNK = 128
REL_BUCKETS = 32
REL_MAX_DIST = 128
RWKV_CHUNK = 64

P_COLS = 4 * 4 * GROUP + 128
LORA_COL = 4 * 4 * GROUP

VMEM_LIMIT_V7X = 48 * 1024 * 1024


def _nn(a, b, precision=None):
    return lax.dot_general(a, b, (((1,), (0,)), ((), ())), precision=precision,
                           preferred_element_type=F32)


def _nt(a, b, precision=None):
    return lax.dot_general(a, b, (((1,), (1,)), ((), ())), precision=precision,
                           preferred_element_type=F32)


def _tn(a, b, precision=None):
    return lax.dot_general(a, b, (((0,), (0,)), ((), ())), precision=precision,
                           preferred_element_type=F32)


def _split(x):
    hi = x.astype(BF16)
    lo = (x - hi.astype(F32)).astype(BF16)
    return hi, lo


def _dot3(dot, a, b):
    out_axis = 1 if dot is _tn else 0
    m = a[0].shape[out_axis]
    both = dot(jnp.concatenate([a[0], a[1]], axis=out_axis), b[0])
    return both[:m] + both[m:] + dot(a[0], b[1])


def _psl(pair, rows, cols):
    return pair[0][rows, cols], pair[1][rows, cols]


def _pcat(pairs, axis):
    return (jnp.concatenate([p[0] for p in pairs], axis=axis),
            jnp.concatenate([p[1] for p in pairs], axis=axis))


def _head_sum(x, esum_bf16):
    hi = x.astype(BF16)
    r1 = x - hi.astype(F32)
    mid = r1.astype(BF16)
    lo = (r1 - mid.astype(F32)).astype(BF16)
    return _nn(hi, esum_bf16) + (_nn(mid, esum_bf16) + _nn(lo, esum_bf16))


def _iota(shape, dim):
    return lax.broadcasted_iota(jnp.int32, shape, dim)


def _head_sum_matrix():
    r = _iota((GROUP, GROUP), 0) // HDIM
    c = _iota((GROUP, GROUP), 1) // HDIM
    return jnp.where(r == c, 1.0, 0.0).astype(F32)


def _silu(x):
    return x * jax.nn.sigmoid(x)


def _inproj_kernel(x_ref, nw_ref, w_ref, o_ref):
    x = x_ref[...]
    h = x * lax.rsqrt(jnp.mean(x * x, axis=-1, keepdims=True) + NORM_EPS) * nw_ref[...]
    o_ref[...] = _nn(h.astype(BF16), w_ref[0])


def _inproj(x2d, norm_w, w_bf16, layer, tm=256):
    m, d = x2d.shape
    n = w_bf16.shape[2]
    return pl.pallas_call(
        _inproj_kernel,
        out_shape=jax.ShapeDtypeStruct((m, n), F32),
        grid=(m // tm,),
        in_specs=[pl.BlockSpec((tm, d), lambda i: (i, 0)),
                  pl.BlockSpec((1, d), lambda i: (0, 0)),
                  pl.BlockSpec((1, d, n), lambda i: (layer, 0, 0))],
        out_specs=pl.BlockSpec((tm, n), lambda i: (i, 0)),
        compiler_params=pltpu.CompilerParams(
            dimension_semantics=("arbitrary",), vmem_limit_bytes=VMEM_LIMIT_V7X),
        name="inproj",
    )(x2d, norm_w.reshape(1, d), w_bf16)


def _gated_conv(p, cw_ref, tail_ref, first_tile):
    @pl.when(first_tile)
    def _():
        tail_ref[...] = jnp.zeros_like(tail_ref)

    n_rows = p.shape[0]
    u = p[:, GROUP:2 * GROUP] * p[:, 2 * GROUP:3 * GROUP]
    rows = _iota((n_rows, 1), 0)
    u1 = jnp.where(rows == 0, tail_ref[1:2, :], pltpu.roll(u, 1, axis=0))
    u2 = jnp.where(rows == 0, tail_ref[0:1, :],
                   jnp.where(rows == 1, tail_ref[1:2, :], pltpu.roll(u, 2, axis=0)))
    tail_ref[0:2, :] = u[n_rows - 2:n_rows, :]
    y = u2 * cw_ref[0:1, :] + u1 * cw_ref[1:2, :] + u * cw_ref[2:3, :]
    return p[:, 0:GROUP] * y * _silu(p[:, 3 * GROUP:4 * GROUP])


def _outproj_kernel(x_ref, y0_ref, pconv_ref, cw_ref, y2_ref, y3_ref, w_ref, fw_ref, o_ref, tail_ref,
                    *, final, tiles_per_seq):
    y_conv = _gated_conv(pconv_ref[...], cw_ref, tail_ref, pl.program_id(0) % tiles_per_seq == 0)
    acc = x_ref[...]
    for g, y in enumerate((y0_ref[...], y_conv, y2_ref[...], y3_ref[...])):
        acc = acc + _nn(y.astype(BF16), w_ref[0, g * GROUP:(g + 1) * GROUP, :])
    if final:
        acc = acc * lax.rsqrt(jnp.mean(acc * acc, axis=-1, keepdims=True) + NORM_EPS) * fw_ref[...]
    o_ref[...] = acc


def _outproj(x2d, p2d, y_rwkv, y_moba, y_ret, conv_w, w_bf16, layer, final_w, final, seq_len, tm=512):
    m, d = x2d.shape
    assert seq_len % tm == 0
    yspec = pl.BlockSpec((tm, GROUP), lambda i: (i, 0))
    return pl.pallas_call(
        functools.partial(_outproj_kernel, final=final, tiles_per_seq=seq_len // tm),
        out_shape=jax.ShapeDtypeStruct((m, d), F32),
        grid=(m // tm,),
        in_specs=[pl.BlockSpec((tm, d), lambda i: (i, 0)), yspec,
                  pl.BlockSpec((tm, 4 * GROUP), lambda i: (i, 1)),
                  pl.BlockSpec((CONV_TAPS, GROUP), lambda i: (0, 0)),
                  yspec, yspec,
                  pl.BlockSpec((1, 4 * GROUP, d), lambda i: (layer, 0, 0)),
                  pl.BlockSpec((1, d), lambda i: (0, 0))],
        out_specs=pl.BlockSpec((tm, d), lambda i: (i, 0)),
        scratch_shapes=[pltpu.VMEM((8, GROUP), F32)],
        compiler_params=pltpu.CompilerParams(
            dimension_semantics=("arbitrary",), vmem_limit_bytes=VMEM_LIMIT_V7X),
        name="outproj_final" if final else "outproj",
    )(x2d, y_rwkv, p2d, conv_w, y_moba, y_ret, w_bf16, final_w.reshape(1, d))


def _rwkv_kernel(rkv_ref, gate_ref, lora_ref, mu_ref, mul_ref, w0_ref, w2_ref, a0_ref, a2_ref,
                 kk_ref, ka_ref, rk_ref, lnw_ref, lnb_ref, o_ref,
                 state_ref, prev_ref, prevl_ref, *, tt):
    C = RWKV_CHUNK
    t_idx = pl.program_id(1)

    @pl.when(t_idx == 0)
    def _():
        state_ref[...] = jnp.zeros_like(state_ref)
        prev_ref[...] = jnp.zeros_like(prev_ref)
        prevl_ref[...] = jnp.zeros_like(prevl_ref)

    p = rkv_ref[0]
    lo = lora_ref[0]
    row0 = _iota((tt, 1), 0) == 0
    p_sh = jnp.where(row0, prev_ref[...], pltpu.roll(p, 1, axis=0))
    lo_sh = jnp.where(row0, prevl_ref[...], pltpu.roll(lo, 1, axis=0))
    prev_ref[...] = p[tt - 1:tt, :]
    prevl_ref[...] = lo[tt - 1:tt, :]
    p = p + (p_sh - p) * mu_ref[...]
    lo = lo + (lo_sh - lo) * mul_ref[...]

    r = p[:, 0:GROUP]
    k = p[:, GROUP:2 * GROUP]
    v = p[:, 2 * GROUP:3 * GROUP]

    esum = _head_sum_matrix().astype(BF16)
    lw = -DECAY_SCALE * jax.nn.sigmoid(
        w0_ref[...] + _dot3(_nn, _split(jnp.tanh(lo)), _split(w2_ref[...])))
    rate = jax.nn.sigmoid(a0_ref[...] + _dot3(_nn, _split(lo), _split(a2_ref[...])))
    kk = k * kk_ref[...]
    kk = kk / jnp.maximum(jnp.sqrt(_head_sum(kk * kk, esum)), 1e-12)
    k2 = k * (1.0 + (rate - 1.0) * ka_ref[...])
    bonus = _head_sum(r * k2 * rk_ref[...], esum) * v
    bvec = kk * rate
    avec = -kk

    rowmod = _iota((tt, 1), 0) % C
    cum = lw
    for sh in (1, 2, 4, 8, 16, 32):
        cum = cum + jnp.where(rowmod >= sh, pltpu.roll(cum, sh, axis=0), 0.0)
    tot_rows = [cum[c * C + C - 1:c * C + C, :] for c in range(tt // C)]
    tot = jnp.concatenate([jnp.broadcast_to(tr, (C, GROUP)) for tr in tot_rows], axis=0)
    e_neg = jnp.exp(-cum)
    e_end = jnp.exp(tot - cum)
    r_t = _split(r * jnp.exp(cum))
    a_t = _split(avec * jnp.exp(cum - lw))
    k_t = _split(k2 * e_neg)
    b_t = _split(bvec * e_neg)
    k_h = _split(k2 * e_end)
    b_h = _split(bvec * e_end)
    v_s = _split(v)

    same_head = (_iota((GROUP, GROUP), 0) // HDIM) == (_iota((GROUP, GROUP), 1) // HDIM)

    def bd(pr):
        return tuple(jnp.where(same_head, jnp.concatenate([part] * HEADS, axis=0), jnp.zeros((), part.dtype))
                     for part in pr)

    t_i = _iota((2 * C, GROUP), 0)
    s_i = _iota((2 * C, GROUP), 1) % C
    tri = ((t_i < C) & (s_i < t_i)) | ((t_i >= C) & (s_i <= t_i - C))
    eye = jnp.where(_iota((C, GROUP), 0) == _iota((C, GROUP), 1) % C, 1.0, 0.0).astype(F32)

    chunks = range(tt // C)
    every = slice(None)
    rows = lambda pr, c: _psl(pr, slice(c * C, (c + 1) * C), every)
    top = lambda pr: _psl(pr, slice(0, C), every)
    bottom = lambda pr: _psl(pr, slice(C, 2 * C), every)

    ar = [_pcat([rows(a_t, c), rows(r_t, c)], 0) for c in chunks]
    bd_v = [bd(rows(v_s, c)) for c in chunks]
    ab_rb = [jnp.where(tri, _dot3(_nt, ar[c], bd(rows(b_t, c))), 0.0) for c in chunks]
    ak_rk = [_split(jnp.where(tri, _dot3(_nt, ar[c], bd(rows(k_t, c))), 0.0)) for c in chunks]
    a_ab = [ab_rb[c][0:C] for c in chunks]
    ab_rb = [_split(x) for x in ab_rb]
    t_row = _iota((C, GROUP), 0)
    s_col = _iota((C, GROUP), 1) % C

    def below_diagonal(n):
        return ((t_row // (2 * n)) == (s_col // (2 * n))) & ((t_row % (2 * n)) >= n) & ((s_col % (2 * n)) < n)

    inv = [eye + jnp.where(below_diagonal(1), a_ab[c], 0.0) for c in chunks]
    n = 2
    while n < C:
        off = below_diagonal(n)
        inv_s = [_split(inv[c]) for c in chunks]
        left = [_split(_dot3(_nn, inv_s[c], bd(_split(jnp.where(off, a_ab[c], 0.0))))) for c in chunks]
        inv = [inv[c] + _dot3(_nn, left[c], bd(inv_s[c])) for c in chunks]
        n *= 2
    inv = [_split(x) for x in inv]
    t_ak = [_split(_dot3(_nn, inv[c], bd(top(ak_rk[c])))) for c in chunks]
    t_a = [_split(_dot3(_nn, inv[c], bd(rows(a_t, c)))) for c in chunks]
    free = [_dot3(_nn, _pcat([t_ak[c], bottom(ak_rk[c])], 0), bd_v[c]) for c in chunks]
    kb_hat = [_pcat([rows(k_h, c), rows(b_h, c)], 0) for c in chunks]
    m_s = [_split(jnp.where(same_head, _dot3(_tn, t_a[c], rows(b_h, c)), 0.0)) for c in chunks]
    g = []
    for c in chunks:
        full = jnp.where(same_head, _dot3(_tn, _pcat([rows(v_s, c), _split(free[c][0:C])], 0), kb_hat[c]), 0.0)
        g.append((full[0:HDIM] + full[HDIM:2 * HDIM]) + (full[2 * HDIM:3 * HDIM] + full[3 * HDIM:4 * HDIM]))
    state = state_ref[...]
    states = []
    for c in chunks:
        st_s = _split(state)
        states.append(bd(st_s))
        state = state * jnp.exp(tot_rows[c]) + _dot3(_nn, st_s, m_s[c]) + g[c]
    state_ref[...] = state
    from_state = [_dot3(_nt, _pcat([t_a[c], rows(r_t, c)], 0), states[c]) for c in chunks]
    y = jnp.concatenate(
        [free[c][C:2 * C] + from_state[c][C:2 * C]
         + _dot3(_nn, bottom(ab_rb[c]), bd(_split(free[c][0:C] + from_state[c][0:C]))) for c in chunks], axis=0)
    mean = _head_sum(y, esum) * (1.0 / HDIM)
    yc = y - mean
    var = _head_sum(yc * yc, esum) * (1.0 / HDIM)
    y = yc * lax.rsqrt(var + LNX_EPS) * lnw_ref[...] + lnb_ref[...] + bonus
    o_ref[0] = y * _silu(gate_ref[0])


def _rwkv(p3, mu, w0, w2, a0, a2, k_k, k_a, r_k, lnx_w, lnx_b, tt=256):
    b, t, _ = p3.shape
    mu_main = mu[:3 * GROUP].reshape(1, 3 * GROUP)
    mu_lora = jnp.concatenate([mu[3 * GROUP:], jnp.zeros((128 - 2 * LORA,), F32)]).reshape(1, 128)
    w2p = jnp.zeros((128, GROUP), F32).at[0:LORA].set(w2)
    a2p = jnp.zeros((128, GROUP), F32).at[LORA:2 * LORA].set(a2)
    row = lambda a: a.reshape(1, GROUP)
    vec = pl.BlockSpec((1, GROUP), lambda i, j: (0, 0))
    return pl.pallas_call(
        functools.partial(_rwkv_kernel, tt=tt),
        out_shape=jax.ShapeDtypeStruct((b, t, GROUP), F32),
        grid=(b, t // tt),
        in_specs=[pl.BlockSpec((1, tt, 3 * GROUP), lambda i, j: (i, j, 0)),
                  pl.BlockSpec((1, tt, GROUP), lambda i, j: (i, j, 3)),
                  pl.BlockSpec((1, tt, 128), lambda i, j: (i, j, LORA_COL // 128)),
                  pl.BlockSpec((1, 3 * GROUP), lambda i, j: (0, 0)),
                  pl.BlockSpec((1, 128), lambda i, j: (0, 0)),
                  vec,
                  pl.BlockSpec((128, GROUP), lambda i, j: (0, 0)),
                  vec,
                  pl.BlockSpec((128, GROUP), lambda i, j: (0, 0)),
                  vec, vec, vec, vec, vec],
        out_specs=pl.BlockSpec((1, tt, GROUP), lambda i, j: (i, j, 0)),
        scratch_shapes=[pltpu.VMEM((HDIM, GROUP), F32),
                        pltpu.VMEM((1, 3 * GROUP), F32),
                        pltpu.VMEM((1, 128), F32)],
        compiler_params=pltpu.CompilerParams(
            dimension_semantics=("arbitrary", "arbitrary"), vmem_limit_bytes=VMEM_LIMIT_V7X),
        name="rwkv7",
    )(p3, p3, p3, mu_main, mu_lora, row(w0), w2p, row(a0), a2p,
      row(k_k), row(k_a), row(r_k), row(lnx_w), row(lnx_b))


def _t5_bucket_np(dist):
    max_exact = REL_BUCKETS // 2
    d_f = np.maximum(dist, 1).astype(np.float32)
    large = max_exact + (np.log(d_f / np.float32(max_exact)) / np.float32(math.log(REL_MAX_DIST / max_exact))
                         * np.float32(REL_BUCKETS - max_exact)).astype(np.int32)
    large = np.minimum(large, REL_BUCKETS - 1)
    return np.where(dist < max_exact, dist, large).astype(np.int32)


MASKED_BUCKET = REL_BUCKETS


def _moba_bucket_table():
    keys = np.arange(MOBA_BLOCK)[:, None]
    queries = np.arange(MOBA_BLOCK)[None, :]
    prev = _t5_bucket_np(queries + MOBA_BLOCK - keys)
    own = np.where(keys <= queries, _t5_bucket_np(np.maximum(queries - keys, 0)), MASKED_BUCKET)
    return np.stack([prev, own]).astype(np.int32)


def _bias_kernel(idx_ref, rb_ref, o_ref):
    h = pl.program_id(0)
    idx = idx_ref[...]
    acc = jnp.full(idx.shape, -jnp.inf, F32)
    for bkt in range(REL_BUCKETS):
        acc = jnp.where(idx == bkt, rb_ref[bkt, h], acc)
    o_ref[0] = acc


def _moba_bias_tiles(rel_bias):
    idx = jnp.asarray(_moba_bucket_table())
    shp = (2, MOBA_BLOCK, MOBA_BLOCK)
    return pl.pallas_call(
        _bias_kernel,
        out_shape=jax.ShapeDtypeStruct((HEADS,) + shp, F32),
        grid=(HEADS,),
        in_specs=[pl.BlockSpec(shp, lambda h: (0, 0, 0)),
                  pl.BlockSpec(memory_space=pltpu.SMEM)],
        out_specs=pl.BlockSpec((1,) + shp, lambda h: (h, 0, 0, 0)),
        name="moba_bias",
    )(idx, rel_bias)


def _moba_kernel(q_ref, k_ref, v_ref, gate_ref, bias_ref, rb_ref, o_ref,
                 kmean_ref, kbf_ref, vt_ref, sel_ref, m_ref, l_ref, acc_ref, *, nb):
    BLK = MOBA_BLOCK
    ib = pl.program_id(1)
    heads = range(HEADS)
    rows_of = lambda h: slice(h * HDIM, (h + 1) * HDIM)
    neg_inf = -jnp.inf
    not_selected = -1e30

    @pl.when(ib == 0)
    def _():
        kmean_ref[...] = jnp.zeros_like(kmean_ref)
        for n in range(nb):
            blk = slice(n * BLK, (n + 1) * BLK)
            kblk = k_ref[0, blk, :]
            kmean_ref[n:n + 1, :] = jnp.mean(kblk, axis=0, keepdims=True)
            kbf_ref[n] = kblk.astype(BF16)
            vt_ref[n, 0:GROUP, :] = v_ref[0, blk, :].T.astype(BF16)
            vt_ref[n, GROUP:GROUP + 16, :] = jnp.ones((16, BLK), BF16)

    q = q_ref[0] * (HDIM ** -0.5)
    q_bf = q.astype(BF16)
    q_s = _split(q)
    lane_head = _iota((1, GROUP), 1) // HDIM
    q_heads = [jnp.where(lane_head == h, q_bf, jnp.zeros_like(q_bf)) for h in heads]

    km = kmean_ref[...]
    gates = [_dot3(_nt, _split(jnp.where(lane_head == h, km, 0.0)), q_s) for h in heads]
    blk_id = _iota((16, BLK), 0)
    for h in heads:
        g = jnp.where(blk_id < ib, gates[h], neg_inf)
        sel = jnp.zeros((16, BLK), F32)
        for _ in range(min(MOBA_TOPK, nb)):
            m = jnp.max(g, axis=0, keepdims=True)
            hit = (g == m) & (m > neg_inf)
            first = jnp.min(jnp.where(hit, blk_id, 16), axis=0, keepdims=True)
            pick = blk_id == first
            sel = jnp.where(pick, 1.0, sel)
            g = jnp.where(pick, neg_inf, g)
        sel_ref[h] = sel

    def attend(blocks, first):
        scores = [[_nt(kbf_ref[n], q_heads[h]) for h in heads] for n, _ in blocks]
        probs, alphas = [], []
        for h in heads:
            ss = [scores[j][h] + extra[h] for j, (_, extra) in enumerate(blocks)]
            m_new = functools.reduce(jnp.maximum, [jnp.max(s, axis=0, keepdims=True) for s in ss])
            if not first:
                m_old = m_ref[h]
                m_new = jnp.maximum(m_old, m_new)
                alphas.append(jnp.exp(m_old - m_new))
            m_ref[h] = m_new
            probs.append([jnp.exp(s - m_new).astype(BF16) for s in ss])
        for h in heads:
            pv = None
            for j, (n, _) in enumerate(blocks):
                lhs = jnp.concatenate([vt_ref[n, rows_of(h), :], vt_ref[n, GROUP:GROUP + 16, :]], axis=0)
                part = _nn(lhs, probs[h][j])
                pv = part if pv is None else pv + part
            if first:
                acc_ref[rows_of(h), :] = pv[0:HDIM]
                l_ref[h] = pv[HDIM:HDIM + 1]
            else:
                acc_ref[rows_of(h), :] = alphas[h] * acc_ref[rows_of(h), :] + pv[0:HDIM]
                l_ref[h] = alphas[h] * l_ref[h] + pv[HDIM:HDIM + 1]

    def selected_row(h, n, valid):
        row = jnp.where(sel_ref[h, pl.ds(n, 1), :] > 0.5, 0.0, not_selected)
        return jnp.where(valid, row, not_selected)

    prev = jnp.maximum(ib - 1, 0)
    attend([(ib, [bias_ref[h, 1] for h in heads]),
            (prev, [bias_ref[h, 0] + selected_row(h, prev, ib >= 1) for h in heads])], first=True)

    n_far = ib - 1
    for n in range(0, nb - 2, 2):
        @pl.when(n < n_far)
        def _(n=n):
            pair = [n] if n + 1 >= nb - 2 else [n, n + 1]
            attend([(j, [selected_row(h, j, j < n_far) + rb_ref[REL_BUCKETS - 1, h] for h in heads])
                    for j in pair], first=False)

    for h in heads:
        acc_ref[rows_of(h), :] = acc_ref[rows_of(h), :] / l_ref[h]
    o_ref[0] = acc_ref[...].T * _silu(gate_ref[0])


def _moba(p3, bias_tiles, rel_bias):
    b, t, _ = p3.shape
    BLK = MOBA_BLOCK
    nb = t // BLK
    assert 2 <= nb <= 16 and t % BLK == 0
    base = 2 * 4
    tile = lambda c: pl.BlockSpec((1, BLK, GROUP), lambda i, j, c=c: (i, j, c))
    seq = lambda c: pl.BlockSpec((1, t, GROUP), lambda i, j, c=c: (i, 0, c))
    return pl.pallas_call(
        functools.partial(_moba_kernel, nb=nb),
        out_shape=jax.ShapeDtypeStruct((b, t, GROUP), F32),
        grid=(b, nb),
        in_specs=[tile(base), seq(base + 1), seq(base + 2), tile(base + 3),
                  pl.BlockSpec(bias_tiles.shape, lambda i, j: (0, 0, 0, 0)),
                  pl.BlockSpec(memory_space=pltpu.SMEM)],
        out_specs=pl.BlockSpec((1, BLK, GROUP), lambda i, j: (i, j, 0)),
        scratch_shapes=[pltpu.VMEM((16, GROUP), F32),
                        pltpu.VMEM((nb, BLK, GROUP), BF16),
                        pltpu.VMEM((nb, GROUP + 16, BLK), BF16),
                        pltpu.VMEM((HEADS, 16, BLK), F32),
                        pltpu.VMEM((HEADS, 1, BLK), F32),
                        pltpu.VMEM((HEADS, 1, BLK), F32),
                        pltpu.VMEM((GROUP, BLK), F32)],
        compiler_params=pltpu.CompilerParams(
            dimension_semantics=("arbitrary", "arbitrary"), vmem_limit_bytes=VMEM_LIMIT_V7X),
        name="moba",
    )(p3, p3, p3, p3, bias_tiles, rel_bias)


def _ret_tables(t):
    half = HDIM // 2
    theta = 1.0 / (10000.0 ** np.linspace(0.0, 1.0, half))
    pos = np.arange(t, dtype=np.float64)
    ang = pos[:, None] * theta[None, :]
    sin, cos = np.sin(ang), np.cos(ang)
    cos2 = np.tile(np.repeat(cos, 2, axis=1), (1, HEADS))
    sin2 = np.tile(np.stack([-sin, sin], axis=-1).reshape(t, HDIM), (1, HEADS))
    log_gamma = np.log(1.0 - 2.0 ** (-5.0 - np.arange(HEADS, dtype=np.float64)))
    idx = np.arange(RET_CHUNK, dtype=np.float64)
    diff = idx[:, None] - idx[None, :]
    decay_intra = np.where(diff >= 0, np.exp(log_gamma[:, None, None] * np.maximum(diff, 0.0)), 0.0)
    q_decay = np.exp(log_gamma[:, None] * (idx + 1.0))
    k_decay = np.exp(log_gamma[:, None] * (RET_CHUNK - 1.0 - idx))
    chunk_decay = np.exp(log_gamma * RET_CHUNK)
    lanes = lambda a: np.repeat(a.T, HDIM, axis=1)
    cd = np.repeat(chunk_decay, HDIM).reshape(1, GROUP)
    return tuple(jnp.asarray(a, F32) for a in (cos2, sin2, decay_intra, lanes(q_decay), lanes(k_decay), cd))


def _ret_kernel(q_ref, k_ref, v_ref, gate_ref, cos_ref, sin_ref, di_ref, qd_ref, kd_ref, cd_ref,
                o_ref, state_ref, *, tt):
    C = RET_CHUNK
    n_chunks = tt // C
    chunks = range(n_chunks)
    heads = range(HEADS)
    b = pl.program_id(1)

    @pl.when(pl.program_id(0) == 0)
    def _():
        state_ref[b] = jnp.zeros((GROUP, GROUP), F32)

    even = (_iota((tt, GROUP), 1) % 2) == 0

    def rotate(x):
        swapped = jnp.where(even, pltpu.roll(x, GROUP - 1, axis=1), pltpu.roll(x, 1, axis=1))
        return x * cos_ref[...] + swapped * sin_ref[...]

    q = rotate(q_ref[0])
    k = rotate(k_ref[0]) * (HDIM ** -0.5)
    lane_head = _iota((1, GROUP), 1) // HDIM
    same_head = (_iota((GROUP, GROUP), 0) // HDIM) == (_iota((GROUP, GROUP), 1) // HDIM)
    rows = lambda c: slice(c * C, (c + 1) * C)

    q_b = [q[rows(c)].astype(BF16) for c in chunks]
    k_b = [k[rows(c)].astype(BF16) for c in chunks]
    v_b = [v_ref[0, rows(c), :].astype(BF16) for c in chunks]
    kd_b = [(k[rows(c)] * kd_ref[...]).astype(BF16) for c in chunks]
    qd_b = [(q[rows(c)] * qd_ref[...]).astype(BF16) for c in chunks]

    inner = [(_nt(jnp.concatenate([jnp.where(lane_head == h, q_b[c], jnp.zeros_like(q_b[c])) for h in heads],
                                  axis=0), k_b[c]) * di_ref[...]).astype(BF16) for c in chunks]
    kv = [jnp.where(same_head, _tn(kd_b[c], v_b[c]), 0.0) for c in chunks]
    intra = []
    for c in chunks:
        stacked = _nn(inner[c], v_b[c])
        acc = jnp.where(lane_head == 0, stacked[0:C], 0.0)
        for h in range(1, HEADS):
            acc = jnp.where(lane_head == h, stacked[h * C:(h + 1) * C], acc)
        intra.append(acc)
    state = state_ref[b]
    states = []
    for c in chunks:
        states.append(state.astype(BF16))
        state = state * cd_ref[...] + kv[c]
    state_ref[b] = state
    y = jnp.concatenate([intra[c] + _nn(qd_b[c], states[c]) for c in chunks], axis=0)
    mean_sq = _head_sum(y * y, _head_sum_matrix().astype(BF16)) * (1.0 / HDIM)
    o_ref[0] = y * lax.rsqrt(mean_sq + NORM_EPS) * _silu(gate_ref[0])


def _ret(p3, tables, tt=512):
    b, t, _ = p3.shape
    base = 3 * 4
    cos2, sin2, di, qd, kd, cd = tables
    di = di.reshape(HEADS * RET_CHUNK, RET_CHUNK)
    blk = lambda c: pl.BlockSpec((1, tt, GROUP), lambda j, i, c=c: (i, j, c))
    full2 = lambda a: pl.BlockSpec(a.shape, lambda j, i: (0, 0))
    return pl.pallas_call(
        functools.partial(_ret_kernel, tt=tt),
        out_shape=jax.ShapeDtypeStruct((b, t, GROUP), F32),
        grid=(t // tt, b),
        in_specs=[blk(base), blk(base + 1), blk(base + 2), blk(base + 3),
                  pl.BlockSpec((tt, GROUP), lambda j, i: (j, 0)),
                  pl.BlockSpec((tt, GROUP), lambda j, i: (j, 0)),
                  full2(di), full2(qd), full2(kd), full2(cd)],
        out_specs=pl.BlockSpec((1, tt, GROUP), lambda j, i: (i, j, 0)),
        scratch_shapes=[pltpu.VMEM((b, GROUP, GROUP), F32)],
        compiler_params=pltpu.CompilerParams(
            dimension_semantics=("arbitrary", "arbitrary"), vmem_limit_bytes=VMEM_LIMIT_V7X),
        name="retnet",
    )(p3, p3, p3, p3, cos2, sin2, di, qd, kd, cd)


def _reorder_w_in(w):
    g3 = 3 * GROUP
    lora = w[..., g3:g3 + 2 * LORA]
    pad = jnp.zeros(w.shape[:-1] + (128 - 2 * LORA,), w.dtype)
    return jnp.concatenate([w[..., :g3], w[..., g3 + 2 * LORA:], lora, pad], axis=-1)


def kernel(x, norm_w, w_in, w_out, rwkv_mu, rwkv_w0, rwkv_w2, rwkv_a0, rwkv_a2, rwkv_k_k, rwkv_k_a,
           rwkv_r_k, rwkv_lnx_w, rwkv_lnx_b, conv_w, rel_bias, final_norm_w):
    b, t, d = x.shape
    depth = w_in.shape[0]
    bias_tiles = _moba_bias_tiles(rel_bias)
    ret_tables = _ret_tables(t)
    x2 = x.reshape(b * t, d)
    w_in_bf = _reorder_w_in(w_in.astype(BF16))
    w_out_bf = w_out.astype(BF16)
    for l in range(depth):
        p2 = _inproj(x2, norm_w[l], w_in_bf, l)
        p3 = p2.reshape(b, t, P_COLS)
        y_rwkv = _rwkv(p3, rwkv_mu[l], rwkv_w0[l], rwkv_w2[l], rwkv_a0[l], rwkv_a2[l], rwkv_k_k[l],
                       rwkv_k_a[l], rwkv_r_k[l].reshape(GROUP), rwkv_lnx_w[l], rwkv_lnx_b[l])
        y_moba = _moba(p3, bias_tiles, rel_bias)
        y_ret = _ret(p3, ret_tables)
        flat = lambda y: y.reshape(b * t, GROUP)
        x2 = _outproj(x2, p2, flat(y_rwkv), flat(y_moba), flat(y_ret), conv_w[l], w_out_bf, l,
                      final_norm_w, final=(l == depth - 1), seq_len=t)
    return x2.reshape(b, t, d)
```

```python
import functools
import math

import numpy as np
import jax
import jax.numpy as jnp
from jax import lax
from jax.experimental import pallas as pl
from jax.experimental.pallas import tpu as pltpu

F32 = jnp.float32
BF16 = jnp.bfloat16
HIGHEST = lax.Precision.HIGHEST

GROUP = 256
HEADS = 4
HDIM = 64
LORA = 32
DECAY_SCALE = math.exp(-0.5)
LNX_EPS = 64e-5
NORM_EPS = 1e-6
CONV_TAPS = 3
MOBA_BLOCK = 256
MOBA_TOPK = 3
MOBA_QBLOCK = 128
RET_CHUNK = 128
REL_BUCKETS = 32
REL_MAX_DIST = 128
RWKV_CHUNK = 64

P_COLS = 4 * 4 * GROUP + 128
LORA_COL = 4 * 4 * GROUP

VMEM_LIMIT_V7X = 48 * 1024 * 1024


def _nn(a, b, precision=None):
    return lax.dot_general(a, b, (((1,), (0,)), ((), ())), precision=precision,
                           preferred_element_type=F32)


def _nt(a, b, precision=None):
    return lax.dot_general(a, b, (((1,), (1,)), ((), ())), precision=precision,
                           preferred_element_type=F32)


def _tn(a, b, precision=None):
    return lax.dot_general(a, b, (((0,), (0,)), ((), ())), precision=precision,
                           preferred_element_type=F32)


def _split(x):
    hi = x.astype(BF16)
    lo = (x - hi.astype(F32)).astype(BF16)
    return hi, lo


def _dot3(dot, a, b):
    out_axis = 1 if dot is _tn else 0
    m = a[0].shape[out_axis]
    both = dot(jnp.concatenate([a[0], a[1]], axis=out_axis), b[0])
    return both[:m] + both[m:] + dot(a[0], b[1])


def _psl(pair, rows, cols):
    return pair[0][rows, cols], pair[1][rows, cols]


def _pcat(pairs, axis):
    return (jnp.concatenate([p[0] for p in pairs], axis=axis),
            jnp.concatenate([p[1] for p in pairs], axis=axis))


def _head_sum(x, esum_bf16):
    hi, lo = _split(x)
    m = x.shape[0]
    both = _nn(jnp.concatenate([hi, lo], axis=0), esum_bf16)
    return both[:m] + both[m:]


def _iota(shape, dim):
    return lax.broadcasted_iota(jnp.int32, shape, dim)


def _head_sum_matrix():
    r = _iota((GROUP, GROUP), 0) // HDIM
    c = _iota((GROUP, GROUP), 1) // HDIM
    return jnp.where(r == c, 1.0, 0.0).astype(F32)


def _silu(x):
    return x * jax.nn.sigmoid(x)


def _inproj_kernel(x_ref, nw_ref, w_ref, o_ref):
    x = x_ref[...]
    h = x * lax.rsqrt(jnp.mean(x * x, axis=-1, keepdims=True) + NORM_EPS) * nw_ref[...]
    o_ref[...] = _nn(h.astype(BF16), w_ref[0])


def _inproj(x2d, norm_w, w_bf16, layer, tm=512):
    m, d = x2d.shape
    n = w_bf16.shape[2]
    return pl.pallas_call(
        _inproj_kernel,
        out_shape=jax.ShapeDtypeStruct((m, n), F32),
        grid=(m // tm,),
        in_specs=[pl.BlockSpec((tm, d), lambda i: (i, 0)),
                  pl.BlockSpec((1, d), lambda i: (0, 0)),
                  pl.BlockSpec((1, d, n), lambda i: (layer, 0, 0))],
        out_specs=pl.BlockSpec((tm, n), lambda i: (i, 0)),
        compiler_params=pltpu.CompilerParams(
            dimension_semantics=("arbitrary",), vmem_limit_bytes=VMEM_LIMIT_V7X),
        name="inproj",
    )(x2d, norm_w.reshape(1, d), w_bf16)


def _gated_conv(p, cw_ref, tail_ref, first_tile):
    @pl.when(first_tile)
    def _():
        tail_ref[...] = jnp.zeros_like(tail_ref)

    n_rows = p.shape[0]
    u = p[:, GROUP:2 * GROUP] * p[:, 2 * GROUP:3 * GROUP]
    rows = _iota((n_rows, 1), 0)
    u1 = jnp.where(rows == 0, tail_ref[1:2, :], pltpu.roll(u, 1, axis=0))
    u2 = jnp.where(rows == 0, tail_ref[0:1, :],
                   jnp.where(rows == 1, tail_ref[1:2, :], pltpu.roll(u, 2, axis=0)))
    tail_ref[0:2, :] = u[n_rows - 2:n_rows, :]
    y = u2 * cw_ref[0:1, :] + u1 * cw_ref[1:2, :] + u * cw_ref[2:3, :]
    return p[:, 0:GROUP] * y * _silu(p[:, 3 * GROUP:4 * GROUP])


def _outproj_kernel(x_ref, y0_ref, pconv_ref, cw_ref, y2_ref, y3_ref, w_ref, fw_ref, o_ref, tail_ref,
                    *, final, tiles_per_seq):
    y_conv = _gated_conv(pconv_ref[...], cw_ref, tail_ref, pl.program_id(0) % tiles_per_seq == 0)
    acc = x_ref[...]
    for g, y in enumerate((y0_ref[...], y_conv, y2_ref[...], y3_ref[...])):
        acc = acc + _nn(y.astype(BF16), w_ref[0, g * GROUP:(g + 1) * GROUP, :])
    if final:
        acc = acc * lax.rsqrt(jnp.mean(acc * acc, axis=-1, keepdims=True) + NORM_EPS) * fw_ref[...]
    o_ref[...] = acc


def _outproj(x2d, p2d, y_rwkv, y_moba, y_ret, conv_w, w_bf16, layer, final_w, final, seq_len, tm=512):
    m, d = x2d.shape
    assert seq_len % tm == 0
    yspec = pl.BlockSpec((tm, GROUP), lambda i: (i, 0))
    return pl.pallas_call(
        functools.partial(_outproj_kernel, final=final, tiles_per_seq=seq_len // tm),
        out_shape=jax.ShapeDtypeStruct((m, d), F32),
        grid=(m // tm,),
        in_specs=[pl.BlockSpec((tm, d), lambda i: (i, 0)), yspec,
                  pl.BlockSpec((tm, 4 * GROUP), lambda i: (i, 1)),
                  pl.BlockSpec((CONV_TAPS, GROUP), lambda i: (0, 0)),
                  yspec, yspec,
                  pl.BlockSpec((1, 4 * GROUP, d), lambda i: (layer, 0, 0)),
                  pl.BlockSpec((1, d), lambda i: (0, 0))],
        out_specs=pl.BlockSpec((tm, d), lambda i: (i, 0)),
        scratch_shapes=[pltpu.VMEM((8, GROUP), F32)],
        compiler_params=pltpu.CompilerParams(
            dimension_semantics=("arbitrary",), vmem_limit_bytes=VMEM_LIMIT_V7X),
        name="outproj_final" if final else "outproj",
    )(x2d, y_rwkv, p2d, conv_w, y_moba, y_ret, w_bf16, final_w.reshape(1, d))


def _rwkv_kernel(rkv_ref, gate_ref, lora_ref, mu_ref, mul_ref, w0_ref, w2_ref, a0_ref, a2_ref,
                 kk_ref, ka_ref, rk_ref, lnw_ref, lnb_ref, o_ref,
                 state_ref, prev_ref, prevl_ref, *, tt, nseq):
    C = RWKV_CHUNK
    n_rows = nseq * tt
    t_idx = pl.program_id(1)

    @pl.when(t_idx == 0)
    def _():
        state_ref[...] = jnp.zeros_like(state_ref)
        prev_ref[...] = jnp.zeros_like(prev_ref)
        prevl_ref[...] = jnp.zeros_like(prevl_ref)

    p = rkv_ref[...].reshape(n_rows, 3 * GROUP)
    lo = lora_ref[...].reshape(n_rows, 128)
    row = _iota((n_rows, 1), 0)
    p_sh = pltpu.roll(p, 1, axis=0)
    lo_sh = pltpu.roll(lo, 1, axis=0)
    for s in range(nseq):
        p_sh = jnp.where(row == s * tt, prev_ref[s:s + 1, :], p_sh)
        lo_sh = jnp.where(row == s * tt, prevl_ref[s:s + 1, :], lo_sh)
    for s in range(nseq):
        prev_ref[s:s + 1, :] = p[(s + 1) * tt - 1:(s + 1) * tt, :]
        prevl_ref[s:s + 1, :] = lo[(s + 1) * tt - 1:(s + 1) * tt, :]
    p = p + (p_sh - p) * mu_ref[...]
    lo = lo + (lo_sh - lo) * mul_ref[...]

    r = p[:, 0:GROUP]
    k = p[:, GROUP:2 * GROUP]
    v = p[:, 2 * GROUP:3 * GROUP]

    esum = _head_sum_matrix().astype(BF16)
    lw = -DECAY_SCALE * jax.nn.sigmoid(
        w0_ref[...] + _dot3(_nn, _split(jnp.tanh(lo)), _split(w2_ref[...])))
    rate = jax.nn.sigmoid(a0_ref[...] + _dot3(_nn, _split(lo), _split(a2_ref[...])))
    kk = k * kk_ref[...]
    kk = kk / jnp.maximum(jnp.sqrt(_head_sum(kk * kk, esum)), 1e-12)
    k2 = k * (1.0 + (rate - 1.0) * ka_ref[...])
    bonus = _head_sum(r * k2 * rk_ref[...], esum) * v
    bvec = kk * rate
    avec = -kk

    rowmod = row % C
    cum = lw
    for sh in (1, 2, 4, 8, 16, 32):
        cum = cum + jnp.where(rowmod >= sh, pltpu.roll(cum, sh, axis=0), 0.0)
    tot_rows = [cum[c * C + C - 1:c * C + C, :] for c in range(n_rows // C)]
    tot = jnp.concatenate([jnp.broadcast_to(tr, (C, GROUP)) for tr in tot_rows], axis=0)
    e_neg = jnp.exp(-cum)
    e_end = jnp.exp(tot - cum)
    r_t = _split(r * jnp.exp(cum))
    a_t = _split(avec * jnp.exp(cum - lw))
    k_t = _split(k2 * e_neg)
    b_t = _split(bvec * e_neg)
    k_h = _split(k2 * e_end)
    b_h = _split(bvec * e_end)
    v_s = _split(v)

    same_head = (_iota((GROUP, GROUP), 0) // HDIM) == (_iota((GROUP, GROUP), 1) // HDIM)

    def bd(pr):
        return tuple(jnp.where(same_head, jnp.concatenate([part] * HEADS, axis=0), jnp.zeros((), part.dtype))
                     for part in pr)

    t_i = _iota((2 * C, GROUP), 0)
    s_i = _iota((2 * C, GROUP), 1) % C
    tri = ((t_i < C) & (s_i < t_i)) | ((t_i >= C) & (s_i <= t_i - C))
    eye = jnp.where(_iota((C, GROUP), 0) == _iota((C, GROUP), 1) % C, 1.0, 0.0).astype(F32)

    per_seq = tt // C
    chunks = range(nseq * per_seq)
    every = slice(None)
    rows = lambda pr, c: _psl(pr, slice(c * C, (c + 1) * C), every)
    top = lambda pr: _psl(pr, slice(0, C), every)
    bottom = lambda pr: _psl(pr, slice(C, 2 * C), every)

    ar = [_pcat([rows(a_t, c), rows(r_t, c)], 0) for c in chunks]
    bd_v = [bd(rows(v_s, c)) for c in chunks]
    ab_rb = [jnp.where(tri, _dot3(_nt, ar[c], bd(rows(b_t, c))), 0.0) for c in chunks]
    ak_rk = [_split(jnp.where(tri, _dot3(_nt, ar[c], bd(rows(k_t, c))), 0.0)) for c in chunks]
    a_ab = [ab_rb[c][0:C] for c in chunks]
    ab_rb = [_split(x) for x in ab_rb]
    t_row = _iota((C, GROUP), 0)
    s_col = _iota((C, GROUP), 1) % C

    def below_diagonal(n):
        return ((t_row // (2 * n)) == (s_col // (2 * n))) & ((t_row % (2 * n)) >= n) & ((s_col % (2 * n)) < n)

    inv = [eye + jnp.where(below_diagonal(1), a_ab[c], 0.0) for c in chunks]
    n = 2
    while n < C:
        off = below_diagonal(n)
        inv_s = [_split(inv[c]) for c in chunks]
        left = [_split(_dot3(_nn, inv_s[c], bd(_split(jnp.where(off, a_ab[c], 0.0))))) for c in chunks]
        inv = [inv[c] + _dot3(_nn, left[c], bd(inv_s[c])) for c in chunks]
        n *= 2
    inv = [_split(x) for x in inv]
    t_ak = [_split(_dot3(_nn, inv[c], bd(top(ak_rk[c])))) for c in chunks]
    t_a = [_split(_dot3(_nn, inv[c], bd(rows(a_t, c)))) for c in chunks]
    free = [_dot3(_nn, _pcat([t_ak[c], bottom(ak_rk[c])], 0), bd_v[c]) for c in chunks]
    kb_hat = [_pcat([rows(k_h, c), rows(b_h, c)], 0) for c in chunks]
    m_s = [_split(jnp.where(same_head, _dot3(_tn, t_a[c], rows(b_h, c)), 0.0)) for c in chunks]
    g = []
    for c in chunks:
        full = jnp.where(same_head, _dot3(_tn, _pcat([rows(v_s, c), _split(free[c][0:C])], 0), kb_hat[c]), 0.0)
        g.append((full[0:HDIM] + full[HDIM:2 * HDIM]) + (full[2 * HDIM:3 * HDIM] + full[3 * HDIM:4 * HDIM]))
    state = [state_ref[s] for s in range(nseq)]
    states = {}
    for step in range(per_seq):
        for s in range(nseq):
            c = s * per_seq + step
            st_s = _split(state[s])
            states[c] = bd(st_s)
            state[s] = state[s] * jnp.exp(tot_rows[c]) + _dot3(_nn, st_s, m_s[c]) + g[c]
    for s in range(nseq):
        state_ref[s] = state[s]
    from_state = [_dot3(_nt, _pcat([t_a[c], rows(r_t, c)], 0), states[c]) for c in chunks]
    y = jnp.concatenate(
        [free[c][C:2 * C] + from_state[c][C:2 * C]
         + _dot3(_nn, bottom(ab_rb[c]), bd(_split(free[c][0:C] + from_state[c][0:C]))) for c in chunks], axis=0)
    mean = _head_sum(y, esum) * (1.0 / HDIM)
    yc = y - mean
    var = _head_sum(yc * yc, esum) * (1.0 / HDIM)
    y = yc * lax.rsqrt(var + LNX_EPS) * lnw_ref[...] + lnb_ref[...] + bonus
    o_ref[...] = (y * _silu(gate_ref[...].reshape(n_rows, GROUP))).reshape(nseq, tt, GROUP)


def _rwkv(p3, mu, w0, w2, a0, a2, k_k, k_a, r_k, lnx_w, lnx_b, tt=256, nseq=2):
    b, t, _ = p3.shape
    assert b % nseq == 0 and t % tt == 0 and nseq <= 8
    mu_main = mu[:3 * GROUP].reshape(1, 3 * GROUP)
    mu_lora = jnp.concatenate([mu[3 * GROUP:], jnp.zeros((128 - 2 * LORA,), F32)]).reshape(1, 128)
    w2p = jnp.zeros((128, GROUP), F32).at[0:LORA].set(w2)
    a2p = jnp.zeros((128, GROUP), F32).at[LORA:2 * LORA].set(a2)
    row = lambda a: a.reshape(1, GROUP)
    vec = pl.BlockSpec((1, GROUP), lambda i, j: (0, 0))
    return pl.pallas_call(
        functools.partial(_rwkv_kernel, tt=tt, nseq=nseq),
        out_shape=jax.ShapeDtypeStruct((b, t, GROUP), F32),
        grid=(b // nseq, t // tt),
        in_specs=[pl.BlockSpec((nseq, tt, 3 * GROUP), lambda i, j: (i, j, 0)),
                  pl.BlockSpec((nseq, tt, GROUP), lambda i, j: (i, j, 3)),
                  pl.BlockSpec((nseq, tt, 128), lambda i, j: (i, j, LORA_COL // 128)),
                  pl.BlockSpec((1, 3 * GROUP), lambda i, j: (0, 0)),
                  pl.BlockSpec((1, 128), lambda i, j: (0, 0)),
                  vec,
                  pl.BlockSpec((128, GROUP), lambda i, j: (0, 0)),
                  vec,
                  pl.BlockSpec((128, GROUP), lambda i, j: (0, 0)),
                  vec, vec, vec, vec, vec],
        out_specs=pl.BlockSpec((nseq, tt, GROUP), lambda i, j: (i, j, 0)),
        scratch_shapes=[pltpu.VMEM((nseq, HDIM, GROUP), F32),
                        pltpu.VMEM((8, 3 * GROUP), F32),
                        pltpu.VMEM((8, 128), F32)],
        compiler_params=pltpu.CompilerParams(
            dimension_semantics=("arbitrary", "arbitrary"), vmem_limit_bytes=VMEM_LIMIT_V7X),
        name="rwkv7",
    )(p3, p3, p3, mu_main, mu_lora, row(w0), w2p, row(a0), a2p,
      row(k_k), row(k_a), row(r_k), row(lnx_w), row(lnx_b))


def _t5_bucket_np(dist):
    max_exact = REL_BUCKETS // 2
    d_f = np.maximum(dist, 1).astype(np.float32)
    large = max_exact + (np.log(d_f / np.float32(max_exact)) / np.float32(math.log(REL_MAX_DIST / max_exact))
                         * np.float32(REL_BUCKETS - max_exact)).astype(np.int32)
    large = np.minimum(large, REL_BUCKETS - 1)
    return np.where(dist < max_exact, dist, large).astype(np.int32)


MASKED_BUCKET = REL_BUCKETS


def _moba_bucket_table():
    keys = np.arange(MOBA_BLOCK)[:, None]
    queries = np.arange(MOBA_BLOCK)[None, :]
    prev = _t5_bucket_np(queries + MOBA_BLOCK - keys)
    own = np.where(keys <= queries, _t5_bucket_np(np.maximum(queries - keys, 0)), MASKED_BUCKET)
    return np.stack([prev, own]).astype(np.int32)


def _bias_kernel(idx_ref, rb_ref, o_ref):
    h = pl.program_id(0)
    idx = idx_ref[...]
    acc = jnp.full(idx.shape, -jnp.inf, F32)
    for bkt in range(REL_BUCKETS):
        acc = jnp.where(idx == bkt, rb_ref[bkt, h], acc)
    o_ref[0] = acc


def _moba_bias_tiles(rel_bias):
    idx = jnp.asarray(_moba_bucket_table())
    shp = (2, MOBA_BLOCK, MOBA_BLOCK)
    return pl.pallas_call(
        _bias_kernel,
        out_shape=jax.ShapeDtypeStruct((HEADS,) + shp, F32),
        grid=(HEADS,),
        in_specs=[pl.BlockSpec(shp, lambda h: (0, 0, 0)),
                  pl.BlockSpec(memory_space=pltpu.SMEM)],
        out_specs=pl.BlockSpec((1,) + shp, lambda h: (h, 0, 0, 0)),
        name="moba_bias",
    )(idx, rel_bias)


def _moba_kernel(q_ref, k_ref, v_ref, gate_ref, bias_ref, rb_ref, o_ref,
                 kmean_ref, kbf_ref, vt_ref, sel_ref, m_ref, l_ref, acc_ref, *, nb):
    BLK = MOBA_BLOCK
    ib = pl.program_id(1)
    heads = range(HEADS)
    rows_of = lambda h: slice(h * HDIM, (h + 1) * HDIM)
    neg_inf = -jnp.inf
    not_selected = -1e30

    @pl.when(ib == 0)
    def _():
        kmean_ref[...] = jnp.zeros_like(kmean_ref)
        for n in range(nb):
            blk = slice(n * BLK, (n + 1) * BLK)
            kblk = k_ref[0, blk, :]
            kmean_ref[n:n + 1, :] = jnp.mean(kblk, axis=0, keepdims=True)
            kbf_ref[n] = kblk.astype(BF16)
            vt_ref[n, 0:GROUP, :] = v_ref[0, blk, :].T.astype(BF16)
            vt_ref[n, GROUP:GROUP + 16, :] = jnp.ones((16, BLK), BF16)

    q = q_ref[0] * (HDIM ** -0.5)
    q_bf = q.astype(BF16)
    q_s = _split(q)
    lane_head = _iota((1, GROUP), 1) // HDIM
    q_heads = [jnp.where(lane_head == h, q_bf, jnp.zeros_like(q_bf)) for h in heads]

    km = kmean_ref[...]
    gates = [_dot3(_nt, _split(jnp.where(lane_head == h, km, 0.0)), q_s) for h in heads]
    blk_id = _iota((16, BLK), 0)
    for h in heads:
        g = jnp.where(blk_id < ib, gates[h], neg_inf)
        sel = jnp.zeros((16, BLK), F32)
        for _ in range(min(MOBA_TOPK, nb)):
            m = jnp.max(g, axis=0, keepdims=True)
            hit = (g == m) & (m > neg_inf)
            first = jnp.min(jnp.where(hit, blk_id, 16), axis=0, keepdims=True)
            pick = blk_id == first
            sel = jnp.where(pick, 1.0, sel)
            g = jnp.where(pick, neg_inf, g)
        sel_ref[h] = sel

    def attend(blocks, first):
        scores = [[_nt(kbf_ref[n], q_heads[h]) for h in heads] for n, _ in blocks]
        probs, alphas = [], []
        for h in heads:
            ss = [scores[j][h] + extra[h] for j, (_, extra) in enumerate(blocks)]
            m_new = functools.reduce(jnp.maximum, [jnp.max(s, axis=0, keepdims=True) for s in ss])
            if not first:
                m_old = m_ref[h]
                m_new = jnp.maximum(m_old, m_new)
                alphas.append(jnp.exp(m_old - m_new))
            m_ref[h] = m_new
            probs.append([jnp.exp(s - m_new).astype(BF16) for s in ss])
        for h in heads:
            pv = None
            for j, (n, _) in enumerate(blocks):
                lhs = jnp.concatenate([vt_ref[n, rows_of(h), :], vt_ref[n, GROUP:GROUP + 16, :]], axis=0)
                part = _nn(lhs, probs[h][j])
                pv = part if pv is None else pv + part
            if first:
                acc_ref[rows_of(h), :] = pv[0:HDIM]
                l_ref[h] = pv[HDIM:HDIM + 1]
            else:
                acc_ref[rows_of(h), :] = alphas[h] * acc_ref[rows_of(h), :] + pv[0:HDIM]
                l_ref[h] = alphas[h] * l_ref[h] + pv[HDIM:HDIM + 1]

    def selected_row(h, n, valid):
        row = jnp.where(sel_ref[h, pl.ds(n, 1), :] > 0.5, 0.0, not_selected)
        return jnp.where(valid, row, not_selected)

    prev = jnp.maximum(ib - 1, 0)
    attend([(ib, [bias_ref[h, 1] for h in heads]),
            (prev, [bias_ref[h, 0] + selected_row(h, prev, ib >= 1) for h in heads])], first=True)

    n_far = ib - 1
    for n in range(0, nb - 2, 2):
        @pl.when(n < n_far)
        def _(n=n):
            pair = [n] if n + 1 >= nb - 2 else [n, n + 1]
            attend([(j, [selected_row(h, j, j < n_far) + rb_ref[REL_BUCKETS - 1, h] for h in heads])
                    for j in pair], first=False)

    for h in heads:
        acc_ref[rows_of(h), :] = acc_ref[rows_of(h), :] / l_ref[h]
    o_ref[0] = acc_ref[...].T * _silu(gate_ref[0])


def _moba(p3, bias_tiles, rel_bias):
    b, t, _ = p3.shape
    BLK = MOBA_BLOCK
    nb = t // BLK
    assert 2 <= nb <= 16 and t % BLK == 0
    base = 2 * 4
    tile = lambda c: pl.BlockSpec((1, BLK, GROUP), lambda i, j, c=c: (i, j, c))
    seq = lambda c: pl.BlockSpec((1, t, GROUP), lambda i, j, c=c: (i, 0, c))
    return pl.pallas_call(
        functools.partial(_moba_kernel, nb=nb),
        out_shape=jax.ShapeDtypeStruct((b, t, GROUP), F32),
        grid=(b, nb),
        in_specs=[tile(base), seq(base + 1), seq(base + 2), tile(base + 3),
                  pl.BlockSpec(bias_tiles.shape, lambda i, j: (0, 0, 0, 0)),
                  pl.BlockSpec(memory_space=pltpu.SMEM)],
        out_specs=pl.BlockSpec((1, BLK, GROUP), lambda i, j: (i, j, 0)),
        scratch_shapes=[pltpu.VMEM((16, GROUP), F32),
                        pltpu.VMEM((nb, BLK, GROUP), BF16),
                        pltpu.VMEM((nb, GROUP + 16, BLK), BF16),
                        pltpu.VMEM((HEADS, 16, BLK), F32),
                        pltpu.VMEM((HEADS, 1, BLK), F32),
                        pltpu.VMEM((HEADS, 1, BLK), F32),
                        pltpu.VMEM((GROUP, BLK), F32)],
        compiler_params=pltpu.CompilerParams(
            dimension_semantics=("arbitrary", "arbitrary"), vmem_limit_bytes=VMEM_LIMIT_V7X),
        name="moba",
    )(p3, p3, p3, p3, bias_tiles, rel_bias)


def _ret_tables(t):
    half = HDIM // 2
    theta = 1.0 / (10000.0 ** np.linspace(0.0, 1.0, half))
    pos = np.arange(t, dtype=np.float64)
    ang = pos[:, None] * theta[None, :]
    sin, cos = np.sin(ang), np.cos(ang)
    cos2 = np.tile(np.repeat(cos, 2, axis=1), (1, HEADS))
    sin2 = np.tile(np.stack([-sin, sin], axis=-1).reshape(t, HDIM), (1, HEADS))
    log_gamma = np.log(1.0 - 2.0 ** (-5.0 - np.arange(HEADS, dtype=np.float64)))
    idx = np.arange(RET_CHUNK, dtype=np.float64)
    diff = idx[:, None] - idx[None, :]
    decay_intra = np.where(diff >= 0, np.exp(log_gamma[:, None, None] * np.maximum(diff, 0.0)), 0.0)
    q_decay = np.exp(log_gamma[:, None] * (idx + 1.0))
    k_decay = np.exp(log_gamma[:, None] * (RET_CHUNK - 1.0 - idx))
    chunk_decay = np.exp(log_gamma * RET_CHUNK)
    lanes = lambda a: np.repeat(a.T, HDIM, axis=1)
    cd = np.repeat(chunk_decay, HDIM).reshape(1, GROUP)
    return tuple(jnp.asarray(a, F32) for a in (cos2, sin2, decay_intra, lanes(q_decay), lanes(k_decay), cd))


def _ret_kernel(q_ref, k_ref, v_ref, gate_ref, cos_ref, sin_ref, di_ref, qd_ref, kd_ref, cd_ref,
                o_ref, state_ref, *, tt):
    C = RET_CHUNK
    n_chunks = tt // C
    chunks = range(n_chunks)
    heads = range(HEADS)
    b = pl.program_id(1)

    @pl.when(pl.program_id(0) == 0)
    def _():
        state_ref[b] = jnp.zeros((GROUP, GROUP), F32)

    even = (_iota((tt, GROUP), 1) % 2) == 0

    def rotate(x):
        swapped = jnp.where(even, pltpu.roll(x, GROUP - 1, axis=1), pltpu.roll(x, 1, axis=1))
        return x * cos_ref[...] + swapped * sin_ref[...]

    q = rotate(q_ref[0])
    k = rotate(k_ref[0]) * (HDIM ** -0.5)
    lane_head = _iota((1, GROUP), 1) // HDIM
    same_head = (_iota((GROUP, GROUP), 0) // HDIM) == (_iota((GROUP, GROUP), 1) // HDIM)
    rows = lambda c: slice(c * C, (c + 1) * C)

    q_b = [q[rows(c)].astype(BF16) for c in chunks]
    k_b = [k[rows(c)].astype(BF16) for c in chunks]
    v_b = [v_ref[0, rows(c), :].astype(BF16) for c in chunks]
    kd_b = [(k[rows(c)] * kd_ref[...]).astype(BF16) for c in chunks]
    qd_b = [(q[rows(c)] * qd_ref[...]).astype(BF16) for c in chunks]

    inner = [(_nt(jnp.concatenate([jnp.where(lane_head == h, q_b[c], jnp.zeros_like(q_b[c])) for h in heads],
                                  axis=0), k_b[c]) * di_ref[...]).astype(BF16) for c in chunks]
    kv = [jnp.where(same_head, _tn(kd_b[c], v_b[c]), 0.0) for c in chunks]
    intra = []
    for c in chunks:
        stacked = _nn(inner[c], v_b[c])
        acc = jnp.where(lane_head == 0, stacked[0:C], 0.0)
        for h in range(1, HEADS):
            acc = jnp.where(lane_head == h, stacked[h * C:(h + 1) * C], acc)
        intra.append(acc)
    state = state_ref[b]
    states = []
    for c in chunks:
        states.append(state.astype(BF16))
        state = state * cd_ref[...] + kv[c]
    state_ref[b] = state
    y = jnp.concatenate([intra[c] + _nn(qd_b[c], states[c]) for c in chunks], axis=0)
    mean_sq = _head_sum(y * y, _head_sum_matrix().astype(BF16)) * (1.0 / HDIM)
    o_ref[0] = y * lax.rsqrt(mean_sq + NORM_EPS) * _silu(gate_ref[0])


def _ret(p3, tables, tt=512):
    b, t, _ = p3.shape
    base = 3 * 4
    cos2, sin2, di, qd, kd, cd = tables
    di = di.reshape(HEADS * RET_CHUNK, RET_CHUNK)
    blk = lambda c: pl.BlockSpec((1, tt, GROUP), lambda j, i, c=c: (i, j, c))
    full2 = lambda a: pl.BlockSpec(a.shape, lambda j, i: (0, 0))
    return pl.pallas_call(
        functools.partial(_ret_kernel, tt=tt),
        out_shape=jax.ShapeDtypeStruct((b, t, GROUP), F32),
        grid=(t // tt, b),
        in_specs=[blk(base), blk(base + 1), blk(base + 2), blk(base + 3),
                  pl.BlockSpec((tt, GROUP), lambda j, i: (j, 0)),
                  pl.BlockSpec((tt, GROUP), lambda j, i: (j, 0)),
                  full2(di), full2(qd), full2(kd), full2(cd)],
        out_specs=pl.BlockSpec((1, tt, GROUP), lambda j, i: (i, j, 0)),
        scratch_shapes=[pltpu.VMEM((b, GROUP, GROUP), F32)],
        compiler_params=pltpu.CompilerParams(
            dimension_semantics=("arbitrary", "arbitrary"), vmem_limit_bytes=VMEM_LIMIT_V7X),
        name="retnet",
    )(p3, p3, p3, p3, cos2, sin2, di, qd, kd, cd)


def _reorder_w_in(w):
    g3 = 3 * GROUP
    lora = w[..., g3:g3 + 2 * LORA]
    pad = jnp.zeros(w.shape[:-1] + (128 - 2 * LORA,), w.dtype)
    return jnp.concatenate([w[..., :g3], w[..., g3 + 2 * LORA:], lora, pad], axis=-1)


def kernel(x, norm_w, w_in, w_out, rwkv_mu, rwkv_w0, rwkv_w2, rwkv_a0, rwkv_a2, rwkv_k_k, rwkv_k_a,
           rwkv_r_k, rwkv_lnx_w, rwkv_lnx_b, conv_w, rel_bias, final_norm_w):
    b, t, d = x.shape
    depth = w_in.shape[0]
    bias_tiles = _moba_bias_tiles(rel_bias)
    ret_tables = _ret_tables(t)
    x2 = x.reshape(b * t, d)
    w_in_bf = _reorder_w_in(w_in.astype(BF16))
    w_out_bf = w_out.astype(BF16)
    for l in range(depth):
        p2 = _inproj(x2, norm_w[l], w_in_bf, l)
        p3 = p2.reshape(b, t, P_COLS)
        y_rwkv = _rwkv(p3, rwkv_mu[l], rwkv_w0[l], rwkv_w2[l], rwkv_a0[l], rwkv_a2[l], rwkv_k_k[l],
                       rwkv_k_a[l], rwkv_r_k[l].reshape(GROUP), rwkv_lnx_w[l], rwkv_lnx_b[l])
        y_moba = _moba(p3, bias_tiles, rel_bias)
        y_ret = _ret(p3, ret_tables)
        flat = lambda y: y.reshape(b * t, GROUP)
        x2 = _outproj(x2, p2, flat(y_rwkv), flat(y_moba), flat(y_ret), conv_w[l], w_out_bf, l,
                      final_norm_w, final=(l == depth - 1), seq_len=t)
    return x2.reshape(b, t, d)
```

```python
import functools
import math

import numpy as np
import jax
import jax.numpy as jnp
from jax import lax
from jax.experimental import pallas as pl
from jax.experimental.pallas import tpu as pltpu

F32 = jnp.float32
BF16 = jnp.bfloat16
HIGHEST = lax.Precision.HIGHEST

GROUP = 256
HEADS = 4
HDIM = 64
LORA = 32
DECAY_SCALE = math.exp(-0.5)
LNX_EPS = 64e-5
NORM_EPS = 1e-6
CONV_TAPS = 3
MOBA_BLOCK = 256
MOBA_TOPK = 3
MOBA_QBLOCK = 128
RET_CHUNK = 128
REL_BUCKETS = 32
REL_MAX_DIST = 128
RWKV_CHUNK = 64

P_COLS = 4 * 4 * GROUP + 128
LORA_COL = 4 * 4 * GROUP

VMEM_LIMIT_V7X = 48 * 1024 * 1024


def _nn(a, b, precision=None):
    return lax.dot_general(a, b, (((1,), (0,)), ((), ())), precision=precision,
                           preferred_element_type=F32)


def _nt(a, b, precision=None):
    return lax.dot_general(a, b, (((1,), (1,)), ((), ())), precision=precision,
                           preferred_element_type=F32)


def _tn(a, b, precision=None):
    return lax.dot_general(a, b, (((0,), (0,)), ((), ())), precision=precision,
                           preferred_element_type=F32)


def _split(x):
    hi = x.astype(BF16)
    lo = (x - hi.astype(F32)).astype(BF16)
    return hi, lo


def _dot3(dot, a, b):
    out_axis = 1 if dot is _tn else 0
    m = a[0].shape[out_axis]
    both = dot(jnp.concatenate([a[0], a[1]], axis=out_axis), b[0])
    return both[:m] + both[m:] + dot(a[0], b[1])


def _psl(pair, rows, cols):
    return pair[0][rows, cols], pair[1][rows, cols]


def _pcat(pairs, axis):
    return (jnp.concatenate([p[0] for p in pairs], axis=axis),
            jnp.concatenate([p[1] for p in pairs], axis=axis))


def _head_sum(x, esum_bf16):
    hi, lo = _split(x)
    m = x.shape[0]
    both = _nn(jnp.concatenate([hi, lo], axis=0), esum_bf16)
    return both[:m] + both[m:]


def _iota(shape, dim):
    return lax.broadcasted_iota(jnp.int32, shape, dim)


def _head_sum_matrix():
    r = _iota((GROUP, GROUP), 0) // HDIM
    c = _iota((GROUP, GROUP), 1) // HDIM
    return jnp.where(r == c, 1.0, 0.0).astype(F32)


def _silu(x):
    return x * jax.nn.sigmoid(x)


def _inproj_kernel(x_ref, nw_ref, w_ref, o_ref):
    x = x_ref[...]
    h = x * lax.rsqrt(jnp.mean(x * x, axis=-1, keepdims=True) + NORM_EPS) * nw_ref[...]
    o_ref[...] = _nn(h.astype(BF16), w_ref[0])


def _inproj(x2d, norm_w, w_bf16, layer, tm=512):
    m, d = x2d.shape
    n = w_bf16.shape[2]
    return pl.pallas_call(
        _inproj_kernel,
        out_shape=jax.ShapeDtypeStruct((m, n), F32),
        grid=(m // tm,),
        in_specs=[pl.BlockSpec((tm, d), lambda i: (i, 0)),
                  pl.BlockSpec((1, d), lambda i: (0, 0)),
                  pl.BlockSpec((1, d, n), lambda i: (layer, 0, 0))],
        out_specs=pl.BlockSpec((tm, n), lambda i: (i, 0)),
        compiler_params=pltpu.CompilerParams(
            dimension_semantics=("arbitrary",), vmem_limit_bytes=VMEM_LIMIT_V7X),
        name="inproj",
    )(x2d, norm_w.reshape(1, d), w_bf16)


def _gated_conv(p, cw_ref, tail_ref, first_tile):
    @pl.when(first_tile)
    def _():
        tail_ref[...] = jnp.zeros_like(tail_ref)

    n_rows = p.shape[0]
    u = p[:, GROUP:2 * GROUP] * p[:, 2 * GROUP:3 * GROUP]
    rows = _iota((n_rows, 1), 0)
    u1 = jnp.where(rows == 0, tail_ref[1:2, :], pltpu.roll(u, 1, axis=0))
    u2 = jnp.where(rows == 0, tail_ref[0:1, :],
                   jnp.where(rows == 1, tail_ref[1:2, :], pltpu.roll(u, 2, axis=0)))
    tail_ref[0:2, :] = u[n_rows - 2:n_rows, :]
    y = u2 * cw_ref[0:1, :] + u1 * cw_ref[1:2, :] + u * cw_ref[2:3, :]
    return p[:, 0:GROUP] * y * _silu(p[:, 3 * GROUP:4 * GROUP])


def _outproj_kernel(x_ref, y0_ref, pconv_ref, cw_ref, y2_ref, y3_ref, w_ref, fw_ref, o_ref, tail_ref,
                    *, final, tiles_per_seq):
    y_conv = _gated_conv(pconv_ref[...], cw_ref, tail_ref, pl.program_id(0) % tiles_per_seq == 0)
    acc = x_ref[...]
    for g, y in enumerate((y0_ref[...], y_conv, y2_ref[...], y3_ref[...])):
        acc = acc + _nn(y.astype(BF16), w_ref[0, g * GROUP:(g + 1) * GROUP, :])
    if final:
        acc = acc * lax.rsqrt(jnp.mean(acc * acc, axis=-1, keepdims=True) + NORM_EPS) * fw_ref[...]
    o_ref[...] = acc


def _outproj(x2d, p2d, y_rwkv, y_moba, y_ret, conv_w, w_bf16, layer, final_w, final, seq_len, tm=512):
    m, d = x2d.shape
    assert seq_len % tm == 0
    yspec = pl.BlockSpec((tm, GROUP), lambda i: (i, 0))
    return pl.pallas_call(
        functools.partial(_outproj_kernel, final=final, tiles_per_seq=seq_len // tm),
        out_shape=jax.ShapeDtypeStruct((m, d), F32),
        grid=(m // tm,),
        in_specs=[pl.BlockSpec((tm, d), lambda i: (i, 0)), yspec,
                  pl.BlockSpec((tm, 4 * GROUP), lambda i: (i, 1)),
                  pl.BlockSpec((CONV_TAPS, GROUP), lambda i: (0, 0)),
                  yspec, yspec,
                  pl.BlockSpec((1, 4 * GROUP, d), lambda i: (layer, 0, 0)),
                  pl.BlockSpec((1, d), lambda i: (0, 0))],
        out_specs=pl.BlockSpec((tm, d), lambda i: (i, 0)),
        scratch_shapes=[pltpu.VMEM((8, GROUP), F32)],
        compiler_params=pltpu.CompilerParams(
            dimension_semantics=("arbitrary",), vmem_limit_bytes=VMEM_LIMIT_V7X),
        name="outproj_final" if final else "outproj",
    )(x2d, y_rwkv, p2d, conv_w, y_moba, y_ret, w_bf16, final_w.reshape(1, d))


def _rwkv_kernel(rkv_ref, gate_ref, lora_ref, mu_ref, mul_ref, w0_ref, w2_ref, a0_ref, a2_ref,
                 kk_ref, ka_ref, rk_ref, lnw_ref, lnb_ref, o_ref,
                 state_ref, prev_ref, prevl_ref, *, tt, nseq):
    C = RWKV_CHUNK
    n_rows = nseq * tt
    t_idx = pl.program_id(1)

    @pl.when(t_idx == 0)
    def _():
        state_ref[...] = jnp.zeros_like(state_ref)
        prev_ref[...] = jnp.zeros_like(prev_ref)
        prevl_ref[...] = jnp.zeros_like(prevl_ref)

    p = rkv_ref[...].reshape(n_rows, 3 * GROUP)
    lo = lora_ref[...].reshape(n_rows, 128)
    row = _iota((n_rows, 1), 0)
    p_sh = pltpu.roll(p, 1, axis=0)
    lo_sh = pltpu.roll(lo, 1, axis=0)
    for s in range(nseq):
        p_sh = jnp.where(row == s * tt, prev_ref[s:s + 1, :], p_sh)
        lo_sh = jnp.where(row == s * tt, prevl_ref[s:s + 1, :], lo_sh)
    for s in range(nseq):
        prev_ref[s:s + 1, :] = p[(s + 1) * tt - 1:(s + 1) * tt, :]
        prevl_ref[s:s + 1, :] = lo[(s + 1) * tt - 1:(s + 1) * tt, :]
    p = p + (p_sh - p) * mu_ref[...]
    lo = lo + (lo_sh - lo) * mul_ref[...]

    r = p[:, 0:GROUP]
    k = p[:, GROUP:2 * GROUP]
    v = p[:, 2 * GROUP:3 * GROUP]

    esum = _head_sum_matrix().astype(BF16)
    lw = -DECAY_SCALE * jax.nn.sigmoid(
        w0_ref[...] + _dot3(_nn, _split(jnp.tanh(lo)), _split(w2_ref[...])))
    rate = jax.nn.sigmoid(a0_ref[...] + _dot3(_nn, _split(lo), _split(a2_ref[...])))
    kk = k * kk_ref[...]
    kk = kk / jnp.maximum(jnp.sqrt(_head_sum(kk * kk, esum)), 1e-12)
    k2 = k * (1.0 + (rate - 1.0) * ka_ref[...])
    bonus = _head_sum(r * k2 * rk_ref[...], esum) * v
    bvec = kk * rate
    avec = -kk

    rowmod = row % C
    cum = lw
    for sh in (1, 2, 4, 8, 16, 32):
        cum = cum + jnp.where(rowmod >= sh, pltpu.roll(cum, sh, axis=0), 0.0)
    tot_rows = [cum[c * C + C - 1:c * C + C, :] for c in range(n_rows // C)]
    tot = jnp.concatenate([jnp.broadcast_to(tr, (C, GROUP)) for tr in tot_rows], axis=0)
    e_neg = jnp.exp(-cum)
    e_end = jnp.exp(tot - cum)
    r_t = _split(r * jnp.exp(cum))
    a_t = _split(avec * jnp.exp(cum - lw))
    k_t = _split(k2 * e_neg)
    b_t = _split(bvec * e_neg)
    k_h = _split(k2 * e_end)
    b_h = _split(bvec * e_end)
    v_s = _split(v)

    same_head = (_iota((GROUP, GROUP), 0) // HDIM) == (_iota((GROUP, GROUP), 1) // HDIM)

    def bd(pr):
        return tuple(jnp.where(same_head, jnp.concatenate([part] * HEADS, axis=0), jnp.zeros((), part.dtype))
                     for part in pr)

    t_i = _iota((2 * C, GROUP), 0)
    s_i = _iota((2 * C, GROUP), 1) % C
    tri = ((t_i < C) & (s_i < t_i)) | ((t_i >= C) & (s_i <= t_i - C))
    eye = jnp.where(_iota((C, GROUP), 0) == _iota((C, GROUP), 1) % C, 1.0, 0.0).astype(F32)

    per_seq = tt // C
    chunks = range(nseq * per_seq)
    every = slice(None)
    rows = lambda pr, c: _psl(pr, slice(c * C, (c + 1) * C), every)
    top = lambda pr: _psl(pr, slice(0, C), every)
    bottom = lambda pr: _psl(pr, slice(C, 2 * C), every)

    ar = [_pcat([rows(a_t, c), rows(r_t, c)], 0) for c in chunks]
    bd_v = [bd(rows(v_s, c)) for c in chunks]
    ab_rb = [jnp.where(tri, _dot3(_nt, ar[c], bd(rows(b_t, c))), 0.0) for c in chunks]
    ak_rk = [_split(jnp.where(tri, _dot3(_nt, ar[c], bd(rows(k_t, c))), 0.0)) for c in chunks]
    a_ab = [ab_rb[c][0:C] for c in chunks]
    ab_rb = [_split(x) for x in ab_rb]
    t_row = _iota((C, GROUP), 0)
    s_col = _iota((C, GROUP), 1) % C

    def below_diagonal(n):
        return ((t_row // (2 * n)) == (s_col // (2 * n))) & ((t_row % (2 * n)) >= n) & ((s_col % (2 * n)) < n)

    inv = [eye + jnp.where(below_diagonal(1), a_ab[c], 0.0) for c in chunks]
    n = 2
    while n < C:
        off = below_diagonal(n)
        inv_s = [_split(inv[c]) for c in chunks]
        left = [_split(_dot3(_nn, inv_s[c], bd(_split(jnp.where(off, a_ab[c], 0.0))))) for c in chunks]
        inv = [inv[c] + _dot3(_nn, left[c], bd(inv_s[c])) for c in chunks]
        n *= 2
    inv = [_split(x) for x in inv]
    t_ak = [_split(_dot3(_nn, inv[c], bd(top(ak_rk[c])))) for c in chunks]
    t_a = [_split(_dot3(_nn, inv[c], bd(rows(a_t, c)))) for c in chunks]
    free = [_dot3(_nn, _pcat([t_ak[c], bottom(ak_rk[c])], 0), bd_v[c]) for c in chunks]
    kb_hat = [_pcat([rows(k_h, c), rows(b_h, c)], 0) for c in chunks]
    m_s = [_split(jnp.where(same_head, _dot3(_tn, t_a[c], rows(b_h, c)), 0.0)) for c in chunks]
    g = []
    for c in chunks:
        full = jnp.where(same_head, _dot3(_tn, _pcat([rows(v_s, c), _split(free[c][0:C])], 0), kb_hat[c]), 0.0)
        g.append((full[0:HDIM] + full[HDIM:2 * HDIM]) + (full[2 * HDIM:3 * HDIM] + full[3 * HDIM:4 * HDIM]))
    state = [state_ref[s] for s in range(nseq)]
    from_state = {}
    for step in range(per_seq):
        for s in range(nseq):
            c = s * per_seq + step
            st_s = _split(state[s])
            state[s] = state[s] * jnp.exp(tot_rows[c]) + _dot3(_nn, st_s, m_s[c]) + g[c]
            from_state[c] = _dot3(_nt, _pcat([t_a[c], rows(r_t, c)], 0), bd(st_s))
    for s in range(nseq):
        state_ref[s] = state[s]
    y = jnp.concatenate(
        [free[c][C:2 * C] + from_state[c][C:2 * C]
         + _dot3(_nn, bottom(ab_rb[c]), bd(_split(free[c][0:C] + from_state[c][0:C]))) for c in chunks], axis=0)
    mean = _head_sum(y, esum) * (1.0 / HDIM)
    yc = y - mean
    var = _head_sum(yc * yc, esum) * (1.0 / HDIM)
    y = yc * lax.rsqrt(var + LNX_EPS) * lnw_ref[...] + lnb_ref[...] + bonus
    o_ref[...] = (y * _silu(gate_ref[...].reshape(n_rows, GROUP))).astype(BF16).reshape(nseq, tt, GROUP)


def _rwkv(p3, mu, w0, w2, a0, a2, k_k, k_a, r_k, lnx_w, lnx_b, tt=256, nseq=2):
    b, t, _ = p3.shape
    assert b % nseq == 0 and t % tt == 0 and nseq <= 8
    mu_main = mu[:3 * GROUP].reshape(1, 3 * GROUP)
    mu_lora = jnp.concatenate([mu[3 * GROUP:], jnp.zeros((128 - 2 * LORA,), F32)]).reshape(1, 128)
    w2p = jnp.zeros((128, GROUP), F32).at[0:LORA].set(w2)
    a2p = jnp.zeros((128, GROUP), F32).at[LORA:2 * LORA].set(a2)
    row = lambda a: a.reshape(1, GROUP)
    vec = pl.BlockSpec((1, GROUP), lambda i, j: (0, 0))
    return pl.pallas_call(
        functools.partial(_rwkv_kernel, tt=tt, nseq=nseq),
        out_shape=jax.ShapeDtypeStruct((b, t, GROUP), BF16),
        grid=(b // nseq, t // tt),
        in_specs=[pl.BlockSpec((nseq, tt, 3 * GROUP), lambda i, j: (i, j, 0)),
                  pl.BlockSpec((nseq, tt, GROUP), lambda i, j: (i, j, 3)),
                  pl.BlockSpec((nseq, tt, 128), lambda i, j: (i, j, LORA_COL // 128)),
                  pl.BlockSpec((1, 3 * GROUP), lambda i, j: (0, 0)),
                  pl.BlockSpec((1, 128), lambda i, j: (0, 0)),
                  vec,
                  pl.BlockSpec((128, GROUP), lambda i, j: (0, 0)),
                  vec,
                  pl.BlockSpec((128, GROUP), lambda i, j: (0, 0)),
                  vec, vec, vec, vec, vec],
        out_specs=pl.BlockSpec((nseq, tt, GROUP), lambda i, j: (i, j, 0)),
        scratch_shapes=[pltpu.VMEM((nseq, HDIM, GROUP), F32),
                        pltpu.VMEM((8, 3 * GROUP), F32),
                        pltpu.VMEM((8, 128), F32)],
        compiler_params=pltpu.CompilerParams(
            dimension_semantics=("arbitrary", "arbitrary"), vmem_limit_bytes=VMEM_LIMIT_V7X),
        name="rwkv7",
    )(p3, p3, p3, mu_main, mu_lora, row(w0), w2p, row(a0), a2p,
      row(k_k), row(k_a), row(r_k), row(lnx_w), row(lnx_b))


def _t5_bucket_np(dist):
    max_exact = REL_BUCKETS // 2
    d_f = np.maximum(dist, 1).astype(np.float32)
    large = max_exact + (np.log(d_f / np.float32(max_exact)) / np.float32(math.log(REL_MAX_DIST / max_exact))
                         * np.float32(REL_BUCKETS - max_exact)).astype(np.int32)
    large = np.minimum(large, REL_BUCKETS - 1)
    return np.where(dist < max_exact, dist, large).astype(np.int32)


MASKED_BUCKET = REL_BUCKETS


def _moba_bucket_table():
    keys = np.arange(MOBA_BLOCK)[:, None]
    queries = np.arange(MOBA_BLOCK)[None, :]
    prev = _t5_bucket_np(queries + MOBA_BLOCK - keys)
    own = np.where(keys <= queries, _t5_bucket_np(np.maximum(queries - keys, 0)), MASKED_BUCKET)
    return np.stack([prev, own]).astype(np.int32)


def _bias_kernel(idx_ref, rb_ref, o_ref):
    h = pl.program_id(0)
    idx = idx_ref[...]
    acc = jnp.full(idx.shape, -jnp.inf, F32)
    for bkt in range(REL_BUCKETS):
        acc = jnp.where(idx == bkt, rb_ref[bkt, h], acc)
    o_ref[0] = acc


def _moba_bias_tiles(rel_bias):
    idx = jnp.asarray(_moba_bucket_table())
    shp = (2, MOBA_BLOCK, MOBA_BLOCK)
    return pl.pallas_call(
        _bias_kernel,
        out_shape=jax.ShapeDtypeStruct((HEADS,) + shp, F32),
        grid=(HEADS,),
        in_specs=[pl.BlockSpec(shp, lambda h: (0, 0, 0)),
                  pl.BlockSpec(memory_space=pltpu.SMEM)],
        out_specs=pl.BlockSpec((1,) + shp, lambda h: (h, 0, 0, 0)),
        name="moba_bias",
    )(idx, rel_bias)


def _moba_kernel(q_ref, k_ref, v_ref, gate_ref, bias_ref, rb_ref, o_ref,
                 kmean_ref, kbf_ref, vt_ref, sel_ref, m_ref, l_ref, acc_ref, *, nb):
    BLK = MOBA_BLOCK
    ib = pl.program_id(1)
    heads = range(HEADS)
    rows_of = lambda h: slice(h * HDIM, (h + 1) * HDIM)
    neg_inf = -jnp.inf
    not_selected = -1e30

    @pl.when(ib == 0)
    def _():
        kmean_ref[...] = jnp.zeros_like(kmean_ref)
        for n in range(nb):
            blk = slice(n * BLK, (n + 1) * BLK)
            kblk = k_ref[0, blk, :]
            kmean_ref[n:n + 1, :] = jnp.mean(kblk, axis=0, keepdims=True)
            kbf_ref[n] = kblk.astype(BF16)
            vt_ref[n, 0:GROUP, :] = v_ref[0, blk, :].T.astype(BF16)
            vt_ref[n, GROUP:GROUP + 16, :] = jnp.ones((16, BLK), BF16)

    q = q_ref[0] * (HDIM ** -0.5)
    q_bf = q.astype(BF16)
    q_s = _split(q)
    lane_head = _iota((1, GROUP), 1) // HDIM
    q_heads = [jnp.where(lane_head == h, q_bf, jnp.zeros_like(q_bf)) for h in heads]

    km = kmean_ref[...]
    gates = [_dot3(_nt, _split(jnp.where(lane_head == h, km, 0.0)), q_s) for h in heads]
    blk_id = _iota((16, BLK), 0)
    for h in heads:
        g = jnp.where(blk_id < ib, gates[h], neg_inf)
        sel = jnp.zeros((16, BLK), F32)
        for _ in range(min(MOBA_TOPK, nb)):
            m = jnp.max(g, axis=0, keepdims=True)
            hit = (g == m) & (m > neg_inf)
            first = jnp.min(jnp.where(hit, blk_id, 16), axis=0, keepdims=True)
            pick = blk_id == first
            sel = jnp.where(pick, 1.0, sel)
            g = jnp.where(pick, neg_inf, g)
        sel_ref[h] = sel

    def attend(blocks, first):
        scores = [[_nt(kbf_ref[n], q_heads[h]) for h in heads] for n, _ in blocks]
        probs, alphas = [], []
        for h in heads:
            ss = [scores[j][h] + extra[h] for j, (_, extra) in enumerate(blocks)]
            m_new = functools.reduce(jnp.maximum, [jnp.max(s, axis=0, keepdims=True) for s in ss])
            if not first:
                m_old = m_ref[h]
                m_new = jnp.maximum(m_old, m_new)
                alphas.append(jnp.exp(m_old - m_new))
            m_ref[h] = m_new
            probs.append([jnp.exp(s - m_new).astype(BF16) for s in ss])
        for h in heads:
            pv = None
            for j, (n, _) in enumerate(blocks):
                lhs = jnp.concatenate([vt_ref[n, rows_of(h), :], vt_ref[n, GROUP:GROUP + 16, :]], axis=0)
                part = _nn(lhs, probs[h][j])
                pv = part if pv is None else pv + part
            if first:
                acc_ref[rows_of(h), :] = pv[0:HDIM]
                l_ref[h] = pv[HDIM:HDIM + 1]
            else:
                acc_ref[rows_of(h), :] = alphas[h] * acc_ref[rows_of(h), :] + pv[0:HDIM]
                l_ref[h] = alphas[h] * l_ref[h] + pv[HDIM:HDIM + 1]

    def selected_row(h, n, valid):
        row = jnp.where(sel_ref[h, pl.ds(n, 1), :] > 0.5, 0.0, not_selected)
        return jnp.where(valid, row, not_selected)

    prev = jnp.maximum(ib - 1, 0)
    attend([(ib, [bias_ref[h, 1] for h in heads]),
            (prev, [bias_ref[h, 0] + selected_row(h, prev, ib >= 1) for h in heads])], first=True)

    n_far = ib - 1
    for n in range(0, nb - 2, 2):
        @pl.when(n < n_far)
        def _(n=n):
            pair = [n] if n + 1 >= nb - 2 else [n, n + 1]
            attend([(j, [selected_row(h, j, j < n_far) + rb_ref[REL_BUCKETS - 1, h] for h in heads])
                    for j in pair], first=False)

    for h in heads:
        acc_ref[rows_of(h), :] = acc_ref[rows_of(h), :] / l_ref[h]
    o_ref[0] = (acc_ref[...].T * _silu(gate_ref[0])).astype(BF16)


def _moba(p3, bias_tiles, rel_bias):
    b, t, _ = p3.shape
    BLK = MOBA_BLOCK
    nb = t // BLK
    assert 2 <= nb <= 16 and t % BLK == 0
    base = 2 * 4
    tile = lambda c: pl.BlockSpec((1, BLK, GROUP), lambda i, j, c=c: (i, j, c))
    seq = lambda c: pl.BlockSpec((1, t, GROUP), lambda i, j, c=c: (i, 0, c))
    return pl.pallas_call(
        functools.partial(_moba_kernel, nb=nb),
        out_shape=jax.ShapeDtypeStruct((b, t, GROUP), BF16),
        grid=(b, nb),
        in_specs=[tile(base), seq(base + 1), seq(base + 2), tile(base + 3),
                  pl.BlockSpec(bias_tiles.shape, lambda i, j: (0, 0, 0, 0)),
                  pl.BlockSpec(memory_space=pltpu.SMEM)],
        out_specs=pl.BlockSpec((1, BLK, GROUP), lambda i, j: (i, j, 0)),
        scratch_shapes=[pltpu.VMEM((16, GROUP), F32),
                        pltpu.VMEM((nb, BLK, GROUP), BF16),
                        pltpu.VMEM((nb, GROUP + 16, BLK), BF16),
                        pltpu.VMEM((HEADS, 16, BLK), F32),
                        pltpu.VMEM((HEADS, 1, BLK), F32),
                        pltpu.VMEM((HEADS, 1, BLK), F32),
                        pltpu.VMEM((GROUP, BLK), F32)],
        compiler_params=pltpu.CompilerParams(
            dimension_semantics=("arbitrary", "arbitrary"), vmem_limit_bytes=VMEM_LIMIT_V7X),
        name="moba",
    )(p3, p3, p3, p3, bias_tiles, rel_bias)


def _ret_tables(t):
    half = HDIM // 2
    theta = 1.0 / (10000.0 ** np.linspace(0.0, 1.0, half))
    pos = np.arange(t, dtype=np.float64)
    ang = pos[:, None] * theta[None, :]
    sin, cos = np.sin(ang), np.cos(ang)
    cos2 = np.tile(np.repeat(cos, 2, axis=1), (1, HEADS))
    sin2 = np.tile(np.stack([-sin, sin], axis=-1).reshape(t, HDIM), (1, HEADS))
    log_gamma = np.log(1.0 - 2.0 ** (-5.0 - np.arange(HEADS, dtype=np.float64)))
    idx = np.arange(RET_CHUNK, dtype=np.float64)
    diff = idx[:, None] - idx[None, :]
    decay_intra = np.where(diff >= 0, np.exp(log_gamma[:, None, None] * np.maximum(diff, 0.0)), 0.0)
    q_decay = np.exp(log_gamma[:, None] * (idx + 1.0))
    k_decay = np.exp(log_gamma[:, None] * (RET_CHUNK - 1.0 - idx))
    chunk_decay = np.exp(log_gamma * RET_CHUNK)
    lanes = lambda a: np.repeat(a.T, HDIM, axis=1)
    cd = np.repeat(chunk_decay, HDIM).reshape(1, GROUP)
    return tuple(jnp.asarray(a, F32) for a in (cos2, sin2, decay_intra, lanes(q_decay), lanes(k_decay), cd))


def _ret_kernel(q_ref, k_ref, v_ref, gate_ref, cos_ref, sin_ref, di_ref, qd_ref, kd_ref, cd_ref,
                o_ref, state_ref, *, tt):
    C = RET_CHUNK
    n_chunks = tt // C
    chunks = range(n_chunks)
    heads = range(HEADS)
    b = pl.program_id(1)

    @pl.when(pl.program_id(0) == 0)
    def _():
        state_ref[b] = jnp.zeros((GROUP, GROUP), F32)

    even = (_iota((tt, GROUP), 1) % 2) == 0

    def rotate(x):
        swapped = jnp.where(even, pltpu.roll(x, GROUP - 1, axis=1), pltpu.roll(x, 1, axis=1))
        return x * cos_ref[...] + swapped * sin_ref[...]

    q = rotate(q_ref[0])
    k = rotate(k_ref[0]) * (HDIM ** -0.5)
    lane_head = _iota((1, GROUP), 1) // HDIM
    same_head = (_iota((GROUP, GROUP), 0) // HDIM) == (_iota((GROUP, GROUP), 1) // HDIM)
    rows = lambda c: slice(c * C, (c + 1) * C)

    q_b = [q[rows(c)].astype(BF16) for c in chunks]
    k_b = [k[rows(c)].astype(BF16) for c in chunks]
    v_b = [v_ref[0, rows(c), :].astype(BF16) for c in chunks]
    kd_b = [(k[rows(c)] * kd_ref[...]).astype(BF16) for c in chunks]
    qd_b = [(q[rows(c)] * qd_ref[...]).astype(BF16) for c in chunks]

    inner = [(_nt(jnp.concatenate([jnp.where(lane_head == h, q_b[c], jnp.zeros_like(q_b[c])) for h in heads],
                                  axis=0), k_b[c]) * di_ref[...]).astype(BF16) for c in chunks]
    kv = [jnp.where(same_head, _tn(kd_b[c], v_b[c]), 0.0) for c in chunks]
    intra = []
    for c in chunks:
        stacked = _nn(inner[c], v_b[c])
        acc = jnp.where(lane_head == 0, stacked[0:C], 0.0)
        for h in range(1, HEADS):
            acc = jnp.where(lane_head == h, stacked[h * C:(h + 1) * C], acc)
        intra.append(acc)
    state = state_ref[b]
    states = []
    for c in chunks:
        states.append(state.astype(BF16))
        state = state * cd_ref[...] + kv[c]
    state_ref[b] = state
    y = jnp.concatenate([intra[c] + _nn(qd_b[c], states[c]) for c in chunks], axis=0)
    mean_sq = _head_sum(y * y, _head_sum_matrix().astype(BF16)) * (1.0 / HDIM)
    o_ref[0] = (y * lax.rsqrt(mean_sq + NORM_EPS) * _silu(gate_ref[0])).astype(BF16)


def _ret(p3, tables, tt=512):
    b, t, _ = p3.shape
    base = 3 * 4
    cos2, sin2, di, qd, kd, cd = tables
    di = di.reshape(HEADS * RET_CHUNK, RET_CHUNK)
    blk = lambda c: pl.BlockSpec((1, tt, GROUP), lambda j, i, c=c: (i, j, c))
    full2 = lambda a: pl.BlockSpec(a.shape, lambda j, i: (0, 0))
    return pl.pallas_call(
        functools.partial(_ret_kernel, tt=tt),
        out_shape=jax.ShapeDtypeStruct((b, t, GROUP), BF16),
        grid=(t // tt, b),
        in_specs=[blk(base), blk(base + 1), blk(base + 2), blk(base + 3),
                  pl.BlockSpec((tt, GROUP), lambda j, i: (j, 0)),
                  pl.BlockSpec((tt, GROUP), lambda j, i: (j, 0)),
                  full2(di), full2(qd), full2(kd), full2(cd)],
        out_specs=pl.BlockSpec((1, tt, GROUP), lambda j, i: (i, j, 0)),
        scratch_shapes=[pltpu.VMEM((b, GROUP, GROUP), F32)],
        compiler_params=pltpu.CompilerParams(
            dimension_semantics=("arbitrary", "arbitrary"), vmem_limit_bytes=VMEM_LIMIT_V7X),
        name="retnet",
    )(p3, p3, p3, p3, cos2, sin2, di, qd, kd, cd)


def _reorder_kernel(w_ref, o_ref):
    w = w_ref[0]
    g3 = 3 * GROUP
    lora = w[:, g3:g3 + 2 * LORA]
    pad = jnp.zeros((w.shape[0], 128 - 2 * LORA), w.dtype)
    o_ref[0] = jnp.concatenate([w[:, :g3], w[:, g3 + 2 * LORA:], lora, pad], axis=1).astype(BF16)


def _reorder_w_in(w, tk=256):
    depth, d, n = w.shape
    return pl.pallas_call(
        _reorder_kernel,
        out_shape=jax.ShapeDtypeStruct((depth, d, P_COLS), BF16),
        grid=(depth, d // tk),
        in_specs=[pl.BlockSpec((1, tk, n), lambda l, i: (l, i, 0))],
        out_specs=pl.BlockSpec((1, tk, P_COLS), lambda l, i: (l, i, 0)),
        compiler_params=pltpu.CompilerParams(
            dimension_semantics=("arbitrary", "arbitrary"), vmem_limit_bytes=VMEM_LIMIT_V7X),
        name="w_in_layout",
    )(w)


def kernel(x, norm_w, w_in, w_out, rwkv_mu, rwkv_w0, rwkv_w2, rwkv_a0, rwkv_a2, rwkv_k_k, rwkv_k_a,
           rwkv_r_k, rwkv_lnx_w, rwkv_lnx_b, conv_w, rel_bias, final_norm_w):
    b, t, d = x.shape
    depth = w_in.shape[0]
    bias_tiles = _moba_bias_tiles(rel_bias)
    ret_tables = _ret_tables(t)
    x2 = x.reshape(b * t, d)
    w_in_bf = _reorder_w_in(w_in)
    w_out_bf = w_out.astype(BF16)
    for l in range(depth):
        p2 = _inproj(x2, norm_w[l], w_in_bf, l)
        p3 = p2.reshape(b, t, P_COLS)
        y_rwkv = _rwkv(p3, rwkv_mu[l], rwkv_w0[l], rwkv_w2[l], rwkv_a0[l], rwkv_a2[l], rwkv_k_k[l],
                       rwkv_k_a[l], rwkv_r_k[l].reshape(GROUP), rwkv_lnx_w[l], rwkv_lnx_b[l])
        y_moba = _moba(p3, bias_tiles, rel_bias)
        y_ret = _ret(p3, ret_tables)
        flat = lambda y: y.reshape(b * t, GROUP)
        x2 = _outproj(x2, p2, flat(y_rwkv), flat(y_moba), flat(y_ret), conv_w[l], w_out_bf, l,
                      final_norm_w, final=(l == depth - 1), seq_len=t)
    return x2.reshape(b, t, d)
```

```python
import functools
import math

import numpy as np
import jax
import jax.numpy as jnp
from jax import lax
from jax.experimental import pallas as pl
from jax.experimental.pallas import tpu as pltpu

F32 = jnp.float32
BF16 = jnp.bfloat16
HIGHEST = lax.Precision.HIGHEST

GROUP = 256
HEADS = 4
HDIM = 64
LORA = 32
DECAY_SCALE = math.exp(-0.5)
LOG2E = 1.0 / math.log(2.0)
LNX_EPS = 64e-5
NORM_EPS = 1e-6
CONV_TAPS = 3
MOBA_BLOCK = 256
MOBA_TOPK = 3
MOBA_QBLOCK = 128
RET_CHUNK = 128
REL_BUCKETS = 32
REL_MAX_DIST = 128
RWKV_CHUNK = 64

P_COLS = 4 * 4 * GROUP + 128
LORA_COL = 4 * 4 * GROUP

VMEM_LIMIT_V7X = 48 * 1024 * 1024


def _nn(a, b, precision=None):
    return lax.dot_general(a, b, (((1,), (0,)), ((), ())), precision=precision,
                           preferred_element_type=F32)


def _nt(a, b, precision=None):
    return lax.dot_general(a, b, (((1,), (1,)), ((), ())), precision=precision,
                           preferred_element_type=F32)


def _tn(a, b, precision=None):
    return lax.dot_general(a, b, (((0,), (0,)), ((), ())), precision=precision,
                           preferred_element_type=F32)


def _split(x):
    hi = x.astype(BF16)
    lo = (x - hi.astype(F32)).astype(BF16)
    return hi, lo


def _dot3(dot, a, b):
    out_axis = 1 if dot is _tn else 0
    m = a[0].shape[out_axis]
    both = dot(jnp.concatenate([a[0], a[1]], axis=out_axis), b[0])
    return both[:m] + both[m:] + dot(a[0], b[1])


def _psl(pair, rows, cols):
    return pair[0][rows, cols], pair[1][rows, cols]


def _pcat(pairs, axis):
    return (jnp.concatenate([p[0] for p in pairs], axis=axis),
            jnp.concatenate([p[1] for p in pairs], axis=axis))


def _head_sum(x, esum_bf16):
    hi, lo = _split(x)
    m = x.shape[0]
    both = _nn(jnp.concatenate([hi, lo], axis=0), esum_bf16)
    return both[:m] + both[m:]


def _iota(shape, dim):
    return lax.broadcasted_iota(jnp.int32, shape, dim)


def _head_sum_matrix():
    r = _iota((GROUP, GROUP), 0) // HDIM
    c = _iota((GROUP, GROUP), 1) // HDIM
    return jnp.where(r == c, 1.0, 0.0).astype(F32)


def _silu(x):
    return x * jax.nn.sigmoid(x)


def _inproj_kernel(x_ref, nw_ref, w_ref, o_ref):
    x = x_ref[...]
    h = x * lax.rsqrt(jnp.mean(x * x, axis=-1, keepdims=True) + NORM_EPS) * nw_ref[...]
    o_ref[...] = _nn(h.astype(BF16), w_ref[0])


def _inproj(x2d, norm_w, w_bf16, layer, tm=512):
    m, d = x2d.shape
    n = w_bf16.shape[2]
    return pl.pallas_call(
        _inproj_kernel,
        out_shape=jax.ShapeDtypeStruct((m, n), F32),
        grid=(m // tm,),
        in_specs=[pl.BlockSpec((tm, d), lambda i: (i, 0)),
                  pl.BlockSpec((1, d), lambda i: (0, 0)),
                  pl.BlockSpec((1, d, n), lambda i: (layer, 0, 0))],
        out_specs=pl.BlockSpec((tm, n), lambda i: (i, 0)),
        compiler_params=pltpu.CompilerParams(
            dimension_semantics=("arbitrary",), vmem_limit_bytes=VMEM_LIMIT_V7X),
        name="inproj",
    )(x2d, norm_w.reshape(1, d), w_bf16)


def _gated_conv(p, cw_ref, tail_ref, first_tile):
    @pl.when(first_tile)
    def _():
        tail_ref[...] = jnp.zeros_like(tail_ref)

    n_rows = p.shape[0]
    u = p[:, GROUP:2 * GROUP] * p[:, 2 * GROUP:3 * GROUP]
    rows = _iota((n_rows, 1), 0)
    u1 = jnp.where(rows == 0, tail_ref[1:2, :], pltpu.roll(u, 1, axis=0))
    u2 = jnp.where(rows == 0, tail_ref[0:1, :],
                   jnp.where(rows == 1, tail_ref[1:2, :], pltpu.roll(u, 2, axis=0)))
    tail_ref[0:2, :] = u[n_rows - 2:n_rows, :]
    y = u2 * cw_ref[0:1, :] + u1 * cw_ref[1:2, :] + u * cw_ref[2:3, :]
    return p[:, 0:GROUP] * y * _silu(p[:, 3 * GROUP:4 * GROUP])


def _outproj_kernel(x_ref, y0_ref, pconv_ref, cw_ref, y2_ref, y3_ref, w_ref, fw_ref, o_ref, tail_ref,
                    *, final, tiles_per_seq):
    y_conv = _gated_conv(pconv_ref[...], cw_ref, tail_ref, pl.program_id(0) % tiles_per_seq == 0)
    acc = x_ref[...]
    for g, y in enumerate((y0_ref[...], y_conv, y2_ref[...], y3_ref[...])):
        acc = acc + _nn(y.astype(BF16), w_ref[0, g * GROUP:(g + 1) * GROUP, :])
    if final:
        acc = acc * lax.rsqrt(jnp.mean(acc * acc, axis=-1, keepdims=True) + NORM_EPS) * fw_ref[...]
    o_ref[...] = acc


def _outproj(x2d, p2d, y_rwkv, y_moba, y_ret, conv_w, w_bf16, layer, final_w, final, seq_len, tm=512):
    m, d = x2d.shape
    assert seq_len % tm == 0
    yspec = pl.BlockSpec((tm, GROUP), lambda i: (i, 0))
    return pl.pallas_call(
        functools.partial(_outproj_kernel, final=final, tiles_per_seq=seq_len // tm),
        out_shape=jax.ShapeDtypeStruct((m, d), F32),
        grid=(m // tm,),
        in_specs=[pl.BlockSpec((tm, d), lambda i: (i, 0)), yspec,
                  pl.BlockSpec((tm, 4 * GROUP), lambda i: (i, 1)),
                  pl.BlockSpec((CONV_TAPS, GROUP), lambda i: (0, 0)),
                  yspec, yspec,
                  pl.BlockSpec((1, 4 * GROUP, d), lambda i: (layer, 0, 0)),
                  pl.BlockSpec((1, d), lambda i: (0, 0))],
        out_specs=pl.BlockSpec((tm, d), lambda i: (i, 0)),
        scratch_shapes=[pltpu.VMEM((8, GROUP), F32)],
        compiler_params=pltpu.CompilerParams(
            dimension_semantics=("arbitrary",), vmem_limit_bytes=VMEM_LIMIT_V7X),
        name="outproj_final" if final else "outproj",
    )(x2d, y_rwkv, p2d, conv_w, y_moba, y_ret, w_bf16, final_w.reshape(1, d))


def _rwkv_kernel(rkv_ref, gate_ref, lora_ref, mu_ref, mul_ref, w0_ref, w2_ref, a0_ref, a2_ref,
                 kk_ref, ka_ref, rk_ref, lnw_ref, lnb_ref, o_ref,
                 state_ref, prev_ref, prevl_ref, *, tt, nseq):
    C = RWKV_CHUNK
    n_rows = nseq * tt
    t_idx = pl.program_id(1)

    @pl.when(t_idx == 0)
    def _():
        state_ref[...] = jnp.zeros_like(state_ref)
        prev_ref[...] = jnp.zeros_like(prev_ref)
        prevl_ref[...] = jnp.zeros_like(prevl_ref)

    p = rkv_ref[...].reshape(n_rows, 3 * GROUP)
    lo = lora_ref[...].reshape(n_rows, 128)
    row = _iota((n_rows, 1), 0)
    p_sh = pltpu.roll(p, 1, axis=0)
    lo_sh = pltpu.roll(lo, 1, axis=0)
    for s in range(nseq):
        p_sh = jnp.where(row == s * tt, prev_ref[s:s + 1, :], p_sh)
        lo_sh = jnp.where(row == s * tt, prevl_ref[s:s + 1, :], lo_sh)
    for s in range(nseq):
        prev_ref[s:s + 1, :] = p[(s + 1) * tt - 1:(s + 1) * tt, :]
        prevl_ref[s:s + 1, :] = lo[(s + 1) * tt - 1:(s + 1) * tt, :]
    p = p + (p_sh - p) * mu_ref[...]
    lo = lo + (lo_sh - lo) * mul_ref[...]

    r = p[:, 0:GROUP]
    k = p[:, GROUP:2 * GROUP]
    v = p[:, 2 * GROUP:3 * GROUP]

    esum = _head_sum_matrix().astype(BF16)
    lw = -DECAY_SCALE * jax.nn.sigmoid(
        w0_ref[...] + _dot3(_nn, _split(jnp.tanh(lo)), _split(w2_ref[...])))
    rate = jax.nn.sigmoid(a0_ref[...] + _dot3(_nn, _split(lo), _split(a2_ref[...])))
    kk = k * kk_ref[...]
    kk = kk / jnp.maximum(jnp.sqrt(_head_sum(kk * kk, esum)), 1e-12)
    k2 = k * (1.0 + (rate - 1.0) * ka_ref[...])
    bonus = _head_sum(r * k2 * rk_ref[...], esum) * v
    bvec = kk * rate
    avec = -kk

    rowmod = row % C
    cum = lw
    for sh in (1, 2, 4, 8, 16, 32):
        cum = cum + jnp.where(rowmod >= sh, pltpu.roll(cum, sh, axis=0), 0.0)
    tot_rows = [cum[c * C + C - 1:c * C + C, :] for c in range(n_rows // C)]
    tot = jnp.concatenate([jnp.broadcast_to(tr, (C, GROUP)) for tr in tot_rows], axis=0)
    e_neg = jnp.exp(-cum)
    e_end = jnp.exp(tot - cum)
    r_t = _split(r * jnp.exp(cum))
    a_t = _split(avec * jnp.exp(cum - lw))
    k_t = _split(k2 * e_neg)
    b_t = _split(bvec * e_neg)
    k_h = _split(k2 * e_end)
    b_h = _split(bvec * e_end)
    v_s = _split(v)

    same_head = (_iota((GROUP, GROUP), 0) // HDIM) == (_iota((GROUP, GROUP), 1) // HDIM)

    def bd(pr):
        return tuple(jnp.where(same_head, jnp.concatenate([part] * HEADS, axis=0), jnp.zeros((), part.dtype))
                     for part in pr)

    t_i = _iota((2 * C, GROUP), 0)
    s_i = _iota((2 * C, GROUP), 1) % C
    tri = ((t_i < C) & (s_i < t_i)) | ((t_i >= C) & (s_i <= t_i - C))
    eye = jnp.where(_iota((C, GROUP), 0) == _iota((C, GROUP), 1) % C, 1.0, 0.0).astype(F32)

    per_seq = tt // C
    chunks = range(nseq * per_seq)
    every = slice(None)
    rows = lambda pr, c: _psl(pr, slice(c * C, (c + 1) * C), every)
    top = lambda pr: _psl(pr, slice(0, C), every)
    bottom = lambda pr: _psl(pr, slice(C, 2 * C), every)

    ar = [_pcat([rows(a_t, c), rows(r_t, c)], 0) for c in chunks]
    bd_v = [bd(rows(v_s, c)) for c in chunks]
    ab_rb = [jnp.where(tri, _dot3(_nt, ar[c], bd(rows(b_t, c))), 0.0) for c in chunks]
    ak_rk = [_split(jnp.where(tri, _dot3(_nt, ar[c], bd(rows(k_t, c))), 0.0)) for c in chunks]
    a_ab = [ab_rb[c][0:C] for c in chunks]
    ab_rb = [_split(x) for x in ab_rb]
    t_row = _iota((C, GROUP), 0)
    s_col = _iota((C, GROUP), 1) % C

    def below_diagonal(n):
        return ((t_row // (2 * n)) == (s_col // (2 * n))) & ((t_row % (2 * n)) >= n) & ((s_col % (2 * n)) < n)

    inv = [eye + jnp.where(below_diagonal(1), a_ab[c], 0.0) for c in chunks]
    n = 2
    while n < C:
        off = below_diagonal(n)
        inv_s = [_split(inv[c]) for c in chunks]
        left = [_split(_dot3(_nn, inv_s[c], bd(_split(jnp.where(off, a_ab[c], 0.0))))) for c in chunks]
        inv = [inv[c] + _dot3(_nn, left[c], bd(inv_s[c])) for c in chunks]
        n *= 2
    inv = [_split(x) for x in inv]
    t_ak = [_split(_dot3(_nn, inv[c], bd(top(ak_rk[c])))) for c in chunks]
    t_a = [_split(_dot3(_nn, inv[c], bd(rows(a_t, c)))) for c in chunks]
    free = [_dot3(_nn, _pcat([t_ak[c], bottom(ak_rk[c])], 0), bd_v[c]) for c in chunks]
    kb_hat = [_pcat([rows(k_h, c), rows(b_h, c)], 0) for c in chunks]
    m_s = [_split(jnp.where(same_head, _dot3(_tn, t_a[c], rows(b_h, c)), 0.0)) for c in chunks]
    g = []
    for c in chunks:
        full = jnp.where(same_head, _dot3(_tn, _pcat([rows(v_s, c), _split(free[c][0:C])], 0), kb_hat[c]), 0.0)
        g.append((full[0:HDIM] + full[HDIM:2 * HDIM]) + (full[2 * HDIM:3 * HDIM] + full[3 * HDIM:4 * HDIM]))
    state = [state_ref[s] for s in range(nseq)]
    from_state = {}
    for step in range(per_seq):
        for s in range(nseq):
            c = s * per_seq + step
            st_s = _split(state[s])
            state[s] = state[s] * jnp.exp(tot_rows[c]) + _dot3(_nn, st_s, m_s[c]) + g[c]
            from_state[c] = _dot3(_nt, _pcat([t_a[c], rows(r_t, c)], 0), bd(st_s))
    for s in range(nseq):
        state_ref[s] = state[s]
    y = jnp.concatenate(
        [free[c][C:2 * C] + from_state[c][C:2 * C]
         + _dot3(_nn, bottom(ab_rb[c]), bd(_split(free[c][0:C] + from_state[c][0:C]))) for c in chunks], axis=0)
    mean = _head_sum(y, esum) * (1.0 / HDIM)
    yc = y - mean
    var = _head_sum(yc * yc, esum) * (1.0 / HDIM)
    y = yc * lax.rsqrt(var + LNX_EPS) * lnw_ref[...] + lnb_ref[...] + bonus
    o_ref[...] = (y * _silu(gate_ref[...].reshape(n_rows, GROUP))).astype(BF16).reshape(nseq, tt, GROUP)


def _rwkv(p3, mu, w0, w2, a0, a2, k_k, k_a, r_k, lnx_w, lnx_b, tt=256, nseq=2):
    b, t, _ = p3.shape
    assert b % nseq == 0 and t % tt == 0 and nseq <= 8
    mu_main = mu[:3 * GROUP].reshape(1, 3 * GROUP)
    mu_lora = jnp.concatenate([mu[3 * GROUP:], jnp.zeros((128 - 2 * LORA,), F32)]).reshape(1, 128)
    w2p = jnp.zeros((128, GROUP), F32).at[0:LORA].set(w2)
    a2p = jnp.zeros((128, GROUP), F32).at[LORA:2 * LORA].set(a2)
    row = lambda a: a.reshape(1, GROUP)
    vec = pl.BlockSpec((1, GROUP), lambda i, j: (0, 0))
    return pl.pallas_call(
        functools.partial(_rwkv_kernel, tt=tt, nseq=nseq),
        out_shape=jax.ShapeDtypeStruct((b, t, GROUP), BF16),
        grid=(b // nseq, t // tt),
        in_specs=[pl.BlockSpec((nseq, tt, 3 * GROUP), lambda i, j: (i, j, 0)),
                  pl.BlockSpec((nseq, tt, GROUP), lambda i, j: (i, j, 3)),
                  pl.BlockSpec((nseq, tt, 128), lambda i, j: (i, j, LORA_COL // 128)),
                  pl.BlockSpec((1, 3 * GROUP), lambda i, j: (0, 0)),
                  pl.BlockSpec((1, 128), lambda i, j: (0, 0)),
                  vec,
                  pl.BlockSpec((128, GROUP), lambda i, j: (0, 0)),
                  vec,
                  pl.BlockSpec((128, GROUP), lambda i, j: (0, 0)),
                  vec, vec, vec, vec, vec],
        out_specs=pl.BlockSpec((nseq, tt, GROUP), lambda i, j: (i, j, 0)),
        scratch_shapes=[pltpu.VMEM((nseq, HDIM, GROUP), F32),
                        pltpu.VMEM((8, 3 * GROUP), F32),
                        pltpu.VMEM((8, 128), F32)],
        compiler_params=pltpu.CompilerParams(
            dimension_semantics=("arbitrary", "arbitrary"), vmem_limit_bytes=VMEM_LIMIT_V7X),
        name="rwkv7",
    )(p3, p3, p3, mu_main, mu_lora, row(w0), w2p, row(a0), a2p,
      row(k_k), row(k_a), row(r_k), row(lnx_w), row(lnx_b))


def _t5_bucket_np(dist):
    max_exact = REL_BUCKETS // 2
    d_f = np.maximum(dist, 1).astype(np.float32)
    large = max_exact + (np.log(d_f / np.float32(max_exact)) / np.float32(math.log(REL_MAX_DIST / max_exact))
                         * np.float32(REL_BUCKETS - max_exact)).astype(np.int32)
    large = np.minimum(large, REL_BUCKETS - 1)
    return np.where(dist < max_exact, dist, large).astype(np.int32)


MASKED_BUCKET = REL_BUCKETS


def _moba_bucket_table():
    keys = np.arange(MOBA_BLOCK)[:, None]
    queries = np.arange(MOBA_BLOCK)[None, :]
    prev = _t5_bucket_np(queries + MOBA_BLOCK - keys)
    own = np.where(keys <= queries, _t5_bucket_np(np.maximum(queries - keys, 0)), MASKED_BUCKET)
    return np.stack([prev, own]).astype(np.int32)


def _bias_kernel(idx_ref, rb_ref, o_ref):
    h = pl.program_id(0)
    idx = idx_ref[...]
    acc = jnp.full(idx.shape, -jnp.inf, F32)
    for bkt in range(REL_BUCKETS):
        acc = jnp.where(idx == bkt, rb_ref[bkt, h], acc)
    o_ref[0] = acc * LOG2E


def _moba_bias_tiles(rel_bias):
    idx = jnp.asarray(_moba_bucket_table())
    shp = (2, MOBA_BLOCK, MOBA_BLOCK)
    return pl.pallas_call(
        _bias_kernel,
        out_shape=jax.ShapeDtypeStruct((HEADS,) + shp, F32),
        grid=(HEADS,),
        in_specs=[pl.BlockSpec(shp, lambda h: (0, 0, 0)),
                  pl.BlockSpec(memory_space=pltpu.SMEM)],
        out_specs=pl.BlockSpec((1,) + shp, lambda h: (h, 0, 0, 0)),
        name="moba_bias",
    )(idx, rel_bias)


def _moba_kernel(q_ref, k_ref, v_ref, gate_ref, bias_ref, rb_ref, o_ref,
                 kmean_ref, kbf_ref, vt_ref, sel_ref, m_ref, l_ref, acc_ref, sc_ref, *, nb):
    BLK = MOBA_BLOCK
    ib = pl.program_id(1)
    heads = range(HEADS)
    rows_of = lambda h: slice(h * HDIM, (h + 1) * HDIM)
    neg_inf = -jnp.inf
    not_selected = -1e30

    @pl.when(ib == 0)
    def _():
        kmean_ref[...] = jnp.zeros_like(kmean_ref)
        for n in range(nb):
            blk = slice(n * BLK, (n + 1) * BLK)
            kblk = k_ref[0, blk, :]
            kmean_ref[n:n + 1, :] = jnp.mean(kblk, axis=0, keepdims=True)
            kbf_ref[n] = kblk.astype(BF16)
            vt_ref[n, 0:GROUP, :] = v_ref[0, blk, :].T.astype(BF16)
            vt_ref[n, GROUP:GROUP + 16, :] = jnp.ones((16, BLK), BF16)

    q = q_ref[0] * (HDIM ** -0.5 * LOG2E)
    q_bf = q.astype(BF16)
    q_s = _split(q)
    lane_head = _iota((1, GROUP), 1) // HDIM
    q_heads = [jnp.where(lane_head == h, q_bf, jnp.zeros_like(q_bf)) for h in heads]

    km = kmean_ref[...]
    gates = [_dot3(_nt, _split(jnp.where(lane_head == h, km, 0.0)), q_s) for h in heads]
    blk_id = _iota((16, BLK), 0)
    for h in heads:
        g = jnp.where(blk_id < ib, gates[h], neg_inf)
        sel = jnp.zeros((16, BLK), F32)
        for _ in range(min(MOBA_TOPK, nb)):
            m = jnp.max(g, axis=0, keepdims=True)
            hit = (g == m) & (m > neg_inf)
            first = jnp.min(jnp.where(hit, blk_id, 16), axis=0, keepdims=True)
            pick = blk_id == first
            sel = jnp.where(pick, 1.0, sel)
            g = jnp.where(pick, neg_inf, g)
        sel_ref[h] = sel

    def raw_scores(n, h):
        return _nt(kbf_ref[n], q_heads[h])

    def attend(blocks, first, scores_slot=None, prefetch=None):
        if scores_slot is None:
            scores = [[raw_scores(n, h) for n, _, _ in blocks] for h in heads]
        else:
            scores = [[sc_ref[scores_slot, j, h] for j in range(len(blocks))] for h in heads]
        if prefetch is not None:
            slot, nxt = prefetch
            for j, n in enumerate(nxt):
                for h in heads:
                    sc_ref[slot, j, h] = raw_scores(n, h)
        probs, alphas = [], []
        for h in heads:
            ss, tops = [], []
            for s, (_, tile, row) in zip(scores[h], blocks):
                s = s if tile is None else s + tile[h]
                top = jnp.max(s, axis=0, keepdims=True)
                ss.append(s)
                tops.append(top if row is None else top + row[h])
            m_new = functools.reduce(jnp.maximum, tops)
            if not first:
                m_old = m_ref[h]
                m_new = jnp.maximum(m_old, m_new)
                alphas.append(jnp.exp2(m_old - m_new))
            m_ref[h] = m_new
            probs.append([jnp.exp2(s + ((-m_new) if row is None else (row[h] - m_new))).astype(BF16)
                          for s, (_, _, row) in zip(ss, blocks)])
        for h in heads:
            pv = None
            for p, (n, _, _) in zip(probs[h], blocks):
                lhs = jnp.concatenate([vt_ref[n, rows_of(h), :], vt_ref[n, GROUP:GROUP + 16, :]], axis=0)
                part = _nn(lhs, p)
                pv = part if pv is None else pv + part
            if first:
                acc_ref[rows_of(h), :] = pv[0:HDIM]
                l_ref[h] = pv[HDIM:HDIM + 1]
            else:
                acc_ref[rows_of(h), :] = alphas[h] * acc_ref[rows_of(h), :] + pv[0:HDIM]
                l_ref[h] = alphas[h] * l_ref[h] + pv[HDIM:HDIM + 1]

    def selected_row(h, n, valid):
        row = jnp.where(sel_ref[h, pl.ds(n, 1), :] > 0.5, 0.0, not_selected)
        return jnp.where(valid, row, not_selected)

    prev = jnp.maximum(ib - 1, 0)
    far_pairs = [[n] if n + 1 >= nb - 2 else [n, n + 1] for n in range(0, nb - 2, 2)]
    attend([(ib, [bias_ref[h, 1] for h in heads], None),
            (prev, [bias_ref[h, 0] for h in heads], [selected_row(h, prev, ib >= 1) for h in heads])],
           first=True)

    n_far = ib - 1
    for k, pair in enumerate(far_pairs):
        @pl.when(pair[0] < n_far)
        def _(k=k, pair=pair):
            attend([(j, None, [(selected_row(h, j, j < n_far) + rb_ref[REL_BUCKETS - 1, h]) * LOG2E for h in heads])
                    for j in pair], first=False, scores_slot=(k % 2) if k > 0 else None,
                   prefetch=((k + 1) % 2, far_pairs[k + 1]) if k + 1 < len(far_pairs) else None)

    for h in heads:
        acc_ref[rows_of(h), :] = acc_ref[rows_of(h), :] / l_ref[h]
    o_ref[0] = (acc_ref[...].T * _silu(gate_ref[0])).astype(BF16)


def _moba(p3, bias_tiles, rel_bias):
    b, t, _ = p3.shape
    BLK = MOBA_BLOCK
    nb = t // BLK
    assert 2 <= nb <= 16 and t % BLK == 0
    base = 2 * 4
    tile = lambda c: pl.BlockSpec((1, BLK, GROUP), lambda i, j, c=c: (i, j, c))
    seq = lambda c: pl.BlockSpec((1, t, GROUP), lambda i, j, c=c: (i, 0, c))
    return pl.pallas_call(
        functools.partial(_moba_kernel, nb=nb),
        out_shape=jax.ShapeDtypeStruct((b, t, GROUP), BF16),
        grid=(b, nb),
        in_specs=[tile(base), seq(base + 1), seq(base + 2), tile(base + 3),
                  pl.BlockSpec(bias_tiles.shape, lambda i, j: (0, 0, 0, 0)),
                  pl.BlockSpec(memory_space=pltpu.SMEM)],
        out_specs=pl.BlockSpec((1, BLK, GROUP), lambda i, j: (i, j, 0)),
        scratch_shapes=[pltpu.VMEM((16, GROUP), F32),
                        pltpu.VMEM((nb, BLK, GROUP), BF16),
                        pltpu.VMEM((nb, GROUP + 16, BLK), BF16),
                        pltpu.VMEM((HEADS, 16, BLK), F32),
                        pltpu.VMEM((HEADS, 1, BLK), F32),
                        pltpu.VMEM((HEADS, 1, BLK), F32),
                        pltpu.VMEM((GROUP, BLK), F32),
                        pltpu.VMEM((2, 2, HEADS, BLK, BLK), F32)],
        compiler_params=pltpu.CompilerParams(
            dimension_semantics=("arbitrary", "arbitrary"), vmem_limit_bytes=VMEM_LIMIT_V7X),
        name="moba",
    )(p3, p3, p3, p3, bias_tiles, rel_bias)


def _ret_tables(t):
    half = HDIM // 2
    theta = 1.0 / (10000.0 ** np.linspace(0.0, 1.0, half))
    pos = np.arange(t, dtype=np.float64)
    ang = pos[:, None] * theta[None, :]
    sin, cos = np.sin(ang), np.cos(ang)
    cos2 = np.tile(np.repeat(cos, 2, axis=1), (1, HEADS))
    sin2 = np.tile(np.stack([-sin, sin], axis=-1).reshape(t, HDIM), (1, HEADS))
    log_gamma = np.log(1.0 - 2.0 ** (-5.0 - np.arange(HEADS, dtype=np.float64)))
    idx = np.arange(RET_CHUNK, dtype=np.float64)
    diff = idx[:, None] - idx[None, :]
    decay_intra = np.where(diff >= 0, np.exp(log_gamma[:, None, None] * np.maximum(diff, 0.0)), 0.0)
    q_decay = np.exp(log_gamma[:, None] * (idx + 1.0))
    k_decay = np.exp(log_gamma[:, None] * (RET_CHUNK - 1.0 - idx))
    chunk_decay = np.exp(log_gamma * RET_CHUNK)
    lanes = lambda a: np.repeat(a.T, HDIM, axis=1)
    cd = np.repeat(chunk_decay, HDIM).reshape(1, GROUP)
    return tuple(jnp.asarray(a, F32) for a in (cos2, sin2, decay_intra, lanes(q_decay), lanes(k_decay), cd))


def _ret_kernel(q_ref, k_ref, v_ref, gate_ref, cos_ref, sin_ref, di_ref, qd_ref, kd_ref, cd_ref,
                o_ref, state_ref, *, tt):
    C = RET_CHUNK
    n_chunks = tt // C
    chunks = range(n_chunks)
    heads = range(HEADS)
    b = pl.program_id(1)

    @pl.when(pl.program_id(0) == 0)
    def _():
        state_ref[b] = jnp.zeros((GROUP, GROUP), F32)

    even = (_iota((tt, GROUP), 1) % 2) == 0

    def rotate(x):
        swapped = jnp.where(even, pltpu.roll(x, GROUP - 1, axis=1), pltpu.roll(x, 1, axis=1))
        return x * cos_ref[...] + swapped * sin_ref[...]

    q = rotate(q_ref[0])
    k = rotate(k_ref[0]) * (HDIM ** -0.5)
    lane_head = _iota((1, GROUP), 1) // HDIM
    same_head = (_iota((GROUP, GROUP), 0) // HDIM) == (_iota((GROUP, GROUP), 1) // HDIM)
    rows = lambda c: slice(c * C, (c + 1) * C)

    q_b = [q[rows(c)].astype(BF16) for c in chunks]
    k_b = [k[rows(c)].astype(BF16) for c in chunks]
    v_b = [v_ref[0, rows(c), :].astype(BF16) for c in chunks]
    kd_b = [(k[rows(c)] * kd_ref[...]).astype(BF16) for c in chunks]
    qd_b = [(q[rows(c)] * qd_ref[...]).astype(BF16) for c in chunks]

    inner = [(_nt(jnp.concatenate([jnp.where(lane_head == h, q_b[c], jnp.zeros_like(q_b[c])) for h in heads],
                                  axis=0), k_b[c]) * di_ref[...]).astype(BF16) for c in chunks]
    kv = [jnp.where(same_head, _tn(kd_b[c], v_b[c]), 0.0) for c in chunks]
    intra = []
    for c in chunks:
        stacked = _nn(inner[c], v_b[c])
        acc = jnp.where(lane_head == 0, stacked[0:C], 0.0)
        for h in range(1, HEADS):
            acc = jnp.where(lane_head == h, stacked[h * C:(h + 1) * C], acc)
        intra.append(acc)
    state = state_ref[b]
    states = []
    for c in chunks:
        states.append(state.astype(BF16))
        state = state * cd_ref[...] + kv[c]
    state_ref[b] = state
    y = jnp.concatenate([intra[c] + _nn(qd_b[c], states[c]) for c in chunks], axis=0)
    mean_sq = _head_sum(y * y, _head_sum_matrix().astype(BF16)) * (1.0 / HDIM)
    o_ref[0] = (y * lax.rsqrt(mean_sq + NORM_EPS) * _silu(gate_ref[0])).astype(BF16)


def _ret(p3, tables, tt=512):
    b, t, _ = p3.shape
    base = 3 * 4
    cos2, sin2, di, qd, kd, cd = tables
    di = di.reshape(HEADS * RET_CHUNK, RET_CHUNK)
    blk = lambda c: pl.BlockSpec((1, tt, GROUP), lambda j, i, c=c: (i, j, c))
    full2 = lambda a: pl.BlockSpec(a.shape, lambda j, i: (0, 0))
    return pl.pallas_call(
        functools.partial(_ret_kernel, tt=tt),
        out_shape=jax.ShapeDtypeStruct((b, t, GROUP), BF16),
        grid=(t // tt, b),
        in_specs=[blk(base), blk(base + 1), blk(base + 2), blk(base + 3),
                  pl.BlockSpec((tt, GROUP), lambda j, i: (j, 0)),
                  pl.BlockSpec((tt, GROUP), lambda j, i: (j, 0)),
                  full2(di), full2(qd), full2(kd), full2(cd)],
        out_specs=pl.BlockSpec((1, tt, GROUP), lambda j, i: (i, j, 0)),
        scratch_shapes=[pltpu.VMEM((b, GROUP, GROUP), F32)],
        compiler_params=pltpu.CompilerParams(
            dimension_semantics=("arbitrary", "arbitrary"), vmem_limit_bytes=VMEM_LIMIT_V7X),
        name="retnet",
    )(p3, p3, p3, p3, cos2, sin2, di, qd, kd, cd)


def _reorder_kernel(w_ref, o_ref):
    w = w_ref[...]
    g3 = 3 * GROUP
    lora = w[:, g3:g3 + 2 * LORA]
    pad = jnp.zeros((w.shape[0], 128 - 2 * LORA), w.dtype)
    o_ref[...] = jnp.concatenate([w[:, :g3], w[:, g3 + 2 * LORA:], lora, pad], axis=1).astype(BF16)


def _reorder_w_in(w, tk=256):
    depth, d, n = w.shape
    out = pl.pallas_call(
        _reorder_kernel,
        out_shape=jax.ShapeDtypeStruct((depth * d, P_COLS), BF16),
        grid=(depth * d // tk,),
        in_specs=[pl.BlockSpec((tk, n), lambda i: (i, 0))],
        out_specs=pl.BlockSpec((tk, P_COLS), lambda i: (i, 0)),
        compiler_params=pltpu.CompilerParams(
            dimension_semantics=("arbitrary",), vmem_limit_bytes=VMEM_LIMIT_V7X),
        name="w_in_layout",
    )(w.reshape(depth * d, n))
    return out.reshape(depth, d, P_COLS)


def kernel(x, norm_w, w_in, w_out, rwkv_mu, rwkv_w0, rwkv_w2, rwkv_a0, rwkv_a2, rwkv_k_k, rwkv_k_a,
           rwkv_r_k, rwkv_lnx_w, rwkv_lnx_b, conv_w, rel_bias, final_norm_w):
    b, t, d = x.shape
    depth = w_in.shape[0]
    bias_tiles = _moba_bias_tiles(rel_bias)
    ret_tables = _ret_tables(t)
    x2 = x.reshape(b * t, d)
    w_in_bf = _reorder_w_in(w_in)
    w_out_bf = w_out.astype(BF16)
    for l in range(depth):
        p2 = _inproj(x2, norm_w[l], w_in_bf, l)
        p3 = p2.reshape(b, t, P_COLS)
        y_rwkv = _rwkv(p3, rwkv_mu[l], rwkv_w0[l], rwkv_w2[l], rwkv_a0[l], rwkv_a2[l], rwkv_k_k[l],
                       rwkv_k_a[l], rwkv_r_k[l].reshape(GROUP), rwkv_lnx_w[l], rwkv_lnx_b[l])
        y_moba = _moba(p3, bias_tiles, rel_bias)
        y_ret = _ret(p3, ret_tables)
        flat = lambda y: y.reshape(b * t, GROUP)
        x2 = _outproj(x2, p2, flat(y_rwkv), flat(y_moba), flat(y_ret), conv_w[l], w_out_bf, l,
                      final_norm_w, final=(l == depth - 1), seq_len=t)
    return x2.reshape(b, t, d)
```

```python
import functools
import math

import numpy as np
import jax
import jax.numpy as jnp
from jax import lax
from jax.experimental import pallas as pl
from jax.experimental.pallas import tpu as pltpu

F32 = jnp.float32
BF16 = jnp.bfloat16
HIGHEST = lax.Precision.HIGHEST

GROUP = 256
HEADS = 4
HDIM = 64
LORA = 32
DECAY_SCALE = math.exp(-0.5)
LOG2E = 1.0 / math.log(2.0)
LNX_EPS = 64e-5
NORM_EPS = 1e-6
CONV_TAPS = 3
MOBA_BLOCK = 256
MOBA_TOPK = 3
MOBA_QBLOCK = 128
RET_CHUNK = 128
REL_BUCKETS = 32
REL_MAX_DIST = 128
RWKV_CHUNK = 64

P_COLS = 4 * 4 * GROUP + 128
LORA_COL = 4 * 4 * GROUP

VMEM_LIMIT_V7X = 48 * 1024 * 1024


def _nn(a, b, precision=None):
    return lax.dot_general(a, b, (((1,), (0,)), ((), ())), precision=precision,
                           preferred_element_type=F32)


def _nt(a, b, precision=None):
    return lax.dot_general(a, b, (((1,), (1,)), ((), ())), precision=precision,
                           preferred_element_type=F32)


def _tn(a, b, precision=None):
    return lax.dot_general(a, b, (((0,), (0,)), ((), ())), precision=precision,
                           preferred_element_type=F32)


def _split(x):
    hi = x.astype(BF16)
    lo = (x - hi.astype(F32)).astype(BF16)
    return hi, lo


def _dot3(dot, a, b):
    out_axis = 1 if dot is _tn else 0
    m = a[0].shape[out_axis]
    both = dot(jnp.concatenate([a[0], a[1]], axis=out_axis), b[0])
    return both[:m] + both[m:] + dot(a[0], b[1])


def _psl(pair, rows, cols):
    return pair[0][rows, cols], pair[1][rows, cols]


def _pcat(pairs, axis):
    return (jnp.concatenate([p[0] for p in pairs], axis=axis),
            jnp.concatenate([p[1] for p in pairs], axis=axis))


def _head_sum(x, esum_bf16):
    hi, lo = _split(x)
    m = x.shape[0]
    both = _nn(jnp.concatenate([hi, lo], axis=0), esum_bf16)
    return both[:m] + both[m:]


def _iota(shape, dim):
    return lax.broadcasted_iota(jnp.int32, shape, dim)


def _head_sum_matrix():
    r = _iota((GROUP, GROUP), 0) // HDIM
    c = _iota((GROUP, GROUP), 1) // HDIM
    return jnp.where(r == c, 1.0, 0.0).astype(F32)


def _silu(x):
    return x * jax.nn.sigmoid(x)


def _inproj_kernel(x_ref, nw_ref, w_ref, o_ref):
    x = x_ref[...]
    h = x * lax.rsqrt(jnp.mean(x * x, axis=-1, keepdims=True) + NORM_EPS) * nw_ref[...]
    o_ref[...] = _nt(h.astype(BF16), w_ref[0])


def _inproj(x2d, norm_w, w_bf16, layer, tm=512):
    m, d = x2d.shape
    n = w_bf16.shape[1]
    return pl.pallas_call(
        _inproj_kernel,
        out_shape=jax.ShapeDtypeStruct((m, n), F32),
        grid=(m // tm,),
        in_specs=[pl.BlockSpec((tm, d), lambda i: (i, 0)),
                  pl.BlockSpec((1, d), lambda i: (0, 0)),
                  pl.BlockSpec((1, n, d), lambda i: (layer, 0, 0))],
        out_specs=pl.BlockSpec((tm, n), lambda i: (i, 0)),
        compiler_params=pltpu.CompilerParams(
            dimension_semantics=("arbitrary",), vmem_limit_bytes=VMEM_LIMIT_V7X),
        name="inproj",
    )(x2d, norm_w.reshape(1, d), w_bf16)


def _gated_conv(p, cw_ref, tail_ref, first_tile):
    @pl.when(first_tile)
    def _():
        tail_ref[...] = jnp.zeros_like(tail_ref)

    n_rows = p.shape[0]
    u = p[:, GROUP:2 * GROUP] * p[:, 2 * GROUP:3 * GROUP]
    rows = _iota((n_rows, 1), 0)
    u1 = jnp.where(rows == 0, tail_ref[1:2, :], pltpu.roll(u, 1, axis=0))
    u2 = jnp.where(rows == 0, tail_ref[0:1, :],
                   jnp.where(rows == 1, tail_ref[1:2, :], pltpu.roll(u, 2, axis=0)))
    tail_ref[0:2, :] = u[n_rows - 2:n_rows, :]
    y = u2 * cw_ref[0:1, :] + u1 * cw_ref[1:2, :] + u * cw_ref[2:3, :]
    return p[:, 0:GROUP] * y * _silu(p[:, 3 * GROUP:4 * GROUP])


def _outproj_kernel(x_ref, y0_ref, pconv_ref, cw_ref, y2_ref, y3_ref, w_ref, fw_ref, o_ref, tail_ref,
                    *, final, tiles_per_seq):
    y_conv = _gated_conv(pconv_ref[...], cw_ref, tail_ref, pl.program_id(0) % tiles_per_seq == 0)
    acc = x_ref[...]
    for g, y in enumerate((y0_ref[...], y_conv, y2_ref[...], y3_ref[...])):
        acc = acc + _nn(y.astype(BF16), w_ref[0, g * GROUP:(g + 1) * GROUP, :])
    if final:
        acc = acc * lax.rsqrt(jnp.mean(acc * acc, axis=-1, keepdims=True) + NORM_EPS) * fw_ref[...]
    o_ref[...] = acc


def _outproj(x2d, p2d, y_rwkv, y_moba, y_ret, conv_w, w_bf16, layer, final_w, final, seq_len, tm=512):
    m, d = x2d.shape
    assert seq_len % tm == 0
    yspec = pl.BlockSpec((tm, GROUP), lambda i: (i, 0))
    return pl.pallas_call(
        functools.partial(_outproj_kernel, final=final, tiles_per_seq=seq_len // tm),
        out_shape=jax.ShapeDtypeStruct((m, d), F32),
        grid=(m // tm,),
        in_specs=[pl.BlockSpec((tm, d), lambda i: (i, 0)), yspec,
                  pl.BlockSpec((tm, 4 * GROUP), lambda i: (i, 1)),
                  pl.BlockSpec((CONV_TAPS, GROUP), lambda i: (0, 0)),
                  yspec, yspec,
                  pl.BlockSpec((1, 4 * GROUP, d), lambda i: (layer, 0, 0)),
                  pl.BlockSpec((1, d), lambda i: (0, 0))],
        out_specs=pl.BlockSpec((tm, d), lambda i: (i, 0)),
        scratch_shapes=[pltpu.VMEM((8, GROUP), F32)],
        compiler_params=pltpu.CompilerParams(
            dimension_semantics=("arbitrary",), vmem_limit_bytes=VMEM_LIMIT_V7X),
        name="outproj_final" if final else "outproj",
    )(x2d, y_rwkv, p2d, conv_w, y_moba, y_ret, w_bf16, final_w.reshape(1, d))


def _rwkv_kernel(rkv_ref, gate_ref, lora_ref, mu_ref, mul_ref, w0_ref, w2_ref, a0_ref, a2_ref,
                 kk_ref, ka_ref, rk_ref, lnw_ref, lnb_ref, o_ref,
                 state_ref, prev_ref, prevl_ref, *, tt, nseq):
    C = RWKV_CHUNK
    n_rows = nseq * tt
    t_idx = pl.program_id(1)

    @pl.when(t_idx == 0)
    def _():
        state_ref[...] = jnp.zeros_like(state_ref)
        prev_ref[...] = jnp.zeros_like(prev_ref)
        prevl_ref[...] = jnp.zeros_like(prevl_ref)

    esum = _head_sum_matrix().astype(BF16)
    per_seq = tt // C
    row = _iota((tt, 1), 0)

    def prepare(s):
        p = rkv_ref[s]
        lo = lora_ref[s]
        p_sh = jnp.where(row == 0, prev_ref[s:s + 1, :], pltpu.roll(p, 1, axis=0))
        lo_sh = jnp.where(row == 0, prevl_ref[s:s + 1, :], pltpu.roll(lo, 1, axis=0))
        prev_ref[s:s + 1, :] = p[tt - 1:tt, :]
        prevl_ref[s:s + 1, :] = lo[tt - 1:tt, :]
        p = p + (p_sh - p) * mu_ref[...]
        lo = lo + (lo_sh - lo) * mul_ref[...]
        r = p[:, 0:GROUP]
        k = p[:, GROUP:2 * GROUP]
        v = p[:, 2 * GROUP:3 * GROUP]
        lw = -DECAY_SCALE * jax.nn.sigmoid(
            w0_ref[...] + _dot3(_nn, _split(jnp.tanh(lo)), _split(w2_ref[...])))
        rate = jax.nn.sigmoid(a0_ref[...] + _dot3(_nn, _split(lo), _split(a2_ref[...])))
        kk = k * kk_ref[...]
        kk = kk / jnp.maximum(jnp.sqrt(_head_sum(kk * kk, esum)), 1e-12)
        k2 = k * (1.0 + (rate - 1.0) * ka_ref[...])
        bonus = _head_sum(r * k2 * rk_ref[...], esum) * v
        bvec = kk * rate
        avec = -kk
        rowmod = row % C
        cum = lw
        for sh in (1, 2, 4, 8, 16, 32):
            cum = cum + jnp.where(rowmod >= sh, pltpu.roll(cum, sh, axis=0), 0.0)
        tot_rows = [cum[c * C + C - 1:c * C + C, :] for c in range(per_seq)]
        tot = jnp.concatenate([jnp.broadcast_to(tr, (C, GROUP)) for tr in tot_rows], axis=0)
        e_neg = jnp.exp(-cum)
        e_end = jnp.exp(tot - cum)
        whole = dict(r_t=r * jnp.exp(cum), a_t=avec * jnp.exp(cum - lw), k_t=k2 * e_neg, b_t=bvec * e_neg,
                     k_h=k2 * e_end, b_h=bvec * e_end, v_s=v)
        pairs = {name: _split(x) for name, x in whole.items()}
        per_chunk = {name: [_psl(pr, slice(c * C, (c + 1) * C), slice(None)) for c in range(per_seq)]
                     for name, pr in pairs.items()}
        return per_chunk, tot_rows, bonus

    same_head = (_iota((GROUP, GROUP), 0) // HDIM) == (_iota((GROUP, GROUP), 1) // HDIM)

    def bd(pr):
        return tuple(jnp.where(same_head, jnp.concatenate([part] * HEADS, axis=0), jnp.zeros((), part.dtype))
                     for part in pr)

    t_i = _iota((2 * C, GROUP), 0)
    s_i = _iota((2 * C, GROUP), 1) % C
    tri = ((t_i < C) & (s_i < t_i)) | ((t_i >= C) & (s_i <= t_i - C))
    eye = jnp.where(_iota((C, GROUP), 0) == _iota((C, GROUP), 1) % C, 1.0, 0.0).astype(F32)

    chunks = range(nseq * per_seq)
    every = slice(None)
    top = lambda pr: _psl(pr, slice(0, C), every)
    bottom = lambda pr: _psl(pr, slice(C, 2 * C), every)

    ops = {name: [] for name in ("r_t", "a_t", "k_t", "b_t", "k_h", "b_h", "v_s")}
    tot_rows, bonus, ab_rb, ak_rk = [], [], [], []
    for s in range(nseq):
        per_chunk, tots, bonus_s = prepare(s)
        for name in ops:
            ops[name] += per_chunk[name]
        tot_rows += tots
        bonus.append(bonus_s)
        for c in range(s * per_seq, (s + 1) * per_seq):
            ar = _pcat([ops["a_t"][c], ops["r_t"][c]], 0)
            ab_rb.append(jnp.where(tri, _dot3(_nt, ar, bd(ops["b_t"][c])), 0.0))
            ak_rk.append(_split(jnp.where(tri, _dot3(_nt, ar, bd(ops["k_t"][c])), 0.0)))
    rows = lambda name, c: ops[name][c]
    bonus = jnp.concatenate(bonus, axis=0)
    bd_v = [bd(rows("v_s", c)) for c in chunks]
    a_ab = [ab_rb[c][0:C] for c in chunks]
    ab_rb = [_split(x) for x in ab_rb]
    t_row = _iota((C, GROUP), 0)
    s_col = _iota((C, GROUP), 1) % C

    def below_diagonal(n):
        return ((t_row // (2 * n)) == (s_col // (2 * n))) & ((t_row % (2 * n)) >= n) & ((s_col % (2 * n)) < n)

    inv = [eye + jnp.where(below_diagonal(1), a_ab[c], 0.0) for c in chunks]
    n = 2
    while n < C:
        off = below_diagonal(n)
        inv_s = [_split(inv[c]) for c in chunks]
        left = [_split(_dot3(_nn, inv_s[c], bd(_split(jnp.where(off, a_ab[c], 0.0))))) for c in chunks]
        inv = [inv[c] + _dot3(_nn, left[c], bd(inv_s[c])) for c in chunks]
        n *= 2
    inv = [_split(x) for x in inv]
    t_ak = [_split(_dot3(_nn, inv[c], bd(top(ak_rk[c])))) for c in chunks]
    t_a = [_split(_dot3(_nn, inv[c], bd(rows("a_t", c)))) for c in chunks]
    free = [_dot3(_nn, _pcat([t_ak[c], bottom(ak_rk[c])], 0), bd_v[c]) for c in chunks]
    kb_hat = [_pcat([rows("k_h", c), rows("b_h", c)], 0) for c in chunks]
    m_s = [_split(jnp.where(same_head, _dot3(_tn, t_a[c], rows("b_h", c)), 0.0)) for c in chunks]
    g = []
    for c in chunks:
        full = jnp.where(same_head, _dot3(_tn, _pcat([rows("v_s", c), _split(free[c][0:C])], 0), kb_hat[c]), 0.0)
        g.append((full[0:HDIM] + full[HDIM:2 * HDIM]) + (full[2 * HDIM:3 * HDIM] + full[3 * HDIM:4 * HDIM]))
    state = [state_ref[s] for s in range(nseq)]
    from_state = {}
    for step in range(per_seq):
        for s in range(nseq):
            c = s * per_seq + step
            st_s = _split(state[s])
            state[s] = state[s] * jnp.exp(tot_rows[c]) + _dot3(_nn, st_s, m_s[c]) + g[c]
            from_state[c] = _dot3(_nt, _pcat([t_a[c], rows("r_t", c)], 0), bd(st_s))
    for s in range(nseq):
        state_ref[s] = state[s]
    y = jnp.concatenate(
        [free[c][C:2 * C] + from_state[c][C:2 * C]
         + _dot3(_nn, bottom(ab_rb[c]), bd(_split(free[c][0:C] + from_state[c][0:C]))) for c in chunks], axis=0)
    mean = _head_sum(y, esum) * (1.0 / HDIM)
    yc = y - mean
    var = _head_sum(yc * yc, esum) * (1.0 / HDIM)
    y = yc * lax.rsqrt(var + LNX_EPS) * lnw_ref[...] + lnb_ref[...] + bonus
    o_ref[...] = (y * _silu(gate_ref[...].reshape(n_rows, GROUP))).astype(BF16).reshape(nseq, tt, GROUP)


def _rwkv(p3, mu, w0, w2, a0, a2, k_k, k_a, r_k, lnx_w, lnx_b, tt=256, nseq=2):
    b, t, _ = p3.shape
    assert b % nseq == 0 and t % tt == 0 and nseq <= 8
    mu_main = mu[:3 * GROUP].reshape(1, 3 * GROUP)
    mu_lora = jnp.concatenate([mu[3 * GROUP:], jnp.zeros((128 - 2 * LORA,), F32)]).reshape(1, 128)
    w2p = jnp.zeros((128, GROUP), F32).at[0:LORA].set(w2)
    a2p = jnp.zeros((128, GROUP), F32).at[LORA:2 * LORA].set(a2)
    row = lambda a: a.reshape(1, GROUP)
    vec = pl.BlockSpec((1, GROUP), lambda i, j: (0, 0))
    return pl.pallas_call(
        functools.partial(_rwkv_kernel, tt=tt, nseq=nseq),
        out_shape=jax.ShapeDtypeStruct((b, t, GROUP), BF16),
        grid=(b // nseq, t // tt),
        in_specs=[pl.BlockSpec((nseq, tt, 3 * GROUP), lambda i, j: (i, j, 0)),
                  pl.BlockSpec((nseq, tt, GROUP), lambda i, j: (i, j, 3)),
                  pl.BlockSpec((nseq, tt, 128), lambda i, j: (i, j, LORA_COL // 128)),
                  pl.BlockSpec((1, 3 * GROUP), lambda i, j: (0, 0)),
                  pl.BlockSpec((1, 128), lambda i, j: (0, 0)),
                  vec,
                  pl.BlockSpec((128, GROUP), lambda i, j: (0, 0)),
                  vec,
                  pl.BlockSpec((128, GROUP), lambda i, j: (0, 0)),
                  vec, vec, vec, vec, vec],
        out_specs=pl.BlockSpec((nseq, tt, GROUP), lambda i, j: (i, j, 0)),
        scratch_shapes=[pltpu.VMEM((nseq, HDIM, GROUP), F32),
                        pltpu.VMEM((8, 3 * GROUP), F32),
                        pltpu.VMEM((8, 128), F32)],
        compiler_params=pltpu.CompilerParams(
            dimension_semantics=("arbitrary", "arbitrary"), vmem_limit_bytes=VMEM_LIMIT_V7X),
        name="rwkv7",
    )(p3, p3, p3, mu_main, mu_lora, row(w0), w2p, row(a0), a2p,
      row(k_k), row(k_a), row(r_k), row(lnx_w), row(lnx_b))


def _t5_bucket_np(dist):
    max_exact = REL_BUCKETS // 2
    d_f = np.maximum(dist, 1).astype(np.float32)
    large = max_exact + (np.log(d_f / np.float32(max_exact)) / np.float32(math.log(REL_MAX_DIST / max_exact))
                         * np.float32(REL_BUCKETS - max_exact)).astype(np.int32)
    large = np.minimum(large, REL_BUCKETS - 1)
    return np.where(dist < max_exact, dist, large).astype(np.int32)


MASKED_BUCKET = REL_BUCKETS


def _moba_bucket_table():
    keys = np.arange(MOBA_BLOCK)[:, None]
    queries = np.arange(MOBA_BLOCK)[None, :]
    prev = _t5_bucket_np(queries + MOBA_BLOCK - keys)
    own = np.where(keys <= queries, _t5_bucket_np(np.maximum(queries - keys, 0)), MASKED_BUCKET)
    return np.stack([prev, own]).astype(np.int32)


def _bias_kernel(idx_ref, rb_ref, o_ref):
    h = pl.program_id(0)
    idx = idx_ref[...]
    acc = jnp.full(idx.shape, -jnp.inf, F32)
    for bkt in range(REL_BUCKETS):
        acc = jnp.where(idx == bkt, rb_ref[bkt, h], acc)
    o_ref[0] = acc * LOG2E


def _moba_bias_tiles(rel_bias):
    idx = jnp.asarray(_moba_bucket_table())
    shp = (2, MOBA_BLOCK, MOBA_BLOCK)
    return pl.pallas_call(
        _bias_kernel,
        out_shape=jax.ShapeDtypeStruct((HEADS,) + shp, F32),
        grid=(HEADS,),
        in_specs=[pl.BlockSpec(shp, lambda h: (0, 0, 0)),
                  pl.BlockSpec(memory_space=pltpu.SMEM)],
        out_specs=pl.BlockSpec((1,) + shp, lambda h: (h, 0, 0, 0)),
        name="moba_bias",
    )(idx, rel_bias)


def _moba_kernel(q_ref, k_ref, v_ref, gate_ref, bias_ref, rb_ref, o_ref,
                 kmean_ref, kbf_ref, vt_ref, sel_ref, m_ref, l_ref, acc_ref, sc_ref, *, nb):
    BLK = MOBA_BLOCK
    ib = pl.program_id(1)
    heads = range(HEADS)
    rows_of = lambda h: slice(h * HDIM, (h + 1) * HDIM)
    neg_inf = -jnp.inf
    not_selected = -1e30

    @pl.when(ib == 0)
    def _():
        kmean_ref[...] = jnp.zeros_like(kmean_ref)
        for n in range(nb):
            blk = slice(n * BLK, (n + 1) * BLK)
            kblk = k_ref[0, blk, :]
            kmean_ref[n:n + 1, :] = jnp.mean(kblk, axis=0, keepdims=True)
            kbf_ref[n] = kblk.astype(BF16)
            vt_ref[n, 0:GROUP, :] = v_ref[0, blk, :].T.astype(BF16)
            vt_ref[n, GROUP:GROUP + 16, :] = jnp.ones((16, BLK), BF16)

    q = q_ref[0] * (HDIM ** -0.5 * LOG2E)
    q_bf = q.astype(BF16)
    q_s = _split(q)
    lane_head = _iota((1, GROUP), 1) // HDIM
    q_heads = [jnp.where(lane_head == h, q_bf, jnp.zeros_like(q_bf)) for h in heads]

    km = kmean_ref[...]
    gates = [_dot3(_nt, _split(jnp.where(lane_head == h, km, 0.0)), q_s) for h in heads]
    blk_id = _iota((16, BLK), 0)
    for h in heads:
        g = jnp.where(blk_id < ib, gates[h], neg_inf)
        sel = jnp.zeros((16, BLK), F32)
        for _ in range(min(MOBA_TOPK, nb)):
            m = jnp.max(g, axis=0, keepdims=True)
            hit = (g == m) & (m > neg_inf)
            first = jnp.min(jnp.where(hit, blk_id, 16), axis=0, keepdims=True)
            pick = blk_id == first
            sel = jnp.where(pick, 1.0, sel)
            g = jnp.where(pick, neg_inf, g)
        sel_ref[h] = sel

    def raw_scores(n, h):
        return _nt(kbf_ref[n], q_heads[h])

    def attend(blocks, first, scores_slot=None, prefetch=None):
        if scores_slot is None:
            scores = [[raw_scores(n, h) for n, _, _ in blocks] for h in heads]
        else:
            scores = [[sc_ref[scores_slot, j, h] for j in range(len(blocks))] for h in heads]
        if prefetch is not None:
            slot, nxt = prefetch
            for j, n in enumerate(nxt):
                for h in heads:
                    sc_ref[slot, j, h] = raw_scores(n, h)
        probs, alphas = [], []
        for h in heads:
            ss, tops = [], []
            for s, (_, tile, row) in zip(scores[h], blocks):
                s = s if tile is None else s + tile[h]
                top = jnp.max(s, axis=0, keepdims=True)
                ss.append(s)
                tops.append(top if row is None else top + row[h])
            m_new = functools.reduce(jnp.maximum, tops)
            if not first:
                m_old = m_ref[h]
                m_new = jnp.maximum(m_old, m_new)
                alphas.append(jnp.exp2(m_old - m_new))
            m_ref[h] = m_new
            probs.append([jnp.exp2(s + ((-m_new) if row is None else (row[h] - m_new))).astype(BF16)
                          for s, (_, _, row) in zip(ss, blocks)])
        for h in heads:
            pv = None
            for p, (n, _, _) in zip(probs[h], blocks):
                lhs = jnp.concatenate([vt_ref[n, rows_of(h), :], vt_ref[n, GROUP:GROUP + 16, :]], axis=0)
                part = _nn(lhs, p)
                pv = part if pv is None else pv + part
            if first:
                acc_ref[rows_of(h), :] = pv[0:HDIM]
                l_ref[h] = pv[HDIM:HDIM + 1]
            else:
                acc_ref[rows_of(h), :] = alphas[h] * acc_ref[rows_of(h), :] + pv[0:HDIM]
                l_ref[h] = alphas[h] * l_ref[h] + pv[HDIM:HDIM + 1]

    def selected_row(h, n, valid):
        row = jnp.where(sel_ref[h, pl.ds(n, 1), :] > 0.5, 0.0, not_selected)
        return jnp.where(valid, row, not_selected)

    prev = jnp.maximum(ib - 1, 0)
    far_pairs = [[n] if n + 1 >= nb - 2 else [n, n + 1] for n in range(0, nb - 2, 2)]
    attend([(ib, [bias_ref[h, 1] for h in heads], None),
            (prev, [bias_ref[h, 0] for h in heads], [selected_row(h, prev, ib >= 1) for h in heads])],
           first=True)

    n_far = ib - 1
    for k, pair in enumerate(far_pairs):
        @pl.when(pair[0] < n_far)
        def _(k=k, pair=pair):
            attend([(j, None, [(selected_row(h, j, j < n_far) + rb_ref[REL_BUCKETS - 1, h]) * LOG2E for h in heads])
                    for j in pair], first=False, scores_slot=(k % 2) if k > 0 else None,
                   prefetch=((k + 1) % 2, far_pairs[k + 1]) if k + 1 < len(far_pairs) else None)

    for h in heads:
        acc_ref[rows_of(h), :] = acc_ref[rows_of(h), :] / l_ref[h]
    o_ref[0] = (acc_ref[...].T * _silu(gate_ref[0])).astype(BF16)


def _moba(p3, bias_tiles, rel_bias):
    b, t, _ = p3.shape
    BLK = MOBA_BLOCK
    nb = t // BLK
    assert 2 <= nb <= 16 and t % BLK == 0
    base = 2 * 4
    tile = lambda c: pl.BlockSpec((1, BLK, GROUP), lambda i, j, c=c: (i, j, c))
    seq = lambda c: pl.BlockSpec((1, t, GROUP), lambda i, j, c=c: (i, 0, c))
    return pl.pallas_call(
        functools.partial(_moba_kernel, nb=nb),
        out_shape=jax.ShapeDtypeStruct((b, t, GROUP), BF16),
        grid=(b, nb),
        in_specs=[tile(base), seq(base + 1), seq(base + 2), tile(base + 3),
                  pl.BlockSpec(bias_tiles.shape, lambda i, j: (0, 0, 0, 0)),
                  pl.BlockSpec(memory_space=pltpu.SMEM)],
        out_specs=pl.BlockSpec((1, BLK, GROUP), lambda i, j: (i, j, 0)),
        scratch_shapes=[pltpu.VMEM((16, GROUP), F32),
                        pltpu.VMEM((nb, BLK, GROUP), BF16),
                        pltpu.VMEM((nb, GROUP + 16, BLK), BF16),
                        pltpu.VMEM((HEADS, 16, BLK), F32),
                        pltpu.VMEM((HEADS, 1, BLK), F32),
                        pltpu.VMEM((HEADS, 1, BLK), F32),
                        pltpu.VMEM((GROUP, BLK), F32),
                        pltpu.VMEM((2, 2, HEADS, BLK, BLK), F32)],
        compiler_params=pltpu.CompilerParams(
            dimension_semantics=("arbitrary", "arbitrary"), vmem_limit_bytes=VMEM_LIMIT_V7X),
        name="moba",
    )(p3, p3, p3, p3, bias_tiles, rel_bias)


def _ret_tables(t):
    half = HDIM // 2
    theta = 1.0 / (10000.0 ** np.linspace(0.0, 1.0, half))
    pos = np.arange(t, dtype=np.float64)
    ang = pos[:, None] * theta[None, :]
    sin, cos = np.sin(ang), np.cos(ang)
    cos2 = np.tile(np.repeat(cos, 2, axis=1), (1, HEADS))
    sin2 = np.tile(np.stack([-sin, sin], axis=-1).reshape(t, HDIM), (1, HEADS))
    log_gamma = np.log(1.0 - 2.0 ** (-5.0 - np.arange(HEADS, dtype=np.float64)))
    idx = np.arange(RET_CHUNK, dtype=np.float64)
    diff = idx[:, None] - idx[None, :]
    decay_intra = np.where(diff >= 0, np.exp(log_gamma[:, None, None] * np.maximum(diff, 0.0)), 0.0)
    q_decay = np.exp(log_gamma[:, None] * (idx + 1.0))
    k_decay = np.exp(log_gamma[:, None] * (RET_CHUNK - 1.0 - idx))
    chunk_decay = np.exp(log_gamma * RET_CHUNK)
    lanes = lambda a: np.repeat(a.T, HDIM, axis=1)
    cd = np.repeat(chunk_decay, HDIM).reshape(1, GROUP)
    return tuple(jnp.asarray(a, F32) for a in (cos2, sin2, decay_intra, lanes(q_decay), lanes(k_decay), cd))


def _ret_kernel(q_ref, k_ref, v_ref, gate_ref, cos_ref, sin_ref, di_ref, qd_ref, kd_ref, cd_ref,
                o_ref, state_ref, *, tt):
    C = RET_CHUNK
    n_chunks = tt // C
    chunks = range(n_chunks)
    heads = range(HEADS)
    b = pl.program_id(1)

    @pl.when(pl.program_id(0) == 0)
    def _():
        state_ref[b] = jnp.zeros((GROUP, GROUP), F32)

    even = (_iota((tt, GROUP), 1) % 2) == 0

    def rotate(x):
        swapped = jnp.where(even, pltpu.roll(x, GROUP - 1, axis=1), pltpu.roll(x, 1, axis=1))
        return x * cos_ref[...] + swapped * sin_ref[...]

    q = rotate(q_ref[0])
    k = rotate(k_ref[0]) * (HDIM ** -0.5)
    lane_head = _iota((1, GROUP), 1) // HDIM
    same_head = (_iota((GROUP, GROUP), 0) // HDIM) == (_iota((GROUP, GROUP), 1) // HDIM)
    rows = lambda c: slice(c * C, (c + 1) * C)

    q_b = [q[rows(c)].astype(BF16) for c in chunks]
    k_b = [k[rows(c)].astype(BF16) for c in chunks]
    v_b = [v_ref[0, rows(c), :].astype(BF16) for c in chunks]
    kd_b = [(k[rows(c)] * kd_ref[...]).astype(BF16) for c in chunks]
    qd_b = [(q[rows(c)] * qd_ref[...]).astype(BF16) for c in chunks]

    inner = [(_nt(jnp.concatenate([jnp.where(lane_head == h, q_b[c], jnp.zeros_like(q_b[c])) for h in heads],
                                  axis=0), k_b[c]) * di_ref[...]).astype(BF16) for c in chunks]
    kv = [jnp.where(same_head, _tn(kd_b[c], v_b[c]), 0.0) for c in chunks]
    intra = []
    for c in chunks:
        stacked = _nn(inner[c], v_b[c])
        acc = jnp.where(lane_head == 0, stacked[0:C], 0.0)
        for h in range(1, HEADS):
            acc = jnp.where(lane_head == h, stacked[h * C:(h + 1) * C], acc)
        intra.append(acc)
    state = state_ref[b]
    states = []
    for c in chunks:
        states.append(state.astype(BF16))
        state = state * cd_ref[...] + kv[c]
    state_ref[b] = state
    y = jnp.concatenate([intra[c] + _nn(qd_b[c], states[c]) for c in chunks], axis=0)
    mean_sq = _head_sum(y * y, _head_sum_matrix().astype(BF16)) * (1.0 / HDIM)
    o_ref[0] = (y * lax.rsqrt(mean_sq + NORM_EPS) * _silu(gate_ref[0])).astype(BF16)


def _ret(p3, tables, tt=512):
    b, t, _ = p3.shape
    base = 3 * 4
    cos2, sin2, di, qd, kd, cd = tables
    di = di.reshape(HEADS * RET_CHUNK, RET_CHUNK)
    blk = lambda c: pl.BlockSpec((1, tt, GROUP), lambda j, i, c=c: (i, j, c))
    full2 = lambda a: pl.BlockSpec(a.shape, lambda j, i: (0, 0))
    return pl.pallas_call(
        functools.partial(_ret_kernel, tt=tt),
        out_shape=jax.ShapeDtypeStruct((b, t, GROUP), BF16),
        grid=(t // tt, b),
        in_specs=[blk(base), blk(base + 1), blk(base + 2), blk(base + 3),
                  pl.BlockSpec((tt, GROUP), lambda j, i: (j, 0)),
                  pl.BlockSpec((tt, GROUP), lambda j, i: (j, 0)),
                  full2(di), full2(qd), full2(kd), full2(cd)],
        out_specs=pl.BlockSpec((1, tt, GROUP), lambda j, i: (i, j, 0)),
        scratch_shapes=[pltpu.VMEM((b, GROUP, GROUP), F32)],
        compiler_params=pltpu.CompilerParams(
            dimension_semantics=("arbitrary", "arbitrary"), vmem_limit_bytes=VMEM_LIMIT_V7X),
        name="retnet",
    )(p3, p3, p3, p3, cos2, sin2, di, qd, kd, cd)


def _reorder_w_in(w):
    wt = jnp.swapaxes(w, 1, 2).astype(BF16)
    g3 = 3 * GROUP
    pad = jnp.zeros((w.shape[0], 128 - 2 * LORA, w.shape[1]), BF16)
    return jnp.concatenate([wt[:, :g3], wt[:, g3 + 2 * LORA:], wt[:, g3:g3 + 2 * LORA], pad], axis=1)


def kernel(x, norm_w, w_in, w_out, rwkv_mu, rwkv_w0, rwkv_w2, rwkv_a0, rwkv_a2, rwkv_k_k, rwkv_k_a,
           rwkv_r_k, rwkv_lnx_w, rwkv_lnx_b, conv_w, rel_bias, final_norm_w):
    b, t, d = x.shape
    depth = w_in.shape[0]
    bias_tiles = _moba_bias_tiles(rel_bias)
    ret_tables = _ret_tables(t)
    x2 = x.reshape(b * t, d)
    w_in_bf = _reorder_w_in(w_in)
    w_out_bf = w_out.astype(BF16)
    for l in range(depth):
        p2 = _inproj(x2, norm_w[l], w_in_bf, l)
        p3 = p2.reshape(b, t, P_COLS)
        y_rwkv = _rwkv(p3, rwkv_mu[l], rwkv_w0[l], rwkv_w2[l], rwkv_a0[l], rwkv_a2[l], rwkv_k_k[l],
                       rwkv_k_a[l], rwkv_r_k[l].reshape(GROUP), rwkv_lnx_w[l], rwkv_lnx_b[l])
        y_moba = _moba(p3, bias_tiles, rel_bias)
        y_ret = _ret(p3, ret_tables)
        flat = lambda y: y.reshape(b * t, GROUP)
        x2 = _outproj(x2, p2, flat(y_rwkv), flat(y_moba), flat(y_ret), conv_w[l], w_out_bf, l,
                      final_norm_w, final=(l == depth - 1), seq_len=t)
    return x2.reshape(b, t, d)
```

```python
import functools
import math

import numpy as np
import jax
import jax.numpy as jnp
from jax import lax
from jax.experimental import pallas as pl
from jax.experimental.pallas import tpu as pltpu

F32 = jnp.float32
BF16 = jnp.bfloat16
HIGHEST = lax.Precision.HIGHEST

GROUP = 256
HEADS = 4
HDIM = 64
LORA = 32
DECAY_SCALE = math.exp(-0.5)
LOG2E = 1.0 / math.log(2.0)
LNX_EPS = 64e-5
NORM_EPS = 1e-6
CONV_TAPS = 3
MOBA_BLOCK = 256
MOBA_TOPK = 3
MOBA_QBLOCK = 128
RET_CHUNK = 128
REL_BUCKETS = 32
REL_MAX_DIST = 128
RWKV_CHUNK = 64

P_COLS = 4 * 4 * GROUP + 128
LORA_COL = 4 * 4 * GROUP

VMEM_LIMIT_V7X = 48 * 1024 * 1024


def _nn(a, b, precision=None):
    return lax.dot_general(a, b, (((1,), (0,)), ((), ())), precision=precision,
                           preferred_element_type=F32)


def _nt(a, b, precision=None):
    return lax.dot_general(a, b, (((1,), (1,)), ((), ())), precision=precision,
                           preferred_element_type=F32)


def _tn(a, b, precision=None):
    return lax.dot_general(a, b, (((0,), (0,)), ((), ())), precision=precision,
                           preferred_element_type=F32)


def _split(x):
    hi = x.astype(BF16)
    lo = (x - hi.astype(F32)).astype(BF16)
    return hi, lo


def _dot3(dot, a, b):
    out_axis = 1 if dot is _tn else 0
    m = a[0].shape[out_axis]
    both = dot(jnp.concatenate([a[0], a[1]], axis=out_axis), b[0])
    return both[:m] + both[m:] + dot(a[0], b[1])


def _psl(pair, rows, cols):
    return pair[0][rows, cols], pair[1][rows, cols]


def _pcat(pairs, axis):
    return (jnp.concatenate([p[0] for p in pairs], axis=axis),
            jnp.concatenate([p[1] for p in pairs], axis=axis))


def _head_sum(x, esum_bf16):
    hi, lo = _split(x)
    m = x.shape[0]
    both = _nn(jnp.concatenate([hi, lo], axis=0), esum_bf16)
    return both[:m] + both[m:]


def _iota(shape, dim):
    return lax.broadcasted_iota(jnp.int32, shape, dim)


def _head_sum_matrix():
    r = _iota((GROUP, GROUP), 0) // HDIM
    c = _iota((GROUP, GROUP), 1) // HDIM
    return jnp.where(r == c, 1.0, 0.0).astype(F32)


def _silu(x):
    return x * jax.nn.sigmoid(x)


def _inproj_kernel(x_ref, nw_ref, w_ref, o_ref):
    x = x_ref[...]
    h = x * lax.rsqrt(jnp.mean(x * x, axis=-1, keepdims=True) + NORM_EPS) * nw_ref[...]
    o_ref[...] = _nt(h.astype(BF16), w_ref[0])


def _inproj(x2d, norm_w, w_bf16, layer, tm=512):
    m, d = x2d.shape
    n = w_bf16.shape[1]
    return pl.pallas_call(
        _inproj_kernel,
        out_shape=jax.ShapeDtypeStruct((m, n), F32),
        grid=(m // tm,),
        in_specs=[pl.BlockSpec((tm, d), lambda i: (i, 0)),
                  pl.BlockSpec((1, d), lambda i: (0, 0)),
                  pl.BlockSpec((1, n, d), lambda i: (layer, 0, 0))],
        out_specs=pl.BlockSpec((tm, n), lambda i: (i, 0)),
        compiler_params=pltpu.CompilerParams(
            dimension_semantics=("arbitrary",), vmem_limit_bytes=VMEM_LIMIT_V7X),
        name="inproj",
    )(x2d, norm_w.reshape(1, d), w_bf16)


def _gated_conv(p, cw_ref, tail_ref, first_tile):
    @pl.when(first_tile)
    def _():
        tail_ref[...] = jnp.zeros_like(tail_ref)

    n_rows = p.shape[0]
    u = p[:, GROUP:2 * GROUP] * p[:, 2 * GROUP:3 * GROUP]
    rows = _iota((n_rows, 1), 0)
    u1 = jnp.where(rows == 0, tail_ref[1:2, :], pltpu.roll(u, 1, axis=0))
    u2 = jnp.where(rows == 0, tail_ref[0:1, :],
                   jnp.where(rows == 1, tail_ref[1:2, :], pltpu.roll(u, 2, axis=0)))
    tail_ref[0:2, :] = u[n_rows - 2:n_rows, :]
    y = u2 * cw_ref[0:1, :] + u1 * cw_ref[1:2, :] + u * cw_ref[2:3, :]
    return p[:, 0:GROUP] * y * _silu(p[:, 3 * GROUP:4 * GROUP])


def _outproj_kernel(x_ref, y0_ref, pconv_ref, cw_ref, y2_ref, y3_ref, w_ref, fw_ref, o_ref, tail_ref,
                    *, final, tiles_per_seq):
    y_conv = _gated_conv(pconv_ref[...], cw_ref, tail_ref, pl.program_id(0) % tiles_per_seq == 0)
    acc = x_ref[...]
    for g, y in enumerate((y0_ref[...], y_conv, y2_ref[...], y3_ref[...])):
        acc = acc + _nn(y.astype(BF16), w_ref[0, g * GROUP:(g + 1) * GROUP, :])
    if final:
        acc = acc * lax.rsqrt(jnp.mean(acc * acc, axis=-1, keepdims=True) + NORM_EPS) * fw_ref[...]
    o_ref[...] = acc


def _outproj(x2d, p2d, y_rwkv, y_moba, y_ret, conv_w, w_bf16, layer, final_w, final, seq_len, tm=1024):
    m, d = x2d.shape
    assert seq_len % tm == 0
    yspec = pl.BlockSpec((tm, GROUP), lambda i: (i, 0))
    return pl.pallas_call(
        functools.partial(_outproj_kernel, final=final, tiles_per_seq=seq_len // tm),
        out_shape=jax.ShapeDtypeStruct((m, d), F32),
        grid=(m // tm,),
        in_specs=[pl.BlockSpec((tm, d), lambda i: (i, 0)), yspec,
                  pl.BlockSpec((tm, 4 * GROUP), lambda i: (i, 1)),
                  pl.BlockSpec((CONV_TAPS, GROUP), lambda i: (0, 0)),
                  yspec, yspec,
                  pl.BlockSpec((1, 4 * GROUP, d), lambda i: (layer, 0, 0)),
                  pl.BlockSpec((1, d), lambda i: (0, 0))],
        out_specs=pl.BlockSpec((tm, d), lambda i: (i, 0)),
        scratch_shapes=[pltpu.VMEM((8, GROUP), F32)],
        compiler_params=pltpu.CompilerParams(
            dimension_semantics=("arbitrary",), vmem_limit_bytes=VMEM_LIMIT_V7X),
        name="outproj_final" if final else "outproj",
    )(x2d, y_rwkv, p2d, conv_w, y_moba, y_ret, w_bf16, final_w.reshape(1, d))


def _rwkv_kernel(rkv_ref, gate_ref, lora_ref, mu_ref, mul_ref, w0_ref, w2_ref, a0_ref, a2_ref,
                 kk_ref, ka_ref, rk_ref, lnw_ref, lnb_ref, o_ref,
                 state_ref, prev_ref, prevl_ref, *, tt, nseq):
    C = RWKV_CHUNK
    n_rows = nseq * tt
    t_idx = pl.program_id(1)

    @pl.when(t_idx == 0)
    def _():
        state_ref[...] = jnp.zeros_like(state_ref)
        prev_ref[...] = jnp.zeros_like(prev_ref)
        prevl_ref[...] = jnp.zeros_like(prevl_ref)

    esum = _head_sum_matrix().astype(BF16)
    per_seq = tt // C
    row = _iota((tt, 1), 0)

    def prepare(s):
        p = rkv_ref[s]
        lo = lora_ref[s]
        p_sh = jnp.where(row == 0, prev_ref[s:s + 1, :], pltpu.roll(p, 1, axis=0))
        lo_sh = jnp.where(row == 0, prevl_ref[s:s + 1, :], pltpu.roll(lo, 1, axis=0))
        prev_ref[s:s + 1, :] = p[tt - 1:tt, :]
        prevl_ref[s:s + 1, :] = lo[tt - 1:tt, :]
        p = p + (p_sh - p) * mu_ref[...]
        lo = lo + (lo_sh - lo) * mul_ref[...]
        r = p[:, 0:GROUP]
        k = p[:, GROUP:2 * GROUP]
        v = p[:, 2 * GROUP:3 * GROUP]
        lw = -DECAY_SCALE * jax.nn.sigmoid(
            w0_ref[...] + _dot3(_nn, _split(jnp.tanh(lo)), _split(w2_ref[...])))
        rate = jax.nn.sigmoid(a0_ref[...] + _dot3(_nn, _split(lo), _split(a2_ref[...])))
        kk = k * kk_ref[...]
        kk = kk / jnp.maximum(jnp.sqrt(_head_sum(kk * kk, esum)), 1e-12)
        k2 = k * (1.0 + (rate - 1.0) * ka_ref[...])
        bonus = _head_sum(r * k2 * rk_ref[...], esum) * v
        bvec = kk * rate
        avec = -kk
        rowmod = row % C
        cum = lw
        for sh in (1, 2, 4, 8, 16, 32):
            cum = cum + jnp.where(rowmod >= sh, pltpu.roll(cum, sh, axis=0), 0.0)
        tot_rows = [cum[c * C + C - 1:c * C + C, :] for c in range(per_seq)]
        tot = jnp.concatenate([jnp.broadcast_to(tr, (C, GROUP)) for tr in tot_rows], axis=0)
        e_neg = jnp.exp(-cum)
        e_end = jnp.exp(tot - cum)
        whole = dict(r_t=r * jnp.exp(cum), a_t=avec * jnp.exp(cum - lw), k_t=k2 * e_neg, b_t=bvec * e_neg,
                     k_h=k2 * e_end, b_h=bvec * e_end, v_s=v)
        pairs = {name: _split(x) for name, x in whole.items()}
        per_chunk = {name: [_psl(pr, slice(c * C, (c + 1) * C), slice(None)) for c in range(per_seq)]
                     for name, pr in pairs.items()}
        return per_chunk, tot_rows, bonus, [v[c * C:(c + 1) * C, :] for c in range(per_seq)]

    same_head = (_iota((GROUP, GROUP), 0) // HDIM) == (_iota((GROUP, GROUP), 1) // HDIM)

    def bd(pr):
        return tuple(jnp.where(same_head, jnp.concatenate([part] * HEADS, axis=0), jnp.zeros((), part.dtype))
                     for part in pr)

    t_i = _iota((2 * C, GROUP), 0)
    s_i = _iota((2 * C, GROUP), 1) % C
    tri = ((t_i < C) & (s_i < t_i)) | ((t_i >= C) & (s_i <= t_i - C))
    eye = jnp.where(_iota((C, GROUP), 0) == _iota((C, GROUP), 1) % C, 1.0, 0.0).astype(F32)

    chunks = range(nseq * per_seq)
    every = slice(None)
    top = lambda pr: _psl(pr, slice(0, C), every)
    bottom = lambda pr: _psl(pr, slice(C, 2 * C), every)

    ops = {name: [] for name in ("r_t", "a_t", "k_t", "b_t", "k_h", "b_h", "v_s")}
    tot_rows, bonus, ab_rb, ak_rk, v_f = [], [], [], [], []
    for s in range(nseq):
        per_chunk, tots, bonus_s, v_chunks = prepare(s)
        for name in ops:
            ops[name] += per_chunk[name]
        tot_rows += tots
        v_f += v_chunks
        bonus.append(bonus_s)
        for c in range(s * per_seq, (s + 1) * per_seq):
            ar = _pcat([ops["a_t"][c], ops["r_t"][c]], 0)
            ab_rb.append(jnp.where(tri, _dot3(_nt, ar, bd(ops["b_t"][c])), 0.0))
            ak_rk.append(_split(jnp.where(tri, _dot3(_nt, ar, bd(ops["k_t"][c])), 0.0)))
    rows = lambda name, c: ops[name][c]
    bonus = jnp.concatenate(bonus, axis=0)
    bd_v = [bd(rows("v_s", c)) for c in chunks]
    a_ab = [ab_rb[c][0:C] for c in chunks]
    ab_rb = [_split(x) for x in ab_rb]
    t_row = _iota((C, GROUP), 0)
    s_col = _iota((C, GROUP), 1) % C

    def below_diagonal(n):
        return ((t_row // (2 * n)) == (s_col // (2 * n))) & ((t_row % (2 * n)) >= n) & ((s_col % (2 * n)) < n)

    inv = [eye + jnp.where(below_diagonal(1), a_ab[c], 0.0) for c in chunks]
    n = 2
    while n < C:
        off = below_diagonal(n)
        inv_s = [_split(inv[c]) for c in chunks]
        left = [_split(_dot3(_nn, inv_s[c], bd(_split(jnp.where(off, a_ab[c], 0.0))))) for c in chunks]
        inv = [inv[c] + _dot3(_nn, left[c], bd(inv_s[c])) for c in chunks]
        n *= 2
    inv = [_split(x) for x in inv]
    t_ak = [_split(_dot3(_nn, inv[c], bd(top(ak_rk[c])))) for c in chunks]
    t_a_f = [_dot3(_nn, inv[c], bd(rows("a_t", c))) for c in chunks]
    t_a = [_split(x) for x in t_a_f]
    free = [_dot3(_nn, _pcat([t_ak[c], bottom(ak_rk[c])], 0), bd_v[c]) for c in chunks]

    def head_transpose(x):
        xt = x.T
        return jnp.concatenate([xt[h * HDIM:(h + 1) * HDIM, :] for h in range(HEADS)], axis=1)

    bd_bh = [bd(rows("b_h", c)) for c in chunks]
    m_s = [bd(_split(_dot3(_nn, _split(head_transpose(t_a_f[c])), bd_bh[c]))) for c in chunks]
    g = [_dot3(_nn, _pcat([_split(head_transpose(v_f[c])), _split(head_transpose(free[c][0:C]))], 1),
               _pcat([bd(rows("k_h", c)), bd_bh[c]], 0)) for c in chunks]
    state = [state_ref[s] for s in range(nseq)]
    from_state = {}
    for step in range(per_seq):
        for s in range(nseq):
            c = s * per_seq + step
            st_s = _split(state[s])
            state[s] = state[s] * jnp.exp(tot_rows[c]) + _dot3(_nn, st_s, m_s[c]) + g[c]
            from_state[c] = _dot3(_nt, _pcat([t_a[c], rows("r_t", c)], 0), bd(st_s))
    for s in range(nseq):
        state_ref[s] = state[s]
    y = jnp.concatenate(
        [free[c][C:2 * C] + from_state[c][C:2 * C]
         + _dot3(_nn, bottom(ab_rb[c]), bd(_split(free[c][0:C] + from_state[c][0:C]))) for c in chunks], axis=0)
    mean = _head_sum(y, esum) * (1.0 / HDIM)
    yc = y - mean
    var = _head_sum(yc * yc, esum) * (1.0 / HDIM)
    y = yc * lax.rsqrt(var + LNX_EPS) * lnw_ref[...] + lnb_ref[...] + bonus
    o_ref[...] = (y * _silu(gate_ref[...].reshape(n_rows, GROUP))).astype(BF16).reshape(nseq, tt, GROUP)


def _rwkv(p3, mu, w0, w2, a0, a2, k_k, k_a, r_k, lnx_w, lnx_b, tt=256, nseq=2):
    b, t, _ = p3.shape
    assert b % nseq == 0 and t % tt == 0 and nseq <= 8
    mu_main = mu[:3 * GROUP].reshape(1, 3 * GROUP)
    mu_lora = jnp.concatenate([mu[3 * GROUP:], jnp.zeros((128 - 2 * LORA,), F32)]).reshape(1, 128)
    w2p = jnp.zeros((128, GROUP), F32).at[0:LORA].set(w2)
    a2p = jnp.zeros((128, GROUP), F32).at[LORA:2 * LORA].set(a2)
    row = lambda a: a.reshape(1, GROUP)
    vec = pl.BlockSpec((1, GROUP), lambda i, j: (0, 0))
    return pl.pallas_call(
        functools.partial(_rwkv_kernel, tt=tt, nseq=nseq),
        out_shape=jax.ShapeDtypeStruct((b, t, GROUP), BF16),
        grid=(b // nseq, t // tt),
        in_specs=[pl.BlockSpec((nseq, tt, 3 * GROUP), lambda i, j: (i, j, 0)),
                  pl.BlockSpec((nseq, tt, GROUP), lambda i, j: (i, j, 3)),
                  pl.BlockSpec((nseq, tt, 128), lambda i, j: (i, j, LORA_COL // 128)),
                  pl.BlockSpec((1, 3 * GROUP), lambda i, j: (0, 0)),
                  pl.BlockSpec((1, 128), lambda i, j: (0, 0)),
                  vec,
                  pl.BlockSpec((128, GROUP), lambda i, j: (0, 0)),
                  vec,
                  pl.BlockSpec((128, GROUP), lambda i, j: (0, 0)),
                  vec, vec, vec, vec, vec],
        out_specs=pl.BlockSpec((nseq, tt, GROUP), lambda i, j: (i, j, 0)),
        scratch_shapes=[pltpu.VMEM((nseq, HDIM, GROUP), F32),
                        pltpu.VMEM((8, 3 * GROUP), F32),
                        pltpu.VMEM((8, 128), F32)],
        compiler_params=pltpu.CompilerParams(
            dimension_semantics=("arbitrary", "arbitrary"), vmem_limit_bytes=VMEM_LIMIT_V7X),
        name="rwkv7",
    )(p3, p3, p3, mu_main, mu_lora, row(w0), w2p, row(a0), a2p,
      row(k_k), row(k_a), row(r_k), row(lnx_w), row(lnx_b))


def _t5_bucket_np(dist):
    max_exact = REL_BUCKETS // 2
    d_f = np.maximum(dist, 1).astype(np.float32)
    large = max_exact + (np.log(d_f / np.float32(max_exact)) / np.float32(math.log(REL_MAX_DIST / max_exact))
                         * np.float32(REL_BUCKETS - max_exact)).astype(np.int32)
    large = np.minimum(large, REL_BUCKETS - 1)
    return np.where(dist < max_exact, dist, large).astype(np.int32)


MASKED_BUCKET = REL_BUCKETS


def _moba_bucket_table():
    keys = np.arange(MOBA_BLOCK)[:, None]
    queries = np.arange(MOBA_BLOCK)[None, :]
    prev = _t5_bucket_np(queries + MOBA_BLOCK - keys)
    own = np.where(keys <= queries, _t5_bucket_np(np.maximum(queries - keys, 0)), MASKED_BUCKET)
    return np.stack([prev, own]).astype(np.int32)


def _bias_kernel(idx_ref, rb_ref, o_ref):
    h = pl.program_id(0)
    idx = idx_ref[...]
    acc = jnp.full(idx.shape, -jnp.inf, F32)
    for bkt in range(REL_BUCKETS):
        acc = jnp.where(idx == bkt, rb_ref[bkt, h], acc)
    o_ref[0] = acc * LOG2E


def _moba_bias_tiles(rel_bias):
    idx = jnp.asarray(_moba_bucket_table())
    shp = (2, MOBA_BLOCK, MOBA_BLOCK)
    return pl.pallas_call(
        _bias_kernel,
        out_shape=jax.ShapeDtypeStruct((HEADS,) + shp, F32),
        grid=(HEADS,),
        in_specs=[pl.BlockSpec(shp, lambda h: (0, 0, 0)),
                  pl.BlockSpec(memory_space=pltpu.SMEM)],
        out_specs=pl.BlockSpec((1,) + shp, lambda h: (h, 0, 0, 0)),
        name="moba_bias",
    )(idx, rel_bias)


def _moba_kernel(q_ref, k_ref, v_ref, gate_ref, bias_ref, rb_ref, o_ref,
                 kmean_ref, kbf_ref, vt_ref, sel_ref, m_ref, l_ref, acc_ref, sc_ref, *, nb):
    BLK = MOBA_BLOCK
    ib = pl.program_id(1)
    heads = range(HEADS)
    rows_of = lambda h: slice(h * HDIM, (h + 1) * HDIM)
    neg_inf = -jnp.inf
    not_selected = -1e30

    @pl.when(ib == 0)
    def _():
        kmean_ref[...] = jnp.zeros_like(kmean_ref)
        for n in range(nb):
            blk = slice(n * BLK, (n + 1) * BLK)
            kblk = k_ref[0, blk, :]
            kmean_ref[n:n + 1, :] = jnp.mean(kblk, axis=0, keepdims=True)
            kbf_ref[n] = kblk.astype(BF16)
            vt_ref[n, 0:GROUP, :] = v_ref[0, blk, :].T.astype(BF16)
            vt_ref[n, GROUP:GROUP + 16, :] = jnp.ones((16, BLK), BF16)

    q = q_ref[0] * (HDIM ** -0.5 * LOG2E)
    q_bf = q.astype(BF16)
    q_s = _split(q)
    lane_head = _iota((1, GROUP), 1) // HDIM
    q_heads = [jnp.where(lane_head == h, q_bf, jnp.zeros_like(q_bf)) for h in heads]

    km = kmean_ref[...]
    gates = _dot3(_nt, _split(jnp.concatenate([jnp.where(lane_head == h, km, 0.0) for h in heads], axis=0)), q_s)
    blk_id = _iota((16, BLK), 0)
    for h in heads:
        g = jnp.where(blk_id < ib, gates[16 * h:16 * (h + 1)], neg_inf)
        ahead = jnp.zeros((16, BLK), F32)
        for m in range(nb):
            g_m = g[m:m + 1, :]
            ahead = ahead + jnp.where((g_m > g) | ((g_m == g) & (blk_id > m)), 1.0, 0.0)
        sel_ref[h] = jnp.where((ahead < MOBA_TOPK) & (g > neg_inf), 1.0, 0.0)

    def raw_scores(n, h):
        return _nt(kbf_ref[n], q_heads[h])

    def attend(blocks, first, scores_slot=None, prefetch=None):
        if scores_slot is None:
            scores = [[raw_scores(n, h) for n, _, _ in blocks] for h in heads]
        else:
            scores = [[sc_ref[scores_slot, j, h] for j in range(len(blocks))] for h in heads]
        if prefetch is not None:
            slot, nxt = prefetch
            for j, n in enumerate(nxt):
                for h in heads:
                    sc_ref[slot, j, h] = raw_scores(n, h)
        probs, alphas = [], []
        for h in heads:
            ss, tops = [], []
            for s, (_, tile, row) in zip(scores[h], blocks):
                s = s if tile is None else s + tile[h]
                top = jnp.max(s, axis=0, keepdims=True)
                ss.append(s)
                tops.append(top if row is None else top + row[h])
            m_new = functools.reduce(jnp.maximum, tops)
            if not first:
                m_old = m_ref[h]
                m_new = jnp.maximum(m_old, m_new)
                alphas.append(jnp.exp2(m_old - m_new))
            m_ref[h] = m_new
            probs.append([jnp.exp2(s + ((-m_new) if row is None else (row[h] - m_new))).astype(BF16)
                          for s, (_, _, row) in zip(ss, blocks)])
        for h in heads:
            pv = None
            for p, (n, _, _) in zip(probs[h], blocks):
                lhs = jnp.concatenate([vt_ref[n, rows_of(h), :], vt_ref[n, GROUP:GROUP + 16, :]], axis=0)
                part = _nn(lhs, p)
                pv = part if pv is None else pv + part
            if first:
                acc_ref[rows_of(h), :] = pv[0:HDIM]
                l_ref[h] = pv[HDIM:HDIM + 1]
            else:
                acc_ref[rows_of(h), :] = alphas[h] * acc_ref[rows_of(h), :] + pv[0:HDIM]
                l_ref[h] = alphas[h] * l_ref[h] + pv[HDIM:HDIM + 1]

    def selected_row(h, n, valid):
        row = jnp.where(sel_ref[h, pl.ds(n, 1), :] > 0.5, 0.0, not_selected)
        return jnp.where(valid, row, not_selected)

    prev = jnp.maximum(ib - 1, 0)
    far_pairs = [[n] if n + 1 >= nb - 2 else [n, n + 1] for n in range(0, nb - 2, 2)]
    attend([(ib, [bias_ref[h, 1] for h in heads], None),
            (prev, [bias_ref[h, 0] for h in heads], [selected_row(h, prev, ib >= 1) for h in heads])],
           first=True)

    n_far = ib - 1
    for k, pair in enumerate(far_pairs):
        @pl.when(pair[0] < n_far)
        def _(k=k, pair=pair):
            attend([(j, None, [(selected_row(h, j, j < n_far) + rb_ref[REL_BUCKETS - 1, h]) * LOG2E for h in heads])
                    for j in pair], first=False, scores_slot=(k % 2) if k > 0 else None,
                   prefetch=((k + 1) % 2, far_pairs[k + 1]) if k + 1 < len(far_pairs) else None)

    for h in heads:
        acc_ref[rows_of(h), :] = acc_ref[rows_of(h), :] / l_ref[h]
    o_ref[0] = (acc_ref[...].T * _silu(gate_ref[0])).astype(BF16)


def _moba(p3, bias_tiles, rel_bias):
    b, t, _ = p3.shape
    BLK = MOBA_BLOCK
    nb = t // BLK
    assert 2 <= nb <= 16 and t % BLK == 0
    base = 2 * 4
    tile = lambda c: pl.BlockSpec((1, BLK, GROUP), lambda i, j, c=c: (i, j, c))
    seq = lambda c: pl.BlockSpec((1, t, GROUP), lambda i, j, c=c: (i, 0, c))
    return pl.pallas_call(
        functools.partial(_moba_kernel, nb=nb),
        out_shape=jax.ShapeDtypeStruct((b, t, GROUP), BF16),
        grid=(b, nb),
        in_specs=[tile(base), seq(base + 1), seq(base + 2), tile(base + 3),
                  pl.BlockSpec(bias_tiles.shape, lambda i, j: (0, 0, 0, 0)),
                  pl.BlockSpec(memory_space=pltpu.SMEM)],
        out_specs=pl.BlockSpec((1, BLK, GROUP), lambda i, j: (i, j, 0)),
        scratch_shapes=[pltpu.VMEM((16, GROUP), F32),
                        pltpu.VMEM((nb, BLK, GROUP), BF16),
                        pltpu.VMEM((nb, GROUP + 16, BLK), BF16),
                        pltpu.VMEM((HEADS, 16, BLK), F32),
                        pltpu.VMEM((HEADS, 1, BLK), F32),
                        pltpu.VMEM((HEADS, 1, BLK), F32),
                        pltpu.VMEM((GROUP, BLK), F32),
                        pltpu.VMEM((2, 2, HEADS, BLK, BLK), F32)],
        compiler_params=pltpu.CompilerParams(
            dimension_semantics=("arbitrary", "arbitrary"), vmem_limit_bytes=VMEM_LIMIT_V7X),
        name="moba",
    )(p3, p3, p3, p3, bias_tiles, rel_bias)


def _ret_tables(t):
    half = HDIM // 2
    theta = 1.0 / (10000.0 ** np.linspace(0.0, 1.0, half))
    pos = np.arange(t, dtype=np.float64)
    ang = pos[:, None] * theta[None, :]
    sin, cos = np.sin(ang), np.cos(ang)
    cos2 = np.tile(np.repeat(cos, 2, axis=1), (1, HEADS))
    sin2 = np.tile(np.stack([-sin, sin], axis=-1).reshape(t, HDIM), (1, HEADS))
    log_gamma = np.log(1.0 - 2.0 ** (-5.0 - np.arange(HEADS, dtype=np.float64)))
    idx = np.arange(RET_CHUNK, dtype=np.float64)
    diff = idx[:, None] - idx[None, :]
    decay_intra = np.where(diff >= 0, np.exp(log_gamma[:, None, None] * np.maximum(diff, 0.0)), 0.0)
    q_decay = np.exp(log_gamma[:, None] * (idx + 1.0))
    k_decay = np.exp(log_gamma[:, None] * (RET_CHUNK - 1.0 - idx))
    chunk_decay = np.exp(log_gamma * RET_CHUNK)
    lanes = lambda a: np.repeat(a.T, HDIM, axis=1)
    cd = np.repeat(chunk_decay, HDIM).reshape(1, GROUP)
    return tuple(jnp.asarray(a, F32) for a in (cos2, sin2, decay_intra, lanes(q_decay), lanes(k_decay), cd))


def _ret_kernel(q_ref, k_ref, v_ref, gate_ref, cos_ref, sin_ref, di_ref, qd_ref, kd_ref, cd_ref,
                o_ref, state_ref, *, tt):
    C = RET_CHUNK
    n_chunks = tt // C
    chunks = range(n_chunks)
    heads = range(HEADS)
    b = pl.program_id(1)

    @pl.when(pl.program_id(0) == 0)
    def _():
        state_ref[b] = jnp.zeros((GROUP, GROUP), F32)

    even = (_iota((tt, GROUP), 1) % 2) == 0

    def rotate(x):
        swapped = jnp.where(even, pltpu.roll(x, GROUP - 1, axis=1), pltpu.roll(x, 1, axis=1))
        return x * cos_ref[...] + swapped * sin_ref[...]

    q = rotate(q_ref[0])
    k = rotate(k_ref[0]) * (HDIM ** -0.5)
    lane_head = _iota((1, GROUP), 1) // HDIM
    same_head = (_iota((GROUP, GROUP), 0) // HDIM) == (_iota((GROUP, GROUP), 1) // HDIM)
    rows = lambda c: slice(c * C, (c + 1) * C)

    q_b = [q[rows(c)].astype(BF16) for c in chunks]
    k_b = [k[rows(c)].astype(BF16) for c in chunks]
    v_b = [v_ref[0, rows(c), :].astype(BF16) for c in chunks]
    kd_b = [(k[rows(c)] * kd_ref[...]).astype(BF16) for c in chunks]
    qd_b = [(q[rows(c)] * qd_ref[...]).astype(BF16) for c in chunks]

    inner = [(_nt(jnp.concatenate([jnp.where(lane_head == h, q_b[c], jnp.zeros_like(q_b[c])) for h in heads],
                                  axis=0), k_b[c]) * di_ref[...]).astype(BF16) for c in chunks]
    kv = [jnp.where(same_head, _tn(kd_b[c], v_b[c]), 0.0) for c in chunks]
    intra = []
    for c in chunks:
        stacked = _nn(inner[c], v_b[c])
        acc = jnp.where(lane_head == 0, stacked[0:C], 0.0)
        for h in range(1, HEADS):
            acc = jnp.where(lane_head == h, stacked[h * C:(h + 1) * C], acc)
        intra.append(acc)
    state = state_ref[b]
    states = []
    for c in chunks:
        states.append(state.astype(BF16))
        state = state * cd_ref[...] + kv[c]
    state_ref[b] = state
    y = jnp.concatenate([intra[c] + _nn(qd_b[c], states[c]) for c in chunks], axis=0)
    mean_sq = _head_sum(y * y, _head_sum_matrix().astype(BF16)) * (1.0 / HDIM)
    o_ref[0] = (y * lax.rsqrt(mean_sq + NORM_EPS) * _silu(gate_ref[0])).astype(BF16)


def _ret(p3, tables, tt=512):
    b, t, _ = p3.shape
    base = 3 * 4
    cos2, sin2, di, qd, kd, cd = tables
    di = di.reshape(HEADS * RET_CHUNK, RET_CHUNK)
    blk = lambda c: pl.BlockSpec((1, tt, GROUP), lambda j, i, c=c: (i, j, c))
    full2 = lambda a: pl.BlockSpec(a.shape, lambda j, i: (0, 0))
    return pl.pallas_call(
        functools.partial(_ret_kernel, tt=tt),
        out_shape=jax.ShapeDtypeStruct((b, t, GROUP), BF16),
        grid=(t // tt, b),
        in_specs=[blk(base), blk(base + 1), blk(base + 2), blk(base + 3),
                  pl.BlockSpec((tt, GROUP), lambda j, i: (j, 0)),
                  pl.BlockSpec((tt, GROUP), lambda j, i: (j, 0)),
                  full2(di), full2(qd), full2(kd), full2(cd)],
        out_specs=pl.BlockSpec((1, tt, GROUP), lambda j, i: (i, j, 0)),
        scratch_shapes=[pltpu.VMEM((b, GROUP, GROUP), F32)],
        compiler_params=pltpu.CompilerParams(
            dimension_semantics=("arbitrary", "arbitrary"), vmem_limit_bytes=VMEM_LIMIT_V7X),
        name="retnet",
    )(p3, p3, p3, p3, cos2, sin2, di, qd, kd, cd)


def _reorder_w_in(w):
    wt = jnp.swapaxes(w, 1, 2).astype(BF16)
    g3 = 3 * GROUP
    pad = jnp.zeros((w.shape[0], 128 - 2 * LORA, w.shape[1]), BF16)
    return jnp.concatenate([wt[:, :g3], wt[:, g3 + 2 * LORA:], wt[:, g3:g3 + 2 * LORA], pad], axis=1)


def kernel(x, norm_w, w_in, w_out, rwkv_mu, rwkv_w0, rwkv_w2, rwkv_a0, rwkv_a2, rwkv_k_k, rwkv_k_a,
           rwkv_r_k, rwkv_lnx_w, rwkv_lnx_b, conv_w, rel_bias, final_norm_w):
    b, t, d = x.shape
    depth = w_in.shape[0]
    bias_tiles = _moba_bias_tiles(rel_bias)
    ret_tables = _ret_tables(t)
    x2 = x.reshape(b * t, d)
    w_in_bf = _reorder_w_in(w_in)
    w_out_bf = w_out.astype(BF16)
    for l in range(depth):
        p2 = _inproj(x2, norm_w[l], w_in_bf, l)
        p3 = p2.reshape(b, t, P_COLS)
        y_rwkv = _rwkv(p3, rwkv_mu[l], rwkv_w0[l], rwkv_w2[l], rwkv_a0[l], rwkv_a2[l], rwkv_k_k[l],
                       rwkv_k_a[l], rwkv_r_k[l].reshape(GROUP), rwkv_lnx_w[l], rwkv_lnx_b[l])
        y_moba = _moba(p3, bias_tiles, rel_bias)
        y_ret = _ret(p3, ret_tables)
        flat = lambda y: y.reshape(b * t, GROUP)
        x2 = _outproj(x2, p2, flat(y_rwkv), flat(y_moba), flat(y_ret), conv_w[l], w_out_bf, l,
                      final_norm_w, final=(l == depth - 1), seq_len=t)
    return x2.reshape(b, t, d)
```

```python
import functools
import math

import numpy as np
import jax
import jax.numpy as jnp
from jax import lax
from jax.experimental import pallas as pl
from jax.experimental.pallas import tpu as pltpu

F32 = jnp.float32
BF16 = jnp.bfloat16
HIGHEST = lax.Precision.HIGHEST

GROUP = 256
HEADS = 4
HDIM = 64
LORA = 32
DECAY_SCALE = math.exp(-0.5)
LOG2E = 1.0 / math.log(2.0)
LNX_EPS = 64e-5
NORM_EPS = 1e-6
CONV_TAPS = 3
MOBA_BLOCK = 256
MOBA_TOPK = 3
MOBA_QBLOCK = 128
RET_CHUNK = 128
REL_BUCKETS = 32
REL_MAX_DIST = 128
RWKV_CHUNK = 64

P_COLS = 4 * 4 * GROUP + 128
LORA_COL = 4 * 4 * GROUP

VMEM_LIMIT_V7X = 48 * 1024 * 1024


def _nn(a, b, precision=None):
    return lax.dot_general(a, b, (((1,), (0,)), ((), ())), precision=precision,
                           preferred_element_type=F32)


def _nt(a, b, precision=None):
    return lax.dot_general(a, b, (((1,), (1,)), ((), ())), precision=precision,
                           preferred_element_type=F32)


def _tn(a, b, precision=None):
    return lax.dot_general(a, b, (((0,), (0,)), ((), ())), precision=precision,
                           preferred_element_type=F32)


def _split(x):
    hi = x.astype(BF16)
    lo = (x - hi.astype(F32)).astype(BF16)
    return hi, lo


def _dot3(dot, a, b):
    out_axis = 1 if dot is _tn else 0
    m = a[0].shape[out_axis]
    both = dot(jnp.concatenate([a[0], a[1]], axis=out_axis), b[0])
    return both[:m] + both[m:] + dot(a[0], b[1])


def _psl(pair, rows, cols):
    return pair[0][rows, cols], pair[1][rows, cols]


def _pcat(pairs, axis):
    return (jnp.concatenate([p[0] for p in pairs], axis=axis),
            jnp.concatenate([p[1] for p in pairs], axis=axis))


def _head_sum(x, esum_bf16):
    hi, lo = _split(x)
    m = x.shape[0]
    both = _nn(jnp.concatenate([hi, lo], axis=0), esum_bf16)
    return both[:m] + both[m:]


def _iota(shape, dim):
    return lax.broadcasted_iota(jnp.int32, shape, dim)


def _head_sum_matrix():
    r = _iota((GROUP, GROUP), 0) // HDIM
    c = _iota((GROUP, GROUP), 1) // HDIM
    return jnp.where(r == c, 1.0, 0.0).astype(F32)


def _silu(x):
    return x * jax.nn.sigmoid(x)


def _inproj_kernel(x_ref, nw_ref, w_ref, o_ref):
    x = x_ref[...]
    h = x * lax.rsqrt(jnp.mean(x * x, axis=-1, keepdims=True) + NORM_EPS) * nw_ref[...]
    o_ref[...] = _nt(h.astype(BF16), w_ref[0])


def _inproj(x2d, norm_w, w_bf16, layer, tm=512):
    m, d = x2d.shape
    n = w_bf16.shape[1]
    return pl.pallas_call(
        _inproj_kernel,
        out_shape=jax.ShapeDtypeStruct((m, n), F32),
        grid=(m // tm,),
        in_specs=[pl.BlockSpec((tm, d), lambda i: (i, 0)),
                  pl.BlockSpec((1, d), lambda i: (0, 0)),
                  pl.BlockSpec((1, n, d), lambda i: (layer, 0, 0))],
        out_specs=pl.BlockSpec((tm, n), lambda i: (i, 0)),
        compiler_params=pltpu.CompilerParams(
            dimension_semantics=("arbitrary",), vmem_limit_bytes=VMEM_LIMIT_V7X),
        name="inproj",
    )(x2d, norm_w.reshape(1, d), w_bf16)


def _gated_conv(p, cw_ref, tail_ref, first_tile):
    @pl.when(first_tile)
    def _():
        tail_ref[...] = jnp.zeros_like(tail_ref)

    n_rows = p.shape[0]
    u = p[:, GROUP:2 * GROUP] * p[:, 2 * GROUP:3 * GROUP]
    rows = _iota((n_rows, 1), 0)
    u1 = jnp.where(rows == 0, tail_ref[1:2, :], pltpu.roll(u, 1, axis=0))
    u2 = jnp.where(rows == 0, tail_ref[0:1, :],
                   jnp.where(rows == 1, tail_ref[1:2, :], pltpu.roll(u, 2, axis=0)))
    tail_ref[0:2, :] = u[n_rows - 2:n_rows, :]
    y = u2 * cw_ref[0:1, :] + u1 * cw_ref[1:2, :] + u * cw_ref[2:3, :]
    return p[:, 0:GROUP] * y * _silu(p[:, 3 * GROUP:4 * GROUP])


def _outproj_kernel(x_ref, y0_ref, pconv_ref, cw_ref, y2_ref, y3_ref, w_ref, fw_ref, o_ref, tail_ref,
                    *, final, tiles_per_seq):
    y_conv = _gated_conv(pconv_ref[...], cw_ref, tail_ref, pl.program_id(0) % tiles_per_seq == 0)
    acc = x_ref[...]
    for g, y in enumerate((y0_ref[...], y_conv, y2_ref[...], y3_ref[...])):
        acc = acc + _nn(y.astype(BF16), w_ref[0, g * GROUP:(g + 1) * GROUP, :])
    if final:
        acc = acc * lax.rsqrt(jnp.mean(acc * acc, axis=-1, keepdims=True) + NORM_EPS) * fw_ref[...]
    o_ref[...] = acc


def _outproj(x2d, p2d, y_rwkv, y_moba, y_ret, conv_w, w_bf16, layer, final_w, final, seq_len, tm=1024):
    m, d = x2d.shape
    assert seq_len % tm == 0
    yspec = pl.BlockSpec((tm, GROUP), lambda i: (i, 0))
    return pl.pallas_call(
        functools.partial(_outproj_kernel, final=final, tiles_per_seq=seq_len // tm),
        out_shape=jax.ShapeDtypeStruct((m, d), F32),
        grid=(m // tm,),
        in_specs=[pl.BlockSpec((tm, d), lambda i: (i, 0)), yspec,
                  pl.BlockSpec((tm, 4 * GROUP), lambda i: (i, 1)),
                  pl.BlockSpec((CONV_TAPS, GROUP), lambda i: (0, 0)),
                  yspec, yspec,
                  pl.BlockSpec((1, 4 * GROUP, d), lambda i: (layer, 0, 0)),
                  pl.BlockSpec((1, d), lambda i: (0, 0))],
        out_specs=pl.BlockSpec((tm, d), lambda i: (i, 0)),
        scratch_shapes=[pltpu.VMEM((8, GROUP), F32)],
        compiler_params=pltpu.CompilerParams(
            dimension_semantics=("arbitrary",), vmem_limit_bytes=VMEM_LIMIT_V7X),
        name="outproj_final" if final else "outproj",
    )(x2d, y_rwkv, p2d, conv_w, y_moba, y_ret, w_bf16, final_w.reshape(1, d))


def _rwkv_kernel(rkv_ref, gate_ref, lora_ref, mu_ref, mul_ref, w0_ref, w2_ref, a0_ref, a2_ref,
                 kk_ref, ka_ref, rk_ref, lnw_ref, lnb_ref, o_ref,
                 state_ref, prev_ref, prevl_ref, *, tt, nseq):
    C = RWKV_CHUNK
    n_rows = nseq * tt
    t_idx = pl.program_id(1)

    @pl.when(t_idx == 0)
    def _():
        state_ref[...] = jnp.zeros_like(state_ref)
        prev_ref[...] = jnp.zeros_like(prev_ref)
        prevl_ref[...] = jnp.zeros_like(prevl_ref)

    esum = _head_sum_matrix().astype(BF16)
    per_seq = tt // C
    row = _iota((tt, 1), 0)

    def prepare(s):
        p = rkv_ref[s]
        lo = lora_ref[s]
        p_sh = jnp.where(row == 0, prev_ref[s:s + 1, :], pltpu.roll(p, 1, axis=0))
        lo_sh = jnp.where(row == 0, prevl_ref[s:s + 1, :], pltpu.roll(lo, 1, axis=0))
        prev_ref[s:s + 1, :] = p[tt - 1:tt, :]
        prevl_ref[s:s + 1, :] = lo[tt - 1:tt, :]
        p = p + (p_sh - p) * mu_ref[...]
        lo = lo + (lo_sh - lo) * mul_ref[...]
        r = p[:, 0:GROUP]
        k = p[:, GROUP:2 * GROUP]
        v = p[:, 2 * GROUP:3 * GROUP]
        lw = -DECAY_SCALE * jax.nn.sigmoid(
            w0_ref[...] + _dot3(_nn, _split(jnp.tanh(lo)), _split(w2_ref[...])))
        rate = jax.nn.sigmoid(a0_ref[...] + _dot3(_nn, _split(lo), _split(a2_ref[...])))
        kk = k * kk_ref[...]
        kk = kk / jnp.maximum(jnp.sqrt(_head_sum(kk * kk, esum)), 1e-12)
        k2 = k * (1.0 + (rate - 1.0) * ka_ref[...])
        bonus = _head_sum(r * k2 * rk_ref[...], esum) * v
        bvec = kk * rate
        avec = -kk
        rowmod = row % C
        cum = lw
        for sh in (1, 2, 4, 8, 16, 32):
            cum = cum + jnp.where(rowmod >= sh, pltpu.roll(cum, sh, axis=0), 0.0)
        tot_rows = [cum[c * C + C - 1:c * C + C, :] for c in range(per_seq)]
        tot = jnp.concatenate([jnp.broadcast_to(tr, (C, GROUP)) for tr in tot_rows], axis=0)
        e_neg = jnp.exp(-cum)
        e_end = jnp.exp(tot - cum)
        whole = dict(r_t=r * jnp.exp(cum), a_t=avec * jnp.exp(cum - lw), k_t=k2 * e_neg, b_t=bvec * e_neg,
                     k_h=k2 * e_end, b_h=bvec * e_end, v_s=v)
        pairs = {name: _split(x) for name, x in whole.items()}
        per_chunk = {name: [_psl(pr, slice(c * C, (c + 1) * C), slice(None)) for c in range(per_seq)]
                     for name, pr in pairs.items()}
        return per_chunk, tot_rows, bonus, [v[c * C:(c + 1) * C, :] for c in range(per_seq)]

    same_head = (_iota((GROUP, GROUP), 0) // HDIM) == (_iota((GROUP, GROUP), 1) // HDIM)

    def bd(pr):
        return tuple(jnp.where(same_head, jnp.concatenate([part] * HEADS, axis=0), jnp.zeros((), part.dtype))
                     for part in pr)

    t_i = _iota((2 * C, GROUP), 0)
    s_i = _iota((2 * C, GROUP), 1) % C
    tri = ((t_i < C) & (s_i < t_i)) | ((t_i >= C) & (s_i <= t_i - C))
    eye = jnp.where(_iota((C, GROUP), 0) == _iota((C, GROUP), 1) % C, 1.0, 0.0).astype(F32)

    chunks = range(nseq * per_seq)
    every = slice(None)
    top = lambda pr: _psl(pr, slice(0, C), every)
    bottom = lambda pr: _psl(pr, slice(C, 2 * C), every)

    ops = {name: [] for name in ("r_t", "a_t", "k_t", "b_t", "k_h", "b_h", "v_s")}
    tot_rows, bonus, ab_rb, ak_rk, v_f = [], [], [], [], []
    for s in range(nseq):
        per_chunk, tots, bonus_s, v_chunks = prepare(s)
        for name in ops:
            ops[name] += per_chunk[name]
        tot_rows += tots
        v_f += v_chunks
        bonus.append(bonus_s)
        for c in range(s * per_seq, (s + 1) * per_seq):
            ar = _pcat([ops["a_t"][c], ops["r_t"][c]], 0)
            ab_rb.append(jnp.where(tri, _dot3(_nt, ar, bd(ops["b_t"][c])), 0.0))
            ak_rk.append(_split(jnp.where(tri, _dot3(_nt, ar, bd(ops["k_t"][c])), 0.0)))
    rows = lambda name, c: ops[name][c]
    bonus = jnp.concatenate(bonus, axis=0)
    bd_v = [bd(rows("v_s", c)) for c in chunks]
    a_ab = [ab_rb[c][0:C] for c in chunks]
    ab_rb = [_split(x) for x in ab_rb]
    t_row = _iota((C, GROUP), 0)
    s_col = _iota((C, GROUP), 1) % C

    def below_diagonal(n):
        return ((t_row // (2 * n)) == (s_col // (2 * n))) & ((t_row % (2 * n)) >= n) & ((s_col % (2 * n)) < n)

    inv = [eye + jnp.where(below_diagonal(1), a_ab[c], 0.0) for c in chunks]
    n = 2
    while n < C:
        off = below_diagonal(n)
        inv_s = [_split(inv[c]) for c in chunks]
        left = [_split(_dot3(_nn, inv_s[c], bd(_split(jnp.where(off, a_ab[c], 0.0))))) for c in chunks]
        inv = [inv[c] + _dot3(_nn, left[c], bd(inv_s[c])) for c in chunks]
        n *= 2
    inv = [_split(x) for x in inv]
    t_ak = [_split(_dot3(_nn, inv[c], bd(top(ak_rk[c])))) for c in chunks]
    t_a_f = [_dot3(_nn, inv[c], bd(rows("a_t", c))) for c in chunks]
    t_a = [_split(x) for x in t_a_f]
    free = [_dot3(_nn, _pcat([t_ak[c], bottom(ak_rk[c])], 0), bd_v[c]) for c in chunks]

    def head_transpose(x):
        xt = x.T
        return jnp.concatenate([xt[h * HDIM:(h + 1) * HDIM, :] for h in range(HEADS)], axis=1)

    bd_bh = [bd(rows("b_h", c)) for c in chunks]
    m_s = [bd(_split(_dot3(_nn, _split(head_transpose(t_a_f[c])), bd_bh[c]))) for c in chunks]
    g = [_dot3(_nn, _pcat([_split(head_transpose(v_f[c])), _split(head_transpose(free[c][0:C]))], 1),
               _pcat([bd(rows("k_h", c)), bd_bh[c]], 0)) for c in chunks]
    state = [state_ref[s] for s in range(nseq)]
    from_state = {}
    for step in range(per_seq):
        for s in range(nseq):
            c = s * per_seq + step
            st_s = _split(state[s])
            state[s] = state[s] * jnp.exp(tot_rows[c]) + _dot3(_nn, st_s, m_s[c]) + g[c]
            from_state[c] = _dot3(_nt, _pcat([t_a[c], rows("r_t", c)], 0), bd(st_s))
    for s in range(nseq):
        state_ref[s] = state[s]
    y = jnp.concatenate(
        [free[c][C:2 * C] + from_state[c][C:2 * C]
         + _dot3(_nn, bottom(ab_rb[c]), bd(_split(free[c][0:C] + from_state[c][0:C]))) for c in chunks], axis=0)
    mean = _head_sum(y, esum) * (1.0 / HDIM)
    yc = y - mean
    var = _head_sum(yc * yc, esum) * (1.0 / HDIM)
    y = yc * lax.rsqrt(var + LNX_EPS) * lnw_ref[...] + lnb_ref[...] + bonus
    o_ref[...] = (y * _silu(gate_ref[...].reshape(n_rows, GROUP))).astype(BF16).reshape(nseq, tt, GROUP)


def _rwkv(p3, mu, w0, w2, a0, a2, k_k, k_a, r_k, lnx_w, lnx_b, tt=256, nseq=4):
    b, t, _ = p3.shape
    assert b % nseq == 0 and t % tt == 0 and nseq <= 8
    mu_main = mu[:3 * GROUP].reshape(1, 3 * GROUP)
    mu_lora = jnp.concatenate([mu[3 * GROUP:], jnp.zeros((128 - 2 * LORA,), F32)]).reshape(1, 128)
    w2p = jnp.zeros((128, GROUP), F32).at[0:LORA].set(w2)
    a2p = jnp.zeros((128, GROUP), F32).at[LORA:2 * LORA].set(a2)
    row = lambda a: a.reshape(1, GROUP)
    vec = pl.BlockSpec((1, GROUP), lambda i, j: (0, 0))
    return pl.pallas_call(
        functools.partial(_rwkv_kernel, tt=tt, nseq=nseq),
        out_shape=jax.ShapeDtypeStruct((b, t, GROUP), BF16),
        grid=(b // nseq, t // tt),
        in_specs=[pl.BlockSpec((nseq, tt, 3 * GROUP), lambda i, j: (i, j, 0)),
                  pl.BlockSpec((nseq, tt, GROUP), lambda i, j: (i, j, 3)),
                  pl.BlockSpec((nseq, tt, 128), lambda i, j: (i, j, LORA_COL // 128)),
                  pl.BlockSpec((1, 3 * GROUP), lambda i, j: (0, 0)),
                  pl.BlockSpec((1, 128), lambda i, j: (0, 0)),
                  vec,
                  pl.BlockSpec((128, GROUP), lambda i, j: (0, 0)),
                  vec,
                  pl.BlockSpec((128, GROUP), lambda i, j: (0, 0)),
                  vec, vec, vec, vec, vec],
        out_specs=pl.BlockSpec((nseq, tt, GROUP), lambda i, j: (i, j, 0)),
        scratch_shapes=[pltpu.VMEM((nseq, HDIM, GROUP), F32),
                        pltpu.VMEM((8, 3 * GROUP), F32),
                        pltpu.VMEM((8, 128), F32)],
        compiler_params=pltpu.CompilerParams(
            dimension_semantics=("arbitrary", "arbitrary"), vmem_limit_bytes=VMEM_LIMIT_V7X),
        name="rwkv7",
    )(p3, p3, p3, mu_main, mu_lora, row(w0), w2p, row(a0), a2p,
      row(k_k), row(k_a), row(r_k), row(lnx_w), row(lnx_b))


def _t5_bucket_np(dist):
    max_exact = REL_BUCKETS // 2
    d_f = np.maximum(dist, 1).astype(np.float32)
    large = max_exact + (np.log(d_f / np.float32(max_exact)) / np.float32(math.log(REL_MAX_DIST / max_exact))
                         * np.float32(REL_BUCKETS - max_exact)).astype(np.int32)
    large = np.minimum(large, REL_BUCKETS - 1)
    return np.where(dist < max_exact, dist, large).astype(np.int32)


MASKED_BUCKET = REL_BUCKETS


def _moba_bucket_table():
    keys = np.arange(MOBA_BLOCK)[:, None]
    queries = np.arange(MOBA_BLOCK)[None, :]
    prev = _t5_bucket_np(queries + MOBA_BLOCK - keys)
    own = np.where(keys <= queries, _t5_bucket_np(np.maximum(queries - keys, 0)), MASKED_BUCKET)
    return np.stack([prev, own]).astype(np.int32)


def _bias_kernel(idx_ref, rb_ref, o_ref):
    h = pl.program_id(0)
    idx = idx_ref[...]
    acc = jnp.full(idx.shape, -jnp.inf, F32)
    for bkt in range(REL_BUCKETS):
        acc = jnp.where(idx == bkt, rb_ref[bkt, h], acc)
    o_ref[0] = acc * LOG2E


def _moba_bias_tiles(rel_bias):
    idx = jnp.asarray(_moba_bucket_table())
    shp = (2, MOBA_BLOCK, MOBA_BLOCK)
    return pl.pallas_call(
        _bias_kernel,
        out_shape=jax.ShapeDtypeStruct((HEADS,) + shp, F32),
        grid=(HEADS,),
        in_specs=[pl.BlockSpec(shp, lambda h: (0, 0, 0)),
                  pl.BlockSpec(memory_space=pltpu.SMEM)],
        out_specs=pl.BlockSpec((1,) + shp, lambda h: (h, 0, 0, 0)),
        name="moba_bias",
    )(idx, rel_bias)


def _moba_kernel(q_ref, k_ref, v_ref, gate_ref, bias_ref, rb_ref, o_ref,
                 kmean_ref, kbf_ref, vt_ref, sel_ref, m_ref, l_ref, acc_ref, sc_ref, *, nb):
    BLK = MOBA_BLOCK
    ib = pl.program_id(1)
    heads = range(HEADS)
    rows_of = lambda h: slice(h * HDIM, (h + 1) * HDIM)
    neg_inf = -jnp.inf
    not_selected = -1e30

    @pl.when(ib == 0)
    def _():
        kmean_ref[...] = jnp.zeros_like(kmean_ref)
        for n in range(nb):
            blk = slice(n * BLK, (n + 1) * BLK)
            kblk = k_ref[0, blk, :]
            kmean_ref[n:n + 1, :] = jnp.mean(kblk, axis=0, keepdims=True)
            kbf_ref[n] = kblk.astype(BF16)
            vt_ref[n, 0:GROUP, :] = v_ref[0, blk, :].T.astype(BF16)
            vt_ref[n, GROUP:GROUP + 16, :] = jnp.ones((16, BLK), BF16)

    q = q_ref[0] * (HDIM ** -0.5 * LOG2E)
    q_bf = q.astype(BF16)
    q_s = _split(q)
    lane_head = _iota((1, GROUP), 1) // HDIM
    q_heads = [jnp.where(lane_head == h, q_bf, jnp.zeros_like(q_bf)) for h in heads]

    km = kmean_ref[...]
    gates = _dot3(_nt, _split(jnp.concatenate([jnp.where(lane_head == h, km, 0.0) for h in heads], axis=0)), q_s)
    blk_id = _iota((16, BLK), 0)
    for h in heads:
        g = jnp.where(blk_id < ib, gates[16 * h:16 * (h + 1)], neg_inf)
        ahead = jnp.zeros((16, BLK), F32)
        for m in range(nb):
            g_m = g[m:m + 1, :]
            ahead = ahead + jnp.where((g_m > g) | ((g_m == g) & (blk_id > m)), 1.0, 0.0)
        sel_ref[h] = jnp.where((ahead < MOBA_TOPK) & (g > neg_inf), 1.0, 0.0)

    def raw_scores(n, h):
        return _nt(kbf_ref[n], q_heads[h])

    def attend(blocks, first, scores_slot=None, prefetch=None):
        if scores_slot is None:
            scores = [[raw_scores(n, h) for n, _, _ in blocks] for h in heads]
        else:
            scores = [[sc_ref[scores_slot, j, h] for j in range(len(blocks))] for h in heads]
        if prefetch is not None:
            slot, nxt = prefetch
            for j, n in enumerate(nxt):
                for h in heads:
                    sc_ref[slot, j, h] = raw_scores(n, h)
        probs, alphas = [], []
        for h in heads:
            ss, tops = [], []
            for s, (_, tile, row) in zip(scores[h], blocks):
                s = s if tile is None else s + tile[h]
                top = jnp.max(s, axis=0, keepdims=True)
                ss.append(s)
                tops.append(top if row is None else top + row[h])
            m_new = functools.reduce(jnp.maximum, tops)
            if not first:
                m_old = m_ref[h]
                m_new = jnp.maximum(m_old, m_new)
                alphas.append(jnp.exp2(m_old - m_new))
            m_ref[h] = m_new
            probs.append([jnp.exp2(s + ((-m_new) if row is None else (row[h] - m_new))).astype(BF16)
                          for s, (_, _, row) in zip(ss, blocks)])
        for h in heads:
            pv = None
            for p, (n, _, _) in zip(probs[h], blocks):
                lhs = jnp.concatenate([vt_ref[n, rows_of(h), :], vt_ref[n, GROUP:GROUP + 16, :]], axis=0)
                part = _nn(lhs, p)
                pv = part if pv is None else pv + part
            if first:
                acc_ref[rows_of(h), :] = pv[0:HDIM]
                l_ref[h] = pv[HDIM:HDIM + 1]
            else:
                acc_ref[rows_of(h), :] = alphas[h] * acc_ref[rows_of(h), :] + pv[0:HDIM]
                l_ref[h] = alphas[h] * l_ref[h] + pv[HDIM:HDIM + 1]

    def selected_row(h, n, valid):
        row = jnp.where(sel_ref[h, pl.ds(n, 1), :] > 0.5, 0.0, not_selected)
        return jnp.where(valid, row, not_selected)

    prev = jnp.maximum(ib - 1, 0)
    far_pairs = [[n] if n + 1 >= nb - 2 else [n, n + 1] for n in range(0, nb - 2, 2)]
    attend([(ib, [bias_ref[h, 1] for h in heads], None),
            (prev, [bias_ref[h, 0] for h in heads], [selected_row(h, prev, ib >= 1) for h in heads])],
           first=True)

    n_far = ib - 1
    for k, pair in enumerate(far_pairs):
        @pl.when(pair[0] < n_far)
        def _(k=k, pair=pair):
            attend([(j, None, [(selected_row(h, j, j < n_far) + rb_ref[REL_BUCKETS - 1, h]) * LOG2E for h in heads])
                    for j in pair], first=False, scores_slot=(k % 2) if k > 0 else None,
                   prefetch=((k + 1) % 2, far_pairs[k + 1]) if k + 1 < len(far_pairs) else None)

    for h in heads:
        acc_ref[rows_of(h), :] = acc_ref[rows_of(h), :] / l_ref[h]
    o_ref[0] = (acc_ref[...].T * _silu(gate_ref[0])).astype(BF16)


def _moba(p3, bias_tiles, rel_bias):
    b, t, _ = p3.shape
    BLK = MOBA_BLOCK
    nb = t // BLK
    assert 2 <= nb <= 16 and t % BLK == 0
    base = 2 * 4
    tile = lambda c: pl.BlockSpec((1, BLK, GROUP), lambda i, j, c=c: (i, j, c))
    seq = lambda c: pl.BlockSpec((1, t, GROUP), lambda i, j, c=c: (i, 0, c))
    return pl.pallas_call(
        functools.partial(_moba_kernel, nb=nb),
        out_shape=jax.ShapeDtypeStruct((b, t, GROUP), BF16),
        grid=(b, nb),
        in_specs=[tile(base), seq(base + 1), seq(base + 2), tile(base + 3),
                  pl.BlockSpec(bias_tiles.shape, lambda i, j: (0, 0, 0, 0)),
                  pl.BlockSpec(memory_space=pltpu.SMEM)],
        out_specs=pl.BlockSpec((1, BLK, GROUP), lambda i, j: (i, j, 0)),
        scratch_shapes=[pltpu.VMEM((16, GROUP), F32),
                        pltpu.VMEM((nb, BLK, GROUP), BF16),
                        pltpu.VMEM((nb, GROUP + 16, BLK), BF16),
                        pltpu.VMEM((HEADS, 16, BLK), F32),
                        pltpu.VMEM((HEADS, 1, BLK), F32),
                        pltpu.VMEM((HEADS, 1, BLK), F32),
                        pltpu.VMEM((GROUP, BLK), F32),
                        pltpu.VMEM((2, 2, HEADS, BLK, BLK), F32)],
        compiler_params=pltpu.CompilerParams(
            dimension_semantics=("arbitrary", "arbitrary"), vmem_limit_bytes=VMEM_LIMIT_V7X),
        name="moba",
    )(p3, p3, p3, p3, bias_tiles, rel_bias)


def _ret_tables(t):
    half = HDIM // 2
    theta = 1.0 / (10000.0 ** np.linspace(0.0, 1.0, half))
    pos = np.arange(t, dtype=np.float64)
    ang = pos[:, None] * theta[None, :]
    sin, cos = np.sin(ang), np.cos(ang)
    cos2 = np.tile(np.repeat(cos, 2, axis=1), (1, HEADS))
    sin2 = np.tile(np.stack([-sin, sin], axis=-1).reshape(t, HDIM), (1, HEADS))
    log_gamma = np.log(1.0 - 2.0 ** (-5.0 - np.arange(HEADS, dtype=np.float64)))
    idx = np.arange(RET_CHUNK, dtype=np.float64)
    diff = idx[:, None] - idx[None, :]
    decay_intra = np.where(diff >= 0, np.exp(log_gamma[:, None, None] * np.maximum(diff, 0.0)), 0.0)
    decay_intra = np.concatenate(list(decay_intra), axis=1)
    q_decay = np.exp(log_gamma[:, None] * (idx + 1.0))
    k_decay = np.exp(log_gamma[:, None] * (RET_CHUNK - 1.0 - idx))
    chunk_decay = np.exp(log_gamma * RET_CHUNK)
    lanes = lambda a: np.repeat(a.T, HDIM, axis=1)
    cd = np.repeat(chunk_decay, HDIM).reshape(1, GROUP)
    return tuple(jnp.asarray(a, F32) for a in (cos2, sin2, decay_intra, lanes(q_decay), lanes(k_decay), cd))


def _ret_kernel(q_ref, k_ref, v_ref, gate_ref, cos_ref, sin_ref, di_ref, qd_ref, kd_ref, cd_ref,
                o_ref, state_ref, *, tt):
    C = RET_CHUNK
    n_chunks = tt // C
    chunks = range(n_chunks)
    heads = range(HEADS)
    b = pl.program_id(1)

    @pl.when(pl.program_id(0) == 0)
    def _():
        state_ref[b] = jnp.zeros((GROUP, GROUP), F32)

    even = (_iota((tt, GROUP), 1) % 2) == 0

    def rotate(x):
        swapped = jnp.where(even, pltpu.roll(x, GROUP - 1, axis=1), pltpu.roll(x, 1, axis=1))
        return x * cos_ref[...] + swapped * sin_ref[...]

    q = rotate(q_ref[0])
    k = rotate(k_ref[0]) * (HDIM ** -0.5)
    lane_head = _iota((1, GROUP), 1) // HDIM
    same_head = (_iota((GROUP, GROUP), 0) // HDIM) == (_iota((GROUP, GROUP), 1) // HDIM)
    rows = lambda c: slice(c * C, (c + 1) * C)

    q_b = [q[rows(c)].astype(BF16) for c in chunks]
    k_b = [k[rows(c)].astype(BF16) for c in chunks]
    v_b = [v_ref[0, rows(c), :].astype(BF16) for c in chunks]
    kd_b = [(k[rows(c)] * kd_ref[...]).astype(BF16) for c in chunks]
    qd_b = [(q[rows(c)] * qd_ref[...]).astype(BF16) for c in chunks]

    row_head_is_lane_head = (_iota((HEADS * C, GROUP), 0) // C) == (_iota((HEADS * C, GROUP), 1) // HDIM)

    def per_head_rows(x):
        return jnp.where(row_head_is_lane_head, jnp.concatenate([x] * HEADS, axis=0), jnp.zeros((), x.dtype))

    inner = [(_nt(q_b[c], per_head_rows(k_b[c])) * di_ref[...]).astype(BF16) for c in chunks]
    kv = [jnp.where(same_head, _tn(kd_b[c], v_b[c]), 0.0) for c in chunks]
    intra = [_nn(inner[c], per_head_rows(v_b[c])) for c in chunks]
    state = state_ref[b]
    states = []
    for c in chunks:
        states.append(state.astype(BF16))
        state = state * cd_ref[...] + kv[c]
    state_ref[b] = state
    y = jnp.concatenate([intra[c] + _nn(qd_b[c], states[c]) for c in chunks], axis=0)
    mean_sq = _head_sum(y * y, _head_sum_matrix().astype(BF16)) * (1.0 / HDIM)
    o_ref[0] = (y * lax.rsqrt(mean_sq + NORM_EPS) * _silu(gate_ref[0])).astype(BF16)


def _ret(p3, tables, tt=512):
    b, t, _ = p3.shape
    base = 3 * 4
    cos2, sin2, di, qd, kd, cd = tables
    blk = lambda c: pl.BlockSpec((1, tt, GROUP), lambda j, i, c=c: (i, j, c))
    full2 = lambda a: pl.BlockSpec(a.shape, lambda j, i: (0, 0))
    return pl.pallas_call(
        functools.partial(_ret_kernel, tt=tt),
        out_shape=jax.ShapeDtypeStruct((b, t, GROUP), BF16),
        grid=(t // tt, b),
        in_specs=[blk(base), blk(base + 1), blk(base + 2), blk(base + 3),
                  pl.BlockSpec((tt, GROUP), lambda j, i: (j, 0)),
                  pl.BlockSpec((tt, GROUP), lambda j, i: (j, 0)),
                  full2(di), full2(qd), full2(kd), full2(cd)],
        out_specs=pl.BlockSpec((1, tt, GROUP), lambda j, i: (i, j, 0)),
        scratch_shapes=[pltpu.VMEM((b, GROUP, GROUP), F32)],
        compiler_params=pltpu.CompilerParams(
            dimension_semantics=("arbitrary", "arbitrary"), vmem_limit_bytes=VMEM_LIMIT_V7X),
        name="retnet",
    )(p3, p3, p3, p3, cos2, sin2, di, qd, kd, cd)


def _reorder_w_in(w):
    wt = jnp.swapaxes(w, 1, 2).astype(BF16)
    g3 = 3 * GROUP
    pad = jnp.zeros((w.shape[0], 128 - 2 * LORA, w.shape[1]), BF16)
    return jnp.concatenate([wt[:, :g3], wt[:, g3 + 2 * LORA:], wt[:, g3:g3 + 2 * LORA], pad], axis=1)


def kernel(x, norm_w, w_in, w_out, rwkv_mu, rwkv_w0, rwkv_w2, rwkv_a0, rwkv_a2, rwkv_k_k, rwkv_k_a,
           rwkv_r_k, rwkv_lnx_w, rwkv_lnx_b, conv_w, rel_bias, final_norm_w):
    b, t, d = x.shape
    depth = w_in.shape[0]
    bias_tiles = _moba_bias_tiles(rel_bias)
    ret_tables = _ret_tables(t)
    x2 = x.reshape(b * t, d)
    w_in_bf = _reorder_w_in(w_in)
    w_out_bf = w_out.astype(BF16)
    for l in range(depth):
        p2 = _inproj(x2, norm_w[l], w_in_bf, l)
        p3 = p2.reshape(b, t, P_COLS)
        y_rwkv = _rwkv(p3, rwkv_mu[l], rwkv_w0[l], rwkv_w2[l], rwkv_a0[l], rwkv_a2[l], rwkv_k_k[l],
                       rwkv_k_a[l], rwkv_r_k[l].reshape(GROUP), rwkv_lnx_w[l], rwkv_lnx_b[l])
        y_moba = _moba(p3, bias_tiles, rel_bias)
        y_ret = _ret(p3, ret_tables)
        flat = lambda y: y.reshape(b * t, GROUP)
        x2 = _outproj(x2, p2, flat(y_rwkv), flat(y_moba), flat(y_ret), conv_w[l], w_out_bf, l,
                      final_norm_w, final=(l == depth - 1), seq_len=t)
    return x2.reshape(b, t, d)
```

```python
import functools
import math

import numpy as np
import jax
import jax.numpy as jnp
from jax import lax
from jax.experimental import pallas as pl
from jax.experimental.pallas import tpu as pltpu

F32 = jnp.float32
BF16 = jnp.bfloat16
HIGHEST = lax.Precision.HIGHEST

GROUP = 256
HEADS = 4
HDIM = 64
LORA = 32
DECAY_SCALE = math.exp(-0.5)
LOG2E = 1.0 / math.log(2.0)
LNX_EPS = 64e-5
NORM_EPS = 1e-6
CONV_TAPS = 3
MOBA_BLOCK = 256
MOBA_TOPK = 3
MOBA_QBLOCK = 128
RET_CHUNK = 128
REL_BUCKETS = 32
REL_MAX_DIST = 128
RWKV_CHUNK = 64

P_COLS = 3 * 4 * GROUP + 128
LORA_COL = 3 * 4 * GROUP
W_ROWS = P_COLS + 4 * GROUP

VMEM_LIMIT_V7X = 48 * 1024 * 1024


def _nn(a, b, precision=None):
    return lax.dot_general(a, b, (((1,), (0,)), ((), ())), precision=precision,
                           preferred_element_type=F32)


def _nt(a, b, precision=None):
    return lax.dot_general(a, b, (((1,), (1,)), ((), ())), precision=precision,
                           preferred_element_type=F32)


def _tn(a, b, precision=None):
    return lax.dot_general(a, b, (((0,), (0,)), ((), ())), precision=precision,
                           preferred_element_type=F32)


def _split(x):
    hi = x.astype(BF16)
    lo = (x - hi.astype(F32)).astype(BF16)
    return hi, lo


def _dot3(dot, a, b):
    out_axis = 1 if dot is _tn else 0
    m = a[0].shape[out_axis]
    both = dot(jnp.concatenate([a[0], a[1]], axis=out_axis), b[0])
    return both[:m] + both[m:] + dot(a[0], b[1])


def _psl(pair, rows, cols):
    return pair[0][rows, cols], pair[1][rows, cols]


def _pcat(pairs, axis):
    return (jnp.concatenate([p[0] for p in pairs], axis=axis),
            jnp.concatenate([p[1] for p in pairs], axis=axis))


def _head_sum(x, esum_bf16):
    hi, lo = _split(x)
    m = x.shape[0]
    both = _nn(jnp.concatenate([hi, lo], axis=0), esum_bf16)
    return both[:m] + both[m:]


def _iota(shape, dim):
    return lax.broadcasted_iota(jnp.int32, shape, dim)


def _head_sum_matrix():
    r = _iota((GROUP, GROUP), 0) // HDIM
    c = _iota((GROUP, GROUP), 1) // HDIM
    return jnp.where(r == c, 1.0, 0.0).astype(F32)


def _silu(x):
    return x * jax.nn.sigmoid(x)


def _gated_conv(p, cw_ref, tail_ref, first_tile):
    @pl.when(first_tile)
    def _():
        tail_ref[...] = jnp.zeros_like(tail_ref)

    n_rows = p.shape[0]
    u = p[:, GROUP:2 * GROUP] * p[:, 2 * GROUP:3 * GROUP]
    rows = _iota((n_rows, 1), 0)
    u1 = jnp.where(rows == 0, tail_ref[1:2, :], pltpu.roll(u, 1, axis=0))
    u2 = jnp.where(rows == 0, tail_ref[0:1, :],
                   jnp.where(rows == 1, tail_ref[1:2, :], pltpu.roll(u, 2, axis=0)))
    tail_ref[0:2, :] = u[n_rows - 2:n_rows, :]
    y = u2 * cw_ref[0:1, :] + u1 * cw_ref[1:2, :] + u * cw_ref[2:3, :]
    return p[:, 0:GROUP] * y * _silu(p[:, 3 * GROUP:4 * GROUP])


def _inproj_kernel(x_ref, nw_ref, w_ref, cw_ref, o_ref, yconv_ref, tail_ref, *, tiles_per_seq):
    x = x_ref[...]
    h = x * lax.rsqrt(jnp.mean(x * x, axis=-1, keepdims=True) + NORM_EPS) * nw_ref[...]
    proj = _nt(h.astype(BF16), w_ref[0])
    o_ref[...] = proj[:, 0:P_COLS]
    y_conv = _gated_conv(proj[:, P_COLS:W_ROWS], cw_ref, tail_ref, pl.program_id(0) % tiles_per_seq == 0)
    yconv_ref[...] = y_conv.astype(BF16)


def _inproj(x2d, norm_w, w_bf16, conv_w, layer, seq_len, tm=512):
    m, d = x2d.shape
    assert seq_len % tm == 0 and w_bf16.shape[1] == W_ROWS
    return pl.pallas_call(
        functools.partial(_inproj_kernel, tiles_per_seq=seq_len // tm),
        out_shape=(jax.ShapeDtypeStruct((m, P_COLS), F32),
                   jax.ShapeDtypeStruct((m, GROUP), BF16)),
        grid=(m // tm,),
        in_specs=[pl.BlockSpec((tm, d), lambda i: (i, 0)),
                  pl.BlockSpec((1, d), lambda i: (0, 0)),
                  pl.BlockSpec((1, W_ROWS, d), lambda i: (layer, 0, 0)),
                  pl.BlockSpec((CONV_TAPS, GROUP), lambda i: (0, 0))],
        out_specs=(pl.BlockSpec((tm, P_COLS), lambda i: (i, 0)),
                   pl.BlockSpec((tm, GROUP), lambda i: (i, 0))),
        scratch_shapes=[pltpu.VMEM((8, GROUP), F32)],
        compiler_params=pltpu.CompilerParams(
            dimension_semantics=("arbitrary",), vmem_limit_bytes=VMEM_LIMIT_V7X),
        name="inproj",
    )(x2d, norm_w.reshape(1, d), w_bf16, conv_w)


def _outproj_kernel(x_ref, y0_ref, y1_ref, y2_ref, y3_ref, w_ref, fw_ref, o_ref, *, final):
    acc = x_ref[...]
    for g, y_ref in enumerate((y0_ref, y1_ref, y2_ref, y3_ref)):
        acc = acc + _nn(y_ref[...], w_ref[0, g * GROUP:(g + 1) * GROUP, :])
    if final:
        acc = acc * lax.rsqrt(jnp.mean(acc * acc, axis=-1, keepdims=True) + NORM_EPS) * fw_ref[...]
    o_ref[...] = acc


def _outproj(x2d, ys, w_bf16, layer, final_w, final, tm=1024):
    m, d = x2d.shape
    yspec = pl.BlockSpec((tm, GROUP), lambda i: (i, 0))
    return pl.pallas_call(
        functools.partial(_outproj_kernel, final=final),
        out_shape=jax.ShapeDtypeStruct((m, d), F32),
        grid=(m // tm,),
        in_specs=[pl.BlockSpec((tm, d), lambda i: (i, 0)), yspec, yspec, yspec, yspec,
                  pl.BlockSpec((1, 4 * GROUP, d), lambda i: (layer, 0, 0)),
                  pl.BlockSpec((1, d), lambda i: (0, 0))],
        out_specs=pl.BlockSpec((tm, d), lambda i: (i, 0)),
        compiler_params=pltpu.CompilerParams(
            dimension_semantics=("arbitrary",), vmem_limit_bytes=VMEM_LIMIT_V7X),
        name="outproj_final" if final else "outproj",
    )(x2d, *ys, w_bf16, final_w.reshape(1, d))


def _rwkv_kernel(rkv_ref, gate_ref, lora_ref, mu_ref, mul_ref, w0_ref, w2_ref, a0_ref, a2_ref,
                 kk_ref, ka_ref, rk_ref, lnw_ref, lnb_ref, o_ref,
                 state_ref, prev_ref, prevl_ref, *, tt, nseq):
    C = RWKV_CHUNK
    n_rows = nseq * tt
    t_idx = pl.program_id(1)

    @pl.when(t_idx == 0)
    def _():
        state_ref[...] = jnp.zeros_like(state_ref)
        prev_ref[...] = jnp.zeros_like(prev_ref)
        prevl_ref[...] = jnp.zeros_like(prevl_ref)

    esum = _head_sum_matrix().astype(BF16)
    per_seq = tt // C
    row = _iota((tt, 1), 0)

    def prepare(s):
        p = rkv_ref[s]
        lo = lora_ref[s]
        p_sh = jnp.where(row == 0, prev_ref[s:s + 1, :], pltpu.roll(p, 1, axis=0))
        lo_sh = jnp.where(row == 0, prevl_ref[s:s + 1, :], pltpu.roll(lo, 1, axis=0))
        prev_ref[s:s + 1, :] = p[tt - 1:tt, :]
        prevl_ref[s:s + 1, :] = lo[tt - 1:tt, :]
        p = p + (p_sh - p) * mu_ref[...]
        lo = lo + (lo_sh - lo) * mul_ref[...]
        r = p[:, 0:GROUP]
        k = p[:, GROUP:2 * GROUP]
        v = p[:, 2 * GROUP:3 * GROUP]
        lw = -DECAY_SCALE * jax.nn.sigmoid(
            w0_ref[...] + _dot3(_nn, _split(jnp.tanh(lo)), _split(w2_ref[...])))
        rate = jax.nn.sigmoid(a0_ref[...] + _dot3(_nn, _split(lo), _split(a2_ref[...])))
        kk = k * kk_ref[...]
        kk = kk / jnp.maximum(jnp.sqrt(_head_sum(kk * kk, esum)), 1e-12)
        k2 = k * (1.0 + (rate - 1.0) * ka_ref[...])
        bonus = _head_sum(r * k2 * rk_ref[...], esum) * v
        bvec = kk * rate
        avec = -kk
        rowmod = row % C
        cum = lw
        for sh in (1, 2, 4, 8, 16, 32):
            cum = cum + jnp.where(rowmod >= sh, pltpu.roll(cum, sh, axis=0), 0.0)
        tot_rows = [cum[c * C + C - 1:c * C + C, :] for c in range(per_seq)]
        tot = jnp.concatenate([jnp.broadcast_to(tr, (C, GROUP)) for tr in tot_rows], axis=0)
        e_neg = jnp.exp(-cum)
        e_end = jnp.exp(tot - cum)
        whole = dict(r_t=r * jnp.exp(cum), a_t=avec * jnp.exp(cum - lw), k_t=k2 * e_neg, b_t=bvec * e_neg,
                     k_h=k2 * e_end, b_h=bvec * e_end, v_s=v)
        pairs = {name: _split(x) for name, x in whole.items()}
        per_chunk = {name: [_psl(pr, slice(c * C, (c + 1) * C), slice(None)) for c in range(per_seq)]
                     for name, pr in pairs.items()}
        return per_chunk, tot_rows, bonus, [v[c * C:(c + 1) * C, :] for c in range(per_seq)]

    same_head = (_iota((GROUP, GROUP), 0) // HDIM) == (_iota((GROUP, GROUP), 1) // HDIM)

    def bd(pr):
        return tuple(jnp.where(same_head, jnp.concatenate([part] * HEADS, axis=0), jnp.zeros((), part.dtype))
                     for part in pr)

    t_i = _iota((2 * C, GROUP), 0)
    s_i = _iota((2 * C, GROUP), 1) % C
    tri = ((t_i < C) & (s_i < t_i)) | ((t_i >= C) & (s_i <= t_i - C))
    eye = jnp.where(_iota((C, GROUP), 0) == _iota((C, GROUP), 1) % C, 1.0, 0.0).astype(F32)

    chunks = range(nseq * per_seq)
    every = slice(None)
    top = lambda pr: _psl(pr, slice(0, C), every)
    bottom = lambda pr: _psl(pr, slice(C, 2 * C), every)

    ops = {name: [] for name in ("r_t", "a_t", "k_t", "b_t", "k_h", "b_h", "v_s")}
    tot_rows, bonus, ab_rb, ak_rk, v_f = [], [], [], [], []
    for s in range(nseq):
        per_chunk, tots, bonus_s, v_chunks = prepare(s)
        for name in ops:
            ops[name] += per_chunk[name]
        tot_rows += tots
        v_f += v_chunks
        bonus.append(bonus_s)
        for c in range(s * per_seq, (s + 1) * per_seq):
            ar = _pcat([ops["a_t"][c], ops["r_t"][c]], 0)
            ab_rb.append(jnp.where(tri, _dot3(_nt, ar, bd(ops["b_t"][c])), 0.0))
            ak_rk.append(_split(jnp.where(tri, _dot3(_nt, ar, bd(ops["k_t"][c])), 0.0)))
    rows = lambda name, c: ops[name][c]
    bonus = jnp.concatenate(bonus, axis=0)
    bd_v = [bd(rows("v_s", c)) for c in chunks]
    a_ab = [ab_rb[c][0:C] for c in chunks]
    ab_rb = [_split(x) for x in ab_rb]
    t_row = _iota((C, GROUP), 0)
    s_col = _iota((C, GROUP), 1) % C

    def below_diagonal(n):
        return ((t_row // (2 * n)) == (s_col // (2 * n))) & ((t_row % (2 * n)) >= n) & ((s_col % (2 * n)) < n)

    inv = [eye + jnp.where(below_diagonal(1), a_ab[c], 0.0) for c in chunks]
    n = 2
    while n < C:
        off = below_diagonal(n)
        inv_s = [_split(inv[c]) for c in chunks]
        left = [_split(_dot3(_nn, inv_s[c], bd(_split(jnp.where(off, a_ab[c], 0.0))))) for c in chunks]
        inv = [inv[c] + _dot3(_nn, left[c], bd(inv_s[c])) for c in chunks]
        n *= 2
    inv = [_split(x) for x in inv]
    t_ak = [_split(_dot3(_nn, inv[c], bd(top(ak_rk[c])))) for c in chunks]
    t_a_f = [_dot3(_nn, inv[c], bd(rows("a_t", c))) for c in chunks]
    t_a = [_split(x) for x in t_a_f]
    free = [_dot3(_nn, _pcat([t_ak[c], bottom(ak_rk[c])], 0), bd_v[c]) for c in chunks]

    def head_transpose(x):
        xt = x.T
        return jnp.concatenate([xt[h * HDIM:(h + 1) * HDIM, :] for h in range(HEADS)], axis=1)

    bd_bh = [bd(rows("b_h", c)) for c in chunks]
    m_s = [bd(_split(_dot3(_nn, _split(head_transpose(t_a_f[c])), bd_bh[c]))) for c in chunks]
    g = [_dot3(_nn, _pcat([_split(head_transpose(v_f[c])), _split(head_transpose(free[c][0:C]))], 1),
               _pcat([bd(rows("k_h", c)), bd_bh[c]], 0)) for c in chunks]
    state = [state_ref[s] for s in range(nseq)]
    from_state = {}
    for step in range(per_seq):
        for s in range(nseq):
            c = s * per_seq + step
            st_s = _split(state[s])
            state[s] = state[s] * jnp.exp(tot_rows[c]) + _dot3(_nn, st_s, m_s[c]) + g[c]
            from_state[c] = _dot3(_nt, _pcat([t_a[c], rows("r_t", c)], 0), bd(st_s))
    for s in range(nseq):
        state_ref[s] = state[s]
    y = jnp.concatenate(
        [free[c][C:2 * C] + from_state[c][C:2 * C]
         + _dot3(_nn, bottom(ab_rb[c]), bd(_split(free[c][0:C] + from_state[c][0:C]))) for c in chunks], axis=0)
    mean = _head_sum(y, esum) * (1.0 / HDIM)
    yc = y - mean
    var = _head_sum(yc * yc, esum) * (1.0 / HDIM)
    y = yc * lax.rsqrt(var + LNX_EPS) * lnw_ref[...] + lnb_ref[...] + bonus
    o_ref[...] = (y * _silu(gate_ref[...].reshape(n_rows, GROUP))).astype(BF16).reshape(nseq, tt, GROUP)


def _rwkv(p3, mu, w0, w2, a0, a2, k_k, k_a, r_k, lnx_w, lnx_b, tt=256, nseq=4):
    b, t, _ = p3.shape
    assert b % nseq == 0 and t % tt == 0 and nseq <= 8
    mu_main = mu[:3 * GROUP].reshape(1, 3 * GROUP)
    mu_lora = jnp.concatenate([mu[3 * GROUP:], jnp.zeros((128 - 2 * LORA,), F32)]).reshape(1, 128)
    w2p = jnp.zeros((128, GROUP), F32).at[0:LORA].set(w2)
    a2p = jnp.zeros((128, GROUP), F32).at[LORA:2 * LORA].set(a2)
    row = lambda a: a.reshape(1, GROUP)
    vec = pl.BlockSpec((1, GROUP), lambda i, j: (0, 0))
    return pl.pallas_call(
        functools.partial(_rwkv_kernel, tt=tt, nseq=nseq),
        out_shape=jax.ShapeDtypeStruct((b, t, GROUP), BF16),
        grid=(b // nseq, t // tt),
        in_specs=[pl.BlockSpec((nseq, tt, 3 * GROUP), lambda i, j: (i, j, 0)),
                  pl.BlockSpec((nseq, tt, GROUP), lambda i, j: (i, j, 3)),
                  pl.BlockSpec((nseq, tt, 128), lambda i, j: (i, j, LORA_COL // 128)),
                  pl.BlockSpec((1, 3 * GROUP), lambda i, j: (0, 0)),
                  pl.BlockSpec((1, 128), lambda i, j: (0, 0)),
                  vec,
                  pl.BlockSpec((128, GROUP), lambda i, j: (0, 0)),
                  vec,
                  pl.BlockSpec((128, GROUP), lambda i, j: (0, 0)),
                  vec, vec, vec, vec, vec],
        out_specs=pl.BlockSpec((nseq, tt, GROUP), lambda i, j: (i, j, 0)),
        scratch_shapes=[pltpu.VMEM((nseq, HDIM, GROUP), F32),
                        pltpu.VMEM((8, 3 * GROUP), F32),
                        pltpu.VMEM((8, 128), F32)],
        compiler_params=pltpu.CompilerParams(
            dimension_semantics=("arbitrary", "arbitrary"), vmem_limit_bytes=VMEM_LIMIT_V7X),
        name="rwkv7",
    )(p3, p3, p3, mu_main, mu_lora, row(w0), w2p, row(a0), a2p,
      row(k_k), row(k_a), row(r_k), row(lnx_w), row(lnx_b))


def _t5_bucket_np(dist):
    max_exact = REL_BUCKETS // 2
    d_f = np.maximum(dist, 1).astype(np.float32)
    large = max_exact + (np.log(d_f / np.float32(max_exact)) / np.float32(math.log(REL_MAX_DIST / max_exact))
                         * np.float32(REL_BUCKETS - max_exact)).astype(np.int32)
    large = np.minimum(large, REL_BUCKETS - 1)
    return np.where(dist < max_exact, dist, large).astype(np.int32)


MASKED_BUCKET = REL_BUCKETS


def _moba_bucket_table():
    keys = np.arange(MOBA_BLOCK)[:, None]
    queries = np.arange(MOBA_BLOCK)[None, :]
    prev = _t5_bucket_np(queries + MOBA_BLOCK - keys)
    own = np.where(keys <= queries, _t5_bucket_np(np.maximum(queries - keys, 0)), MASKED_BUCKET)
    return np.stack([prev, own]).astype(np.int32)


def _bias_kernel(idx_ref, rb_ref, o_ref):
    h = pl.program_id(0)
    idx = idx_ref[...]
    acc = jnp.full(idx.shape, -jnp.inf, F32)
    for bkt in range(REL_BUCKETS):
        acc = jnp.where(idx == bkt, rb_ref[bkt, h], acc)
    o_ref[0] = acc * LOG2E


def _moba_bias_tiles(rel_bias):
    idx = jnp.asarray(_moba_bucket_table())
    shp = (2, MOBA_BLOCK, MOBA_BLOCK)
    return pl.pallas_call(
        _bias_kernel,
        out_shape=jax.ShapeDtypeStruct((HEADS,) + shp, F32),
        grid=(HEADS,),
        in_specs=[pl.BlockSpec(shp, lambda h: (0, 0, 0)),
                  pl.BlockSpec(memory_space=pltpu.SMEM)],
        out_specs=pl.BlockSpec((1,) + shp, lambda h: (h, 0, 0, 0)),
        name="moba_bias",
    )(idx, rel_bias)


def _moba_kernel(q_ref, k_ref, v_ref, gate_ref, bias_ref, rb_ref, o_ref,
                 kmean_ref, kbf_ref, vt_ref, sel_ref, m_ref, l_ref, acc_ref, sc_ref, *, nb):
    BLK = MOBA_BLOCK
    ib = pl.program_id(1)
    heads = range(HEADS)
    rows_of = lambda h: slice(h * HDIM, (h + 1) * HDIM)
    neg_inf = -jnp.inf
    not_selected = -1e30

    @pl.when(ib == 0)
    def _():
        kmean_ref[...] = jnp.zeros_like(kmean_ref)
        for n in range(nb):
            blk = slice(n * BLK, (n + 1) * BLK)
            kblk = k_ref[0, blk, :]
            kmean_ref[n:n + 1, :] = jnp.mean(kblk, axis=0, keepdims=True)
            kbf_ref[n] = kblk.astype(BF16)
            vt_ref[n, 0:GROUP, :] = v_ref[0, blk, :].T.astype(BF16)
            vt_ref[n, GROUP:GROUP + 16, :] = jnp.ones((16, BLK), BF16)

    q = q_ref[0] * (HDIM ** -0.5 * LOG2E)
    q_bf = q.astype(BF16)
    q_s = _split(q)
    lane_head = _iota((1, GROUP), 1) // HDIM
    q_heads = [jnp.where(lane_head == h, q_bf, jnp.zeros_like(q_bf)) for h in heads]

    km = kmean_ref[...]
    gates = _dot3(_nt, _split(jnp.concatenate([jnp.where(lane_head == h, km, 0.0) for h in heads], axis=0)), q_s)
    blk_id = _iota((16, BLK), 0)
    for h in heads:
        g = jnp.where(blk_id < ib, gates[16 * h:16 * (h + 1)], neg_inf)
        ahead = jnp.zeros((16, BLK), F32)
        for m in range(nb):
            g_m = g[m:m + 1, :]
            ahead = ahead + jnp.where((g_m > g) | ((g_m == g) & (blk_id > m)), 1.0, 0.0)
        sel_ref[h] = jnp.where((ahead < MOBA_TOPK) & (g > neg_inf), 1.0, 0.0)

    def raw_scores(n, h):
        return _nt(kbf_ref[n], q_heads[h])

    def attend(blocks, first, scores_slot=None, prefetch=None):
        if scores_slot is None:
            scores = [[raw_scores(n, h) for n, _, _ in blocks] for h in heads]
        else:
            scores = [[sc_ref[scores_slot, j, h] for j in range(len(blocks))] for h in heads]
        if prefetch is not None:
            slot, nxt = prefetch
            for j, n in enumerate(nxt):
                for h in heads:
                    sc_ref[slot, j, h] = raw_scores(n, h)
        probs, alphas = [], []
        for h in heads:
            ss, tops = [], []
            for s, (_, tile, row) in zip(scores[h], blocks):
                s = s if tile is None else s + tile[h]
                top = jnp.max(s, axis=0, keepdims=True)
                ss.append(s)
                tops.append(top if row is None else top + row[h])
            m_new = functools.reduce(jnp.maximum, tops)
            if not first:
                m_old = m_ref[h]
                m_new = jnp.maximum(m_old, m_new)
                alphas.append(jnp.exp2(m_old - m_new))
            m_ref[h] = m_new
            probs.append([jnp.exp2(s + ((-m_new) if row is None else (row[h] - m_new))).astype(BF16)
                          for s, (_, _, row) in zip(ss, blocks)])
        for h in heads:
            pv = None
            for p, (n, _, _) in zip(probs[h], blocks):
                lhs = jnp.concatenate([vt_ref[n, rows_of(h), :], vt_ref[n, GROUP:GROUP + 16, :]], axis=0)
                part = _nn(lhs, p)
                pv = part if pv is None else pv + part
            if first:
                acc_ref[rows_of(h), :] = pv[0:HDIM]
                l_ref[h] = pv[HDIM:HDIM + 1]
            else:
                acc_ref[rows_of(h), :] = alphas[h] * acc_ref[rows_of(h), :] + pv[0:HDIM]
                l_ref[h] = alphas[h] * l_ref[h] + pv[HDIM:HDIM + 1]

    def selected_row(h, n, valid):
        row = jnp.where(sel_ref[h, pl.ds(n, 1), :] > 0.5, 0.0, not_selected)
        return jnp.where(valid, row, not_selected)

    prev = jnp.maximum(ib - 1, 0)
    far_pairs = [[n] if n + 1 >= nb - 2 else [n, n + 1] for n in range(0, nb - 2, 2)]
    attend([(ib, [bias_ref[h, 1] for h in heads], None),
            (prev, [bias_ref[h, 0] for h in heads], [selected_row(h, prev, ib >= 1) for h in heads])],
           first=True)

    n_far = ib - 1
    for k, pair in enumerate(far_pairs):
        @pl.when(pair[0] < n_far)
        def _(k=k, pair=pair):
            attend([(j, None, [(selected_row(h, j, j < n_far) + rb_ref[REL_BUCKETS - 1, h]) * LOG2E for h in heads])
                    for j in pair], first=False, scores_slot=(k % 2) if k > 0 else None,
                   prefetch=((k + 1) % 2, far_pairs[k + 1]) if k + 1 < len(far_pairs) else None)

    for h in heads:
        acc_ref[rows_of(h), :] = acc_ref[rows_of(h), :] / l_ref[h]
    o_ref[0] = (acc_ref[...].T * _silu(gate_ref[0])).astype(BF16)


def _moba(p3, bias_tiles, rel_bias):
    b, t, _ = p3.shape
    BLK = MOBA_BLOCK
    nb = t // BLK
    assert 2 <= nb <= 16 and t % BLK == 0
    base = 1 * 4
    tile = lambda c: pl.BlockSpec((1, BLK, GROUP), lambda i, j, c=c: (i, j, c))
    seq = lambda c: pl.BlockSpec((1, t, GROUP), lambda i, j, c=c: (i, 0, c))
    return pl.pallas_call(
        functools.partial(_moba_kernel, nb=nb),
        out_shape=jax.ShapeDtypeStruct((b, t, GROUP), BF16),
        grid=(b, nb),
        in_specs=[tile(base), seq(base + 1), seq(base + 2), tile(base + 3),
                  pl.BlockSpec(bias_tiles.shape, lambda i, j: (0, 0, 0, 0)),
                  pl.BlockSpec(memory_space=pltpu.SMEM)],
        out_specs=pl.BlockSpec((1, BLK, GROUP), lambda i, j: (i, j, 0)),
        scratch_shapes=[pltpu.VMEM((16, GROUP), F32),
                        pltpu.VMEM((nb, BLK, GROUP), BF16),
                        pltpu.VMEM((nb, GROUP + 16, BLK), BF16),
                        pltpu.VMEM((HEADS, 16, BLK), F32),
                        pltpu.VMEM((HEADS, 1, BLK), F32),
                        pltpu.VMEM((HEADS, 1, BLK), F32),
                        pltpu.VMEM((GROUP, BLK), F32),
                        pltpu.VMEM((2, 2, HEADS, BLK, BLK), F32)],
        compiler_params=pltpu.CompilerParams(
            dimension_semantics=("arbitrary", "arbitrary"), vmem_limit_bytes=VMEM_LIMIT_V7X),
        name="moba",
    )(p3, p3, p3, p3, bias_tiles, rel_bias)


def _ret_tables(t):
    half = HDIM // 2
    theta = 1.0 / (10000.0 ** np.linspace(0.0, 1.0, half))
    pos = np.arange(t, dtype=np.float64)
    ang = pos[:, None] * theta[None, :]
    sin, cos = np.sin(ang), np.cos(ang)
    cos2 = np.tile(np.repeat(cos, 2, axis=1), (1, HEADS))
    sin2 = np.tile(np.stack([-sin, sin], axis=-1).reshape(t, HDIM), (1, HEADS))
    log_gamma = np.log(1.0 - 2.0 ** (-5.0 - np.arange(HEADS, dtype=np.float64)))
    idx = np.arange(RET_CHUNK, dtype=np.float64)
    diff = idx[:, None] - idx[None, :]
    decay_intra = np.where(diff >= 0, np.exp(log_gamma[:, None, None] * np.maximum(diff, 0.0)), 0.0)
    decay_intra = np.concatenate(list(decay_intra), axis=1)
    q_decay = np.exp(log_gamma[:, None] * (idx + 1.0))
    k_decay = np.exp(log_gamma[:, None] * (RET_CHUNK - 1.0 - idx))
    chunk_decay = np.exp(log_gamma * RET_CHUNK)
    lanes = lambda a: np.repeat(a.T, HDIM, axis=1)
    cd = np.repeat(chunk_decay, HDIM).reshape(1, GROUP)
    return tuple(jnp.asarray(a, F32) for a in (cos2, sin2, decay_intra, lanes(q_decay), lanes(k_decay), cd))


def _ret_kernel(q_ref, k_ref, v_ref, gate_ref, cos_ref, sin_ref, di_ref, qd_ref, kd_ref, cd_ref,
                o_ref, state_ref, *, tt):
    C = RET_CHUNK
    n_chunks = tt // C
    chunks = range(n_chunks)
    heads = range(HEADS)
    b = pl.program_id(1)

    @pl.when(pl.program_id(0) == 0)
    def _():
        state_ref[b] = jnp.zeros((GROUP, GROUP), F32)

    even = (_iota((tt, GROUP), 1) % 2) == 0

    def rotate(x):
        swapped = jnp.where(even, pltpu.roll(x, GROUP - 1, axis=1), pltpu.roll(x, 1, axis=1))
        return x * cos_ref[...] + swapped * sin_ref[...]

    q = rotate(q_ref[0])
    k = rotate(k_ref[0]) * (HDIM ** -0.5)
    lane_head = _iota((1, GROUP), 1) // HDIM
    same_head = (_iota((GROUP, GROUP), 0) // HDIM) == (_iota((GROUP, GROUP), 1) // HDIM)
    rows = lambda c: slice(c * C, (c + 1) * C)

    q_b = [q[rows(c)].astype(BF16) for c in chunks]
    k_b = [k[rows(c)].astype(BF16) for c in chunks]
    v_b = [v_ref[0, rows(c), :].astype(BF16) for c in chunks]
    kd_b = [(k[rows(c)] * kd_ref[...]).astype(BF16) for c in chunks]
    qd_b = [(q[rows(c)] * qd_ref[...]).astype(BF16) for c in chunks]

    row_head_is_lane_head = (_iota((HEADS * C, GROUP), 0) // C) == (_iota((HEADS * C, GROUP), 1) // HDIM)

    def per_head_rows(x):
        return jnp.where(row_head_is_lane_head, jnp.concatenate([x] * HEADS, axis=0), jnp.zeros((), x.dtype))

    inner = [(_nt(q_b[c], per_head_rows(k_b[c])) * di_ref[...]).astype(BF16) for c in chunks]
    kv = [jnp.where(same_head, _tn(kd_b[c], v_b[c]), 0.0) for c in chunks]
    intra = [_nn(inner[c], per_head_rows(v_b[c])) for c in chunks]
    state = state_ref[b]
    states = []
    for c in chunks:
        states.append(state.astype(BF16))
        state = state * cd_ref[...] + kv[c]
    state_ref[b] = state
    y = jnp.concatenate([intra[c] + _nn(qd_b[c], states[c]) for c in chunks], axis=0)
    mean_sq = _head_sum(y * y, _head_sum_matrix().astype(BF16)) * (1.0 / HDIM)
    o_ref[0] = (y * lax.rsqrt(mean_sq + NORM_EPS) * _silu(gate_ref[0])).astype(BF16)


def _ret(p3, tables, tt=512):
    b, t, _ = p3.shape
    base = 2 * 4
    cos2, sin2, di, qd, kd, cd = tables
    blk = lambda c: pl.BlockSpec((1, tt, GROUP), lambda j, i, c=c: (i, j, c))
    full2 = lambda a: pl.BlockSpec(a.shape, lambda j, i: (0, 0))
    return pl.pallas_call(
        functools.partial(_ret_kernel, tt=tt),
        out_shape=jax.ShapeDtypeStruct((b, t, GROUP), BF16),
        grid=(t // tt, b),
        in_specs=[blk(base), blk(base + 1), blk(base + 2), blk(base + 3),
                  pl.BlockSpec((tt, GROUP), lambda j, i: (j, 0)),
                  pl.BlockSpec((tt, GROUP), lambda j, i: (j, 0)),
                  full2(di), full2(qd), full2(kd), full2(cd)],
        out_specs=pl.BlockSpec((1, tt, GROUP), lambda j, i: (i, j, 0)),
        scratch_shapes=[pltpu.VMEM((b, GROUP, GROUP), F32)],
        compiler_params=pltpu.CompilerParams(
            dimension_semantics=("arbitrary", "arbitrary"), vmem_limit_bytes=VMEM_LIMIT_V7X),
        name="retnet",
    )(p3, p3, p3, p3, cos2, sin2, di, qd, kd, cd)


def _reorder_w_in(w):
    wt = jnp.swapaxes(w, 1, 2).astype(BF16)
    g3, g4 = 3 * GROUP, 4 * GROUP
    lora_end = g3 + 2 * LORA
    conv_start = lora_end + GROUP
    pad = jnp.zeros((w.shape[0], 128 - 2 * LORA, w.shape[1]), BF16)
    return jnp.concatenate([wt[:, :g3], wt[:, lora_end:conv_start], wt[:, conv_start + g4:],
                            wt[:, g3:lora_end], pad, wt[:, conv_start:conv_start + g4]], axis=1)


def kernel(x, norm_w, w_in, w_out, rwkv_mu, rwkv_w0, rwkv_w2, rwkv_a0, rwkv_a2, rwkv_k_k, rwkv_k_a,
           rwkv_r_k, rwkv_lnx_w, rwkv_lnx_b, conv_w, rel_bias, final_norm_w):
    b, t, d = x.shape
    depth = w_in.shape[0]
    bias_tiles = _moba_bias_tiles(rel_bias)
    ret_tables = _ret_tables(t)
    x2 = x.reshape(b * t, d)
    w_in_bf = _reorder_w_in(w_in)
    w_out_bf = w_out.astype(BF16)
    for l in range(depth):
        p2, y_conv = _inproj(x2, norm_w[l], w_in_bf, conv_w[l], l, seq_len=t)
        p3 = p2.reshape(b, t, P_COLS)
        y_rwkv = _rwkv(p3, rwkv_mu[l], rwkv_w0[l], rwkv_w2[l], rwkv_a0[l], rwkv_a2[l], rwkv_k_k[l],
                       rwkv_k_a[l], rwkv_r_k[l].reshape(GROUP), rwkv_lnx_w[l], rwkv_lnx_b[l])
        y_moba = _moba(p3, bias_tiles, rel_bias)
        y_ret = _ret(p3, ret_tables)
        flat = lambda y: y.reshape(b * t, GROUP)
        x2 = _outproj(x2, [flat(y_rwkv), y_conv, flat(y_moba), flat(y_ret)], w_out_bf, l,
                      final_norm_w, final=(l == depth - 1))
    return x2.reshape(b, t, d)
```

```python
import functools
import math

import numpy as np
import jax
import jax.numpy as jnp
from jax import lax
from jax.experimental import pallas as pl
from jax.experimental.pallas import tpu as pltpu

F32 = jnp.float32
BF16 = jnp.bfloat16
HIGHEST = lax.Precision.HIGHEST

GROUP = 256
HEADS = 4
HDIM = 64
LORA = 32
DECAY_SCALE = math.exp(-0.5)
LOG2E = 1.0 / math.log(2.0)
LNX_EPS = 64e-5
NORM_EPS = 1e-6
CONV_TAPS = 3
MOBA_BLOCK = 256
MOBA_TOPK = 3
MOBA_QBLOCK = 128
RET_CHUNK = 128
REL_BUCKETS = 32
REL_MAX_DIST = 128
RWKV_CHUNK = 64

P_COLS = 3 * 4 * GROUP + 128
LORA_COL = 3 * 4 * GROUP
W_ROWS = P_COLS + 4 * GROUP

VMEM_LIMIT_V7X = 48 * 1024 * 1024


def _nn(a, b, precision=None):
    return lax.dot_general(a, b, (((1,), (0,)), ((), ())), precision=precision,
                           preferred_element_type=F32)


def _nt(a, b, precision=None):
    return lax.dot_general(a, b, (((1,), (1,)), ((), ())), precision=precision,
                           preferred_element_type=F32)


def _tn(a, b, precision=None):
    return lax.dot_general(a, b, (((0,), (0,)), ((), ())), precision=precision,
                           preferred_element_type=F32)


def _split(x):
    hi = x.astype(BF16)
    lo = (x - hi.astype(F32)).astype(BF16)
    return hi, lo


def _dot3(dot, a, b):
    out_axis = 1 if dot is _tn else 0
    m = a[0].shape[out_axis]
    both = dot(jnp.concatenate([a[0], a[1]], axis=out_axis), b[0])
    return both[:m] + both[m:] + dot(a[0], b[1])


def _psl(pair, rows, cols):
    return pair[0][rows, cols], pair[1][rows, cols]


def _pcat(pairs, axis):
    return (jnp.concatenate([p[0] for p in pairs], axis=axis),
            jnp.concatenate([p[1] for p in pairs], axis=axis))


def _head_sum(x, esum_bf16):
    hi, lo = _split(x)
    m = x.shape[0]
    both = _nn(jnp.concatenate([hi, lo], axis=0), esum_bf16)
    return both[:m] + both[m:]


def _iota(shape, dim):
    return lax.broadcasted_iota(jnp.int32, shape, dim)


def _head_sum_matrix():
    r = _iota((GROUP, GROUP), 0) // HDIM
    c = _iota((GROUP, GROUP), 1) // HDIM
    return jnp.where(r == c, 1.0, 0.0).astype(F32)


def _silu(x):
    return x * jax.nn.sigmoid(x)


def _gated_conv(p, cw_ref, tail_ref, first_tile):
    @pl.when(first_tile)
    def _():
        tail_ref[...] = jnp.zeros_like(tail_ref)

    n_rows = p.shape[0]
    u = p[:, GROUP:2 * GROUP] * p[:, 2 * GROUP:3 * GROUP]
    rows = _iota((n_rows, 1), 0)
    u1 = jnp.where(rows == 0, tail_ref[1:2, :], pltpu.roll(u, 1, axis=0))
    u2 = jnp.where(rows == 0, tail_ref[0:1, :],
                   jnp.where(rows == 1, tail_ref[1:2, :], pltpu.roll(u, 2, axis=0)))
    tail_ref[0:2, :] = u[n_rows - 2:n_rows, :]
    y = u2 * cw_ref[0:1, :] + u1 * cw_ref[1:2, :] + u * cw_ref[2:3, :]
    return p[:, 0:GROUP] * y * _silu(p[:, 3 * GROUP:4 * GROUP])


def _inproj_kernel(x_ref, nw_ref, w_ref, cw_ref, o_ref, yconv_ref, tail_ref, *, tiles_per_seq):
    x = x_ref[...]
    h = x * lax.rsqrt(jnp.mean(x * x, axis=-1, keepdims=True) + NORM_EPS) * nw_ref[...]
    proj = _nt(h.astype(BF16), w_ref[0])
    o_ref[...] = proj[:, 0:P_COLS]
    y_conv = _gated_conv(proj[:, P_COLS:W_ROWS], cw_ref, tail_ref, pl.program_id(0) % tiles_per_seq == 0)
    yconv_ref[...] = y_conv.astype(BF16)


def _inproj(x2d, norm_w, w_bf16, conv_w, layer, seq_len, tm=512):
    m, d = x2d.shape
    assert seq_len % tm == 0 and w_bf16.shape[1] == W_ROWS
    return pl.pallas_call(
        functools.partial(_inproj_kernel, tiles_per_seq=seq_len // tm),
        out_shape=(jax.ShapeDtypeStruct((m, P_COLS), F32),
                   jax.ShapeDtypeStruct((m, GROUP), BF16)),
        grid=(m // tm,),
        in_specs=[pl.BlockSpec((tm, d), lambda i: (i, 0)),
                  pl.BlockSpec((1, d), lambda i: (0, 0)),
                  pl.BlockSpec((1, W_ROWS, d), lambda i: (layer, 0, 0)),
                  pl.BlockSpec((CONV_TAPS, GROUP), lambda i: (0, 0))],
        out_specs=(pl.BlockSpec((tm, P_COLS), lambda i: (i, 0)),
                   pl.BlockSpec((tm, GROUP), lambda i: (i, 0))),
        scratch_shapes=[pltpu.VMEM((8, GROUP), F32)],
        compiler_params=pltpu.CompilerParams(
            dimension_semantics=("arbitrary",), vmem_limit_bytes=VMEM_LIMIT_V7X),
        name="inproj",
    )(x2d, norm_w.reshape(1, d), w_bf16, conv_w)


def _outproj_kernel(x_ref, y0_ref, y1_ref, y2_ref, y3_ref, w_ref, fw_ref, o_ref, *, final):
    acc = x_ref[...]
    for g, y_ref in enumerate((y0_ref, y1_ref, y2_ref, y3_ref)):
        acc = acc + _nn(y_ref[...], w_ref[0, g * GROUP:(g + 1) * GROUP, :])
    if final:
        acc = acc * lax.rsqrt(jnp.mean(acc * acc, axis=-1, keepdims=True) + NORM_EPS) * fw_ref[...]
    o_ref[...] = acc


def _outproj(x2d, ys, w_bf16, layer, final_w, final, tm=1024):
    m, d = x2d.shape
    yspec = pl.BlockSpec((tm, GROUP), lambda i: (i, 0))
    return pl.pallas_call(
        functools.partial(_outproj_kernel, final=final),
        out_shape=jax.ShapeDtypeStruct((m, d), F32),
        grid=(m // tm,),
        in_specs=[pl.BlockSpec((tm, d), lambda i: (i, 0)), yspec, yspec, yspec, yspec,
                  pl.BlockSpec((1, 4 * GROUP, d), lambda i: (layer, 0, 0)),
                  pl.BlockSpec((1, d), lambda i: (0, 0))],
        out_specs=pl.BlockSpec((tm, d), lambda i: (i, 0)),
        compiler_params=pltpu.CompilerParams(
            dimension_semantics=("arbitrary",), vmem_limit_bytes=VMEM_LIMIT_V7X),
        name="outproj_final" if final else "outproj",
    )(x2d, *ys, w_bf16, final_w.reshape(1, d))


def _rwkv_kernel(rkv_ref, gate_ref, lora_ref, mu_ref, mul_ref, w0_ref, w2_ref, a0_ref, a2_ref,
                 kk_ref, ka_ref, rk_ref, lnw_ref, lnb_ref, o_ref,
                 state_ref, prev_ref, prevl_ref, *, tt, nseq):
    C = RWKV_CHUNK
    n_rows = nseq * tt
    t_idx = pl.program_id(1)

    @pl.when(t_idx == 0)
    def _():
        state_ref[...] = jnp.zeros_like(state_ref)
        prev_ref[...] = jnp.zeros_like(prev_ref)
        prevl_ref[...] = jnp.zeros_like(prevl_ref)

    esum = _head_sum_matrix().astype(BF16)
    per_seq = tt // C
    row = _iota((tt, 1), 0)

    def prepare(s):
        p = rkv_ref[s]
        lo = lora_ref[s]
        p_sh = jnp.where(row == 0, prev_ref[s:s + 1, :], pltpu.roll(p, 1, axis=0))
        lo_sh = jnp.where(row == 0, prevl_ref[s:s + 1, :], pltpu.roll(lo, 1, axis=0))
        prev_ref[s:s + 1, :] = p[tt - 1:tt, :]
        prevl_ref[s:s + 1, :] = lo[tt - 1:tt, :]
        p = p + (p_sh - p) * mu_ref[...]
        lo = lo + (lo_sh - lo) * mul_ref[...]
        r = p[:, 0:GROUP]
        k = p[:, GROUP:2 * GROUP]
        v = p[:, 2 * GROUP:3 * GROUP]
        lw = -DECAY_SCALE * jax.nn.sigmoid(
            w0_ref[...] + _dot3(_nn, _split(jnp.tanh(lo)), _split(w2_ref[...])))
        rate = jax.nn.sigmoid(a0_ref[...] + _dot3(_nn, _split(lo), _split(a2_ref[...])))
        kk = k * kk_ref[...]
        kk = kk / jnp.maximum(jnp.sqrt(_head_sum(kk * kk, esum)), 1e-12)
        k2 = k * (1.0 + (rate - 1.0) * ka_ref[...])
        bonus = _head_sum(r * k2 * rk_ref[...], esum) * v
        bvec = kk * rate
        avec = -kk
        rowmod = row % C
        cum = lw
        for sh in (1, 2, 4, 8, 16, 32):
            cum = cum + jnp.where(rowmod >= sh, pltpu.roll(cum, sh, axis=0), 0.0)
        tot_rows = [cum[c * C + C - 1:c * C + C, :] for c in range(per_seq)]
        tot = jnp.concatenate([jnp.broadcast_to(tr, (C, GROUP)) for tr in tot_rows], axis=0)
        e_neg = jnp.exp(-cum)
        e_end = jnp.exp(tot - cum)
        whole = dict(r_t=r * jnp.exp(cum), a_t=avec * jnp.exp(cum - lw), k_t=k2 * e_neg, b_t=bvec * e_neg,
                     k_h=k2 * e_end, b_h=bvec * e_end, v_s=v)
        pairs = {name: _split(x) for name, x in whole.items()}
        per_chunk = {name: [_psl(pr, slice(c * C, (c + 1) * C), slice(None)) for c in range(per_seq)]
                     for name, pr in pairs.items()}
        return per_chunk, tot_rows, bonus, [v[c * C:(c + 1) * C, :] for c in range(per_seq)]

    same_head = (_iota((GROUP, GROUP), 0) // HDIM) == (_iota((GROUP, GROUP), 1) // HDIM)

    def bd(pr):
        return tuple(jnp.where(same_head, jnp.concatenate([part] * HEADS, axis=0), jnp.zeros((), part.dtype))
                     for part in pr)

    t_i = _iota((2 * C, GROUP), 0)
    s_i = _iota((2 * C, GROUP), 1) % C
    tri = ((t_i < C) & (s_i < t_i)) | ((t_i >= C) & (s_i <= t_i - C))
    eye = jnp.where(_iota((C, GROUP), 0) == _iota((C, GROUP), 1) % C, 1.0, 0.0).astype(F32)

    chunks = range(nseq * per_seq)
    every = slice(None)
    top = lambda pr: _psl(pr, slice(0, C), every)
    bottom = lambda pr: _psl(pr, slice(C, 2 * C), every)

    ops = {name: [] for name in ("r_t", "a_t", "k_t", "b_t", "k_h", "b_h", "v_s")}
    tot_rows, bonus, ab_rb, ak_rk, v_f = [], [], [], [], []
    for s in range(nseq):
        per_chunk, tots, bonus_s, v_chunks = prepare(s)
        for name in ops:
            ops[name] += per_chunk[name]
        tot_rows += tots
        v_f += v_chunks
        bonus.append(bonus_s)
        for c in range(s * per_seq, (s + 1) * per_seq):
            ar = _pcat([ops["a_t"][c], ops["r_t"][c]], 0)
            ab_rb.append(jnp.where(tri, _dot3(_nt, ar, bd(ops["b_t"][c])), 0.0))
            ak_rk.append(_split(jnp.where(tri, _dot3(_nt, ar, bd(ops["k_t"][c])), 0.0)))
    rows = lambda name, c: ops[name][c]
    bonus = jnp.concatenate(bonus, axis=0)
    bd_v = [bd(rows("v_s", c)) for c in chunks]
    a_ab = [ab_rb[c][0:C] for c in chunks]
    ab_rb = [_split(x) for x in ab_rb]
    t_row = _iota((C, GROUP), 0)
    s_col = _iota((C, GROUP), 1) % C

    def below_diagonal(n):
        return ((t_row // (2 * n)) == (s_col // (2 * n))) & ((t_row % (2 * n)) >= n) & ((s_col % (2 * n)) < n)

    inv = [eye + jnp.where(below_diagonal(1), a_ab[c], 0.0) for c in chunks]
    n = 2
    while n < C:
        off = below_diagonal(n)
        inv_s = [_split(inv[c]) for c in chunks]
        left = [_split(_dot3(_nn, inv_s[c], bd(_split(jnp.where(off, a_ab[c], 0.0))))) for c in chunks]
        inv = [inv[c] + _dot3(_nn, left[c], bd(inv_s[c])) for c in chunks]
        n *= 2
    inv = [_split(x) for x in inv]
    t_ak = [_split(_dot3(_nn, inv[c], bd(top(ak_rk[c])))) for c in chunks]
    t_a_f = [_dot3(_nn, inv[c], bd(rows("a_t", c))) for c in chunks]
    t_a = [_split(x) for x in t_a_f]
    free = [_dot3(_nn, _pcat([t_ak[c], bottom(ak_rk[c])], 0), bd_v[c]) for c in chunks]

    def head_transpose(x):
        xt = x.T
        return jnp.concatenate([xt[h * HDIM:(h + 1) * HDIM, :] for h in range(HEADS)], axis=1)

    bd_bh = [bd(rows("b_h", c)) for c in chunks]
    m_s = [bd(_split(_dot3(_nn, _split(head_transpose(t_a_f[c])), bd_bh[c]))) for c in chunks]
    g = [_dot3(_nn, _pcat([_split(head_transpose(v_f[c])), _split(head_transpose(free[c][0:C]))], 1),
               _pcat([bd(rows("k_h", c)), bd_bh[c]], 0)) for c in chunks]
    state = [state_ref[s] for s in range(nseq)]
    from_state = {}
    for step in range(per_seq):
        for s in range(nseq):
            c = s * per_seq + step
            st_s = _split(state[s])
            state[s] = state[s] * jnp.exp(tot_rows[c]) + _dot3(_nn, st_s, m_s[c]) + g[c]
            from_state[c] = _dot3(_nt, _pcat([t_a[c], rows("r_t", c)], 0), bd(st_s))
    for s in range(nseq):
        state_ref[s] = state[s]
    y = jnp.concatenate(
        [free[c][C:2 * C] + from_state[c][C:2 * C]
         + _dot3(_nn, bottom(ab_rb[c]), bd(_split(free[c][0:C] + from_state[c][0:C]))) for c in chunks], axis=0)
    mean = _head_sum(y, esum) * (1.0 / HDIM)
    yc = y - mean
    var = _head_sum(yc * yc, esum) * (1.0 / HDIM)
    y = yc * lax.rsqrt(var + LNX_EPS) * lnw_ref[...] + lnb_ref[...] + bonus
    o_ref[...] = (y * _silu(gate_ref[...].reshape(n_rows, GROUP))).astype(BF16).reshape(nseq, tt, GROUP)


def _rwkv(p3, mu, w0, w2, a0, a2, k_k, k_a, r_k, lnx_w, lnx_b, tt=256, nseq=4):
    b, t, _ = p3.shape
    assert b % nseq == 0 and t % tt == 0 and nseq <= 8
    mu_main = mu[:3 * GROUP].reshape(1, 3 * GROUP)
    mu_lora = jnp.concatenate([mu[3 * GROUP:], jnp.zeros((128 - 2 * LORA,), F32)]).reshape(1, 128)
    w2p = jnp.zeros((128, GROUP), F32).at[0:LORA].set(w2)
    a2p = jnp.zeros((128, GROUP), F32).at[LORA:2 * LORA].set(a2)
    row = lambda a: a.reshape(1, GROUP)
    vec = pl.BlockSpec((1, GROUP), lambda i, j: (0, 0))
    return pl.pallas_call(
        functools.partial(_rwkv_kernel, tt=tt, nseq=nseq),
        out_shape=jax.ShapeDtypeStruct((b, t, GROUP), BF16),
        grid=(b // nseq, t // tt),
        in_specs=[pl.BlockSpec((nseq, tt, 3 * GROUP), lambda i, j: (i, j, 0)),
                  pl.BlockSpec((nseq, tt, GROUP), lambda i, j: (i, j, 3)),
                  pl.BlockSpec((nseq, tt, 128), lambda i, j: (i, j, LORA_COL // 128)),
                  pl.BlockSpec((1, 3 * GROUP), lambda i, j: (0, 0)),
                  pl.BlockSpec((1, 128), lambda i, j: (0, 0)),
                  vec,
                  pl.BlockSpec((128, GROUP), lambda i, j: (0, 0)),
                  vec,
                  pl.BlockSpec((128, GROUP), lambda i, j: (0, 0)),
                  vec, vec, vec, vec, vec],
        out_specs=pl.BlockSpec((nseq, tt, GROUP), lambda i, j: (i, j, 0)),
        scratch_shapes=[pltpu.VMEM((nseq, HDIM, GROUP), F32),
                        pltpu.VMEM((8, 3 * GROUP), F32),
                        pltpu.VMEM((8, 128), F32)],
        compiler_params=pltpu.CompilerParams(
            dimension_semantics=("arbitrary", "arbitrary"), vmem_limit_bytes=VMEM_LIMIT_V7X),
        name="rwkv7",
    )(p3, p3, p3, mu_main, mu_lora, row(w0), w2p, row(a0), a2p,
      row(k_k), row(k_a), row(r_k), row(lnx_w), row(lnx_b))


def _t5_bucket_np(dist):
    max_exact = REL_BUCKETS // 2
    d_f = np.maximum(dist, 1).astype(np.float32)
    large = max_exact + (np.log(d_f / np.float32(max_exact)) / np.float32(math.log(REL_MAX_DIST / max_exact))
                         * np.float32(REL_BUCKETS - max_exact)).astype(np.int32)
    large = np.minimum(large, REL_BUCKETS - 1)
    return np.where(dist < max_exact, dist, large).astype(np.int32)


MASKED_BUCKET = REL_BUCKETS


def _moba_bucket_table():
    keys = np.arange(MOBA_BLOCK)[:, None]
    queries = np.arange(MOBA_BLOCK)[None, :]
    prev = _t5_bucket_np(queries + MOBA_BLOCK - keys)
    own = np.where(keys <= queries, _t5_bucket_np(np.maximum(queries - keys, 0)), MASKED_BUCKET)
    return np.stack([prev, own]).astype(np.int32)


def _bias_kernel(idx_ref, rb_ref, o_ref):
    h = pl.program_id(0)
    idx = idx_ref[...]
    acc = jnp.full(idx.shape, -jnp.inf, F32)
    for bkt in range(REL_BUCKETS):
        acc = jnp.where(idx == bkt, rb_ref[bkt, h], acc)
    o_ref[0] = acc * LOG2E


def _moba_bias_tiles(rel_bias):
    idx = jnp.asarray(_moba_bucket_table())
    shp = (2, MOBA_BLOCK, MOBA_BLOCK)
    return pl.pallas_call(
        _bias_kernel,
        out_shape=jax.ShapeDtypeStruct((HEADS,) + shp, F32),
        grid=(HEADS,),
        in_specs=[pl.BlockSpec(shp, lambda h: (0, 0, 0)),
                  pl.BlockSpec(memory_space=pltpu.SMEM)],
        out_specs=pl.BlockSpec((1,) + shp, lambda h: (h, 0, 0, 0)),
        name="moba_bias",
    )(idx, rel_bias)


def _moba_kernel(q_ref, k_ref, v_ref, gate_ref, bias_ref, rb_ref, o_ref,
                 kmean_ref, kbf_ref, vt_ref, sel_ref, m_ref, l_ref, acc_ref, sc_ref, *, nb, nseq):
    BLK = MOBA_BLOCK
    ib = pl.program_id(1)
    heads = range(nseq * HEADS)
    seq_of = lambda u: u // HEADS
    rows_of = lambda u: slice((u % HEADS) * HDIM, (u % HEADS + 1) * HDIM)
    neg_inf = -jnp.inf
    not_selected = -1e30

    @pl.when(ib == 0)
    def _():
        kmean_ref[...] = jnp.zeros_like(kmean_ref)
        for s in range(nseq):
            for n in range(nb):
                blk = slice(n * BLK, (n + 1) * BLK)
                kblk = k_ref[s, blk, :]
                kmean_ref[s, n:n + 1, :] = jnp.mean(kblk, axis=0, keepdims=True)
                kbf_ref[s, n] = kblk.astype(BF16)
                vt_ref[s, n, 0:GROUP, :] = v_ref[s, blk, :].T.astype(BF16)
                vt_ref[s, n, GROUP:GROUP + 16, :] = jnp.ones((16, BLK), BF16)

    lane_head = _iota((1, GROUP), 1) // HDIM
    blk_id = _iota((16, BLK), 0)
    q_heads = []
    for s in range(nseq):
        q = q_ref[s] * (HDIM ** -0.5 * LOG2E)
        q_bf = q.astype(BF16)
        q_s = _split(q)
        q_heads += [jnp.where(lane_head == h, q_bf, jnp.zeros_like(q_bf)) for h in range(HEADS)]
        km = kmean_ref[s]
        gates = _dot3(_nt, _split(jnp.concatenate([jnp.where(lane_head == h, km, 0.0) for h in range(HEADS)],
                                                  axis=0)), q_s)
        for h in range(HEADS):
            g = jnp.where(blk_id < ib, gates[16 * h:16 * (h + 1)], neg_inf)
            ahead = jnp.zeros((16, BLK), F32)
            for m in range(nb):
                g_m = g[m:m + 1, :]
                ahead = ahead + jnp.where((g_m > g) | ((g_m == g) & (blk_id > m)), 1.0, 0.0)
            sel_ref[s * HEADS + h] = jnp.where((ahead < MOBA_TOPK) & (g > neg_inf), 1.0, 0.0)

    def raw_scores(n, u):
        return _nt(kbf_ref[seq_of(u), n], q_heads[u])

    def attend(blocks, first, scores_slot=None, prefetch=None):
        if scores_slot is None:
            scores = [[raw_scores(n, h) for n, _, _ in blocks] for h in heads]
        else:
            scores = [[sc_ref[scores_slot, j, h] for j in range(len(blocks))] for h in heads]
        if prefetch is not None:
            slot, nxt = prefetch
            for j, n in enumerate(nxt):
                for h in heads:
                    sc_ref[slot, j, h] = raw_scores(n, h)
        probs, alphas = [], []
        for h in heads:
            ss, tops = [], []
            for s, (_, tile, row) in zip(scores[h], blocks):
                s = s if tile is None else s + tile[h]
                top = jnp.max(s, axis=0, keepdims=True)
                ss.append(s)
                tops.append(top if row is None else top + row[h])
            m_new = functools.reduce(jnp.maximum, tops)
            if not first:
                m_old = m_ref[h]
                m_new = jnp.maximum(m_old, m_new)
                alphas.append(jnp.exp2(m_old - m_new))
            m_ref[h] = m_new
            probs.append([jnp.exp2(s + ((-m_new) if row is None else (row[h] - m_new))).astype(BF16)
                          for s, (_, _, row) in zip(ss, blocks)])
        for h in heads:
            pv = None
            for p, (n, _, _) in zip(probs[h], blocks):
                vt = vt_ref.at[seq_of(h), n]
                lhs = jnp.concatenate([vt[rows_of(h), :], vt[GROUP:GROUP + 16, :]], axis=0)
                part = _nn(lhs, p)
                pv = part if pv is None else pv + part
            acc = acc_ref.at[seq_of(h)]
            if first:
                acc[rows_of(h), :] = pv[0:HDIM]
                l_ref[h] = pv[HDIM:HDIM + 1]
            else:
                acc[rows_of(h), :] = alphas[h] * acc[rows_of(h), :] + pv[0:HDIM]
                l_ref[h] = alphas[h] * l_ref[h] + pv[HDIM:HDIM + 1]

    def selected_row(h, n, valid):
        row = jnp.where(sel_ref[h, pl.ds(n, 1), :] > 0.5, 0.0, not_selected)
        return jnp.where(valid, row, not_selected)

    prev = jnp.maximum(ib - 1, 0)
    far_pairs = [[n] if n + 1 >= nb - 2 else [n, n + 1] for n in range(0, nb - 2, 2)]
    attend([(ib, [bias_ref[h % HEADS, 1] for h in heads], None),
            (prev, [bias_ref[h % HEADS, 0] for h in heads], [selected_row(h, prev, ib >= 1) for h in heads])],
           first=True)

    n_far = ib - 1
    for k, pair in enumerate(far_pairs):
        @pl.when(pair[0] < n_far)
        def _(k=k, pair=pair):
            attend([(j, None, [(selected_row(h, j, j < n_far) + rb_ref[REL_BUCKETS - 1, h % HEADS]) * LOG2E
                               for h in heads])
                    for j in pair], first=False, scores_slot=(k % 2) if k > 0 else None,
                   prefetch=((k + 1) % 2, far_pairs[k + 1]) if k + 1 < len(far_pairs) else None)

    for h in heads:
        acc = acc_ref.at[seq_of(h)]
        acc[rows_of(h), :] = acc[rows_of(h), :] / l_ref[h]
    for s in range(nseq):
        o_ref[s] = (acc_ref[s].T * _silu(gate_ref[s])).astype(BF16)


def _moba(p3, bias_tiles, rel_bias, nseq=2):
    b, t, _ = p3.shape
    BLK = MOBA_BLOCK
    nb = t // BLK
    assert 2 <= nb <= 16 and t % BLK == 0 and b % nseq == 0
    base = 1 * 4
    tile = lambda c: pl.BlockSpec((nseq, BLK, GROUP), lambda i, j, c=c: (i, j, c))
    seq = lambda c: pl.BlockSpec((nseq, t, GROUP), lambda i, j, c=c: (i, 0, c))
    pairs = nseq * HEADS
    return pl.pallas_call(
        functools.partial(_moba_kernel, nb=nb, nseq=nseq),
        out_shape=jax.ShapeDtypeStruct((b, t, GROUP), BF16),
        grid=(b // nseq, nb),
        in_specs=[tile(base), seq(base + 1), seq(base + 2), tile(base + 3),
                  pl.BlockSpec(bias_tiles.shape, lambda i, j: (0, 0, 0, 0)),
                  pl.BlockSpec(memory_space=pltpu.SMEM)],
        out_specs=pl.BlockSpec((nseq, BLK, GROUP), lambda i, j: (i, j, 0)),
        scratch_shapes=[pltpu.VMEM((nseq, 16, GROUP), F32),
                        pltpu.VMEM((nseq, nb, BLK, GROUP), BF16),
                        pltpu.VMEM((nseq, nb, GROUP + 16, BLK), BF16),
                        pltpu.VMEM((pairs, 16, BLK), F32),
                        pltpu.VMEM((pairs, 1, BLK), F32),
                        pltpu.VMEM((pairs, 1, BLK), F32),
                        pltpu.VMEM((nseq, GROUP, BLK), F32),
                        pltpu.VMEM((2, 2, pairs, BLK, BLK), F32)],
        compiler_params=pltpu.CompilerParams(
            dimension_semantics=("arbitrary", "arbitrary"), vmem_limit_bytes=VMEM_LIMIT_V7X),
        name="moba",
    )(p3, p3, p3, p3, bias_tiles, rel_bias)


def _ret_tables(t):
    half = HDIM // 2
    theta = 1.0 / (10000.0 ** np.linspace(0.0, 1.0, half))
    pos = np.arange(t, dtype=np.float64)
    ang = pos[:, None] * theta[None, :]
    sin, cos = np.sin(ang), np.cos(ang)
    cos2 = np.tile(np.repeat(cos, 2, axis=1), (1, HEADS))
    sin2 = np.tile(np.stack([-sin, sin], axis=-1).reshape(t, HDIM), (1, HEADS))
    log_gamma = np.log(1.0 - 2.0 ** (-5.0 - np.arange(HEADS, dtype=np.float64)))
    idx = np.arange(RET_CHUNK, dtype=np.float64)
    diff = idx[:, None] - idx[None, :]
    decay_intra = np.where(diff >= 0, np.exp(log_gamma[:, None, None] * np.maximum(diff, 0.0)), 0.0)
    decay_intra = np.concatenate(list(decay_intra), axis=1)
    q_decay = np.exp(log_gamma[:, None] * (idx + 1.0))
    k_decay = np.exp(log_gamma[:, None] * (RET_CHUNK - 1.0 - idx))
    chunk_decay = np.exp(log_gamma * RET_CHUNK)
    lanes = lambda a: np.repeat(a.T, HDIM, axis=1)
    cd = np.repeat(chunk_decay, HDIM).reshape(1, GROUP)
    return tuple(jnp.asarray(a, F32) for a in (cos2, sin2, decay_intra, lanes(q_decay), lanes(k_decay), cd))


def _ret_kernel(q_ref, k_ref, v_ref, gate_ref, cos_ref, sin_ref, di_ref, qd_ref, kd_ref, cd_ref,
                o_ref, state_ref, *, tt):
    C = RET_CHUNK
    n_chunks = tt // C
    chunks = range(n_chunks)
    heads = range(HEADS)
    b = pl.program_id(1)

    @pl.when(pl.program_id(0) == 0)
    def _():
        state_ref[b] = jnp.zeros((GROUP, GROUP), F32)

    even = (_iota((tt, GROUP), 1) % 2) == 0

    def rotate(x):
        swapped = jnp.where(even, pltpu.roll(x, GROUP - 1, axis=1), pltpu.roll(x, 1, axis=1))
        return x * cos_ref[...] + swapped * sin_ref[...]

    q = rotate(q_ref[0])
    k = rotate(k_ref[0]) * (HDIM ** -0.5)
    lane_head = _iota((1, GROUP), 1) // HDIM
    same_head = (_iota((GROUP, GROUP), 0) // HDIM) == (_iota((GROUP, GROUP), 1) // HDIM)
    rows = lambda c: slice(c * C, (c + 1) * C)

    q_b = [q[rows(c)].astype(BF16) for c in chunks]
    k_b = [k[rows(c)].astype(BF16) for c in chunks]
    v_b = [v_ref[0, rows(c), :].astype(BF16) for c in chunks]
    kd_b = [(k[rows(c)] * kd_ref[...]).astype(BF16) for c in chunks]
    qd_b = [(q[rows(c)] * qd_ref[...]).astype(BF16) for c in chunks]

    row_head_is_lane_head = (_iota((HEADS * C, GROUP), 0) // C) == (_iota((HEADS * C, GROUP), 1) // HDIM)

    def per_head_rows(x):
        return jnp.where(row_head_is_lane_head, jnp.concatenate([x] * HEADS, axis=0), jnp.zeros((), x.dtype))

    inner = [(_nt(q_b[c], per_head_rows(k_b[c])) * di_ref[...]).astype(BF16) for c in chunks]
    kv = [jnp.where(same_head, _tn(kd_b[c], v_b[c]), 0.0) for c in chunks]
    intra = [_nn(inner[c], per_head_rows(v_b[c])) for c in chunks]
    state = state_ref[b]
    states = []
    for c in chunks:
        states.append(state.astype(BF16))
        state = state * cd_ref[...] + kv[c]
    state_ref[b] = state
    y = jnp.concatenate([intra[c] + _nn(qd_b[c], states[c]) for c in chunks], axis=0)
    mean_sq = _head_sum(y * y, _head_sum_matrix().astype(BF16)) * (1.0 / HDIM)
    o_ref[0] = (y * lax.rsqrt(mean_sq + NORM_EPS) * _silu(gate_ref[0])).astype(BF16)


def _ret(p3, tables, tt=512):
    b, t, _ = p3.shape
    base = 2 * 4
    cos2, sin2, di, qd, kd, cd = tables
    blk = lambda c: pl.BlockSpec((1, tt, GROUP), lambda j, i, c=c: (i, j, c))
    full2 = lambda a: pl.BlockSpec(a.shape, lambda j, i: (0, 0))
    return pl.pallas_call(
        functools.partial(_ret_kernel, tt=tt),
        out_shape=jax.ShapeDtypeStruct((b, t, GROUP), BF16),
        grid=(t // tt, b),
        in_specs=[blk(base), blk(base + 1), blk(base + 2), blk(base + 3),
                  pl.BlockSpec((tt, GROUP), lambda j, i: (j, 0)),
                  pl.BlockSpec((tt, GROUP), lambda j, i: (j, 0)),
                  full2(di), full2(qd), full2(kd), full2(cd)],
        out_specs=pl.BlockSpec((1, tt, GROUP), lambda j, i: (i, j, 0)),
        scratch_shapes=[pltpu.VMEM((b, GROUP, GROUP), F32)],
        compiler_params=pltpu.CompilerParams(
            dimension_semantics=("arbitrary", "arbitrary"), vmem_limit_bytes=VMEM_LIMIT_V7X),
        name="retnet",
    )(p3, p3, p3, p3, cos2, sin2, di, qd, kd, cd)


def _reorder_w_in(w):
    wt = jnp.swapaxes(w, 1, 2).astype(BF16)
    g3, g4 = 3 * GROUP, 4 * GROUP
    lora_end = g3 + 2 * LORA
    conv_start = lora_end + GROUP
    pad = jnp.zeros((w.shape[0], 128 - 2 * LORA, w.shape[1]), BF16)
    return jnp.concatenate([wt[:, :g3], wt[:, lora_end:conv_start], wt[:, conv_start + g4:],
                            wt[:, g3:lora_end], pad, wt[:, conv_start:conv_start + g4]], axis=1)


def kernel(x, norm_w, w_in, w_out, rwkv_mu, rwkv_w0, rwkv_w2, rwkv_a0, rwkv_a2, rwkv_k_k, rwkv_k_a,
           rwkv_r_k, rwkv_lnx_w, rwkv_lnx_b, conv_w, rel_bias, final_norm_w):
    b, t, d = x.shape
    depth = w_in.shape[0]
    bias_tiles = _moba_bias_tiles(rel_bias)
    ret_tables = _ret_tables(t)
    x2 = x.reshape(b * t, d)
    w_in_bf = _reorder_w_in(w_in)
    w_out_bf = w_out.astype(BF16)
    for l in range(depth):
        p2, y_conv = _inproj(x2, norm_w[l], w_in_bf, conv_w[l], l, seq_len=t)
        p3 = p2.reshape(b, t, P_COLS)
        y_rwkv = _rwkv(p3, rwkv_mu[l], rwkv_w0[l], rwkv_w2[l], rwkv_a0[l], rwkv_a2[l], rwkv_k_k[l],
                       rwkv_k_a[l], rwkv_r_k[l].reshape(GROUP), rwkv_lnx_w[l], rwkv_lnx_b[l])
        y_moba = _moba(p3, bias_tiles, rel_bias)
        y_ret = _ret(p3, ret_tables)
        flat = lambda y: y.reshape(b * t, GROUP)
        x2 = _outproj(x2, [flat(y_rwkv), y_conv, flat(y_moba), flat(y_ret)], w_out_bf, l,
                      final_norm_w, final=(l == depth - 1))
    return x2.reshape(b, t, d)
```

```python
import functools
import math

import numpy as np
import jax
import jax.numpy as jnp
from jax import lax
from jax.experimental import pallas as pl
from jax.experimental.pallas import tpu as pltpu

F32 = jnp.float32
BF16 = jnp.bfloat16

GROUP = 256
HEADS = 4
HDIM = 64
LORA = 32
DECAY_SCALE = math.exp(-0.5)
LOG2E = 1.0 / math.log(2.0)
LNX_EPS = 64e-5
NORM_EPS = 1e-6
CONV_TAPS = 3
MOBA_BLOCK = 256
MOBA_TOPK = 3
RET_CHUNK = 128
REL_BUCKETS = 32
REL_MAX_DIST = 128
RWKV_CHUNK = 64

P_COLS = 3 * 4 * GROUP + 128
LORA_COL = 3 * 4 * GROUP
W_ROWS = P_COLS + 4 * GROUP

VMEM_LIMIT_V7X = 48 * 1024 * 1024
BF16_ROWS = 16


def _nn(a, b, precision=None):
    return lax.dot_general(a, b, (((1,), (0,)), ((), ())), precision=precision,
                           preferred_element_type=F32)


def _nt(a, b, precision=None):
    return lax.dot_general(a, b, (((1,), (1,)), ((), ())), precision=precision,
                           preferred_element_type=F32)


def _tn(a, b, precision=None):
    return lax.dot_general(a, b, (((0,), (0,)), ((), ())), precision=precision,
                           preferred_element_type=F32)


def _split(x):
    hi = x.astype(BF16)
    lo = (x - hi.astype(F32)).astype(BF16)
    return hi, lo


def _dot3(dot, a, b):
    out_axis = 1 if dot is _tn else 0
    m = a[0].shape[out_axis]
    both = dot(jnp.concatenate([a[0], a[1]], axis=out_axis), b[0])
    return both[:m] + both[m:] + dot(a[0], b[1])


def _psl(pair, rows, cols):
    return pair[0][rows, cols], pair[1][rows, cols]


def _pcat(pairs, axis):
    return (jnp.concatenate([p[0] for p in pairs], axis=axis),
            jnp.concatenate([p[1] for p in pairs], axis=axis))


def _head_sum(x, esum_bf16):
    hi, lo = _split(x)
    m = x.shape[0]
    both = _nn(jnp.concatenate([hi, lo], axis=0), esum_bf16)
    return both[:m] + both[m:]


def _iota(shape, dim):
    return lax.broadcasted_iota(jnp.int32, shape, dim)


def _head_sum_matrix():
    r = _iota((GROUP, GROUP), 0) // HDIM
    c = _iota((GROUP, GROUP), 1) // HDIM
    return jnp.where(r == c, 1.0, 0.0).astype(F32)


def _silu(x):
    return x * jax.nn.sigmoid(x)


def _gated_conv(p, cw_ref, tail_ref, first_tile):
    @pl.when(first_tile)
    def _():
        tail_ref[...] = jnp.zeros_like(tail_ref)

    n_rows = p.shape[0]
    u = p[:, GROUP:2 * GROUP] * p[:, 2 * GROUP:3 * GROUP]
    rows = _iota((n_rows, 1), 0)
    u1 = jnp.where(rows == 0, tail_ref[1:2, :], pltpu.roll(u, 1, axis=0))
    u2 = jnp.where(rows == 0, tail_ref[0:1, :],
                   jnp.where(rows == 1, tail_ref[1:2, :], pltpu.roll(u, 2, axis=0)))
    tail_ref[0:2, :] = u[n_rows - 2:n_rows, :]
    y = u2 * cw_ref[0:1, :] + u1 * cw_ref[1:2, :] + u * cw_ref[2:3, :]
    return p[:, 0:GROUP] * y * _silu(p[:, 3 * GROUP:4 * GROUP])


def _inproj_kernel(x_ref, nw_ref, w_ref, cw_ref, o_ref, yconv_ref, tail_ref, *, tiles_per_seq):
    x = x_ref[...]
    h = x * lax.rsqrt(jnp.mean(x * x, axis=-1, keepdims=True) + NORM_EPS) * nw_ref[...]
    proj = _nt(h.astype(BF16), w_ref[0])
    o_ref[...] = proj[:, 0:P_COLS]
    y_conv = _gated_conv(proj[:, P_COLS:W_ROWS], cw_ref, tail_ref, pl.program_id(0) % tiles_per_seq == 0)
    yconv_ref[...] = y_conv.astype(BF16)


def _inproj(x2d, norm_w, w_bf16, conv_w, layer, seq_len, tm=512):
    m, d = x2d.shape
    assert seq_len % tm == 0 and w_bf16.shape[1] == W_ROWS
    return pl.pallas_call(
        functools.partial(_inproj_kernel, tiles_per_seq=seq_len // tm),
        out_shape=(jax.ShapeDtypeStruct((m, P_COLS), F32),
                   jax.ShapeDtypeStruct((m, GROUP), BF16)),
        grid=(m // tm,),
        in_specs=[pl.BlockSpec((tm, d), lambda i: (i, 0)),
                  pl.BlockSpec((1, d), lambda i: (0, 0)),
                  pl.BlockSpec((1, W_ROWS, d), lambda i: (layer, 0, 0)),
                  pl.BlockSpec((CONV_TAPS, GROUP), lambda i: (0, 0))],
        out_specs=(pl.BlockSpec((tm, P_COLS), lambda i: (i, 0)),
                   pl.BlockSpec((tm, GROUP), lambda i: (i, 0))),
        scratch_shapes=[pltpu.VMEM((8, GROUP), F32)],
        compiler_params=pltpu.CompilerParams(
            dimension_semantics=("arbitrary",), vmem_limit_bytes=VMEM_LIMIT_V7X),
        name="inproj",
    )(x2d, norm_w.reshape(1, d), w_bf16, conv_w)


def _outproj_kernel(x_ref, y0_ref, y1_ref, y2_ref, y3_ref, w_ref, fw_ref, o_ref, *, final):
    acc = x_ref[...]
    for g, y_ref in enumerate((y0_ref, y1_ref, y2_ref, y3_ref)):
        acc = acc + _nn(y_ref[...], w_ref[0, g * GROUP:(g + 1) * GROUP, :])
    if final:
        acc = acc * lax.rsqrt(jnp.mean(acc * acc, axis=-1, keepdims=True) + NORM_EPS) * fw_ref[...]
    o_ref[...] = acc


def _outproj(x2d, ys, w_bf16, layer, final_w, final, tm=1024):
    m, d = x2d.shape
    yspec = pl.BlockSpec((tm, GROUP), lambda i: (i, 0))
    return pl.pallas_call(
        functools.partial(_outproj_kernel, final=final),
        out_shape=jax.ShapeDtypeStruct((m, d), F32),
        grid=(m // tm,),
        in_specs=[pl.BlockSpec((tm, d), lambda i: (i, 0)), yspec, yspec, yspec, yspec,
                  pl.BlockSpec((1, 4 * GROUP, d), lambda i: (layer, 0, 0)),
                  pl.BlockSpec((1, d), lambda i: (0, 0))],
        out_specs=pl.BlockSpec((tm, d), lambda i: (i, 0)),
        compiler_params=pltpu.CompilerParams(
            dimension_semantics=("arbitrary",), vmem_limit_bytes=VMEM_LIMIT_V7X),
        name="outproj_final" if final else "outproj",
    )(x2d, *ys, w_bf16, final_w.reshape(1, d))


def _rwkv_kernel(rkv_ref, gate_ref, lora_ref, mu_ref, mul_ref, w0_ref, w2_ref, a0_ref, a2_ref,
                 kk_ref, ka_ref, rk_ref, lnw_ref, lnb_ref, o_ref,
                 state_ref, prev_ref, prevl_ref, *, tt, nseq):
    C = RWKV_CHUNK
    n_rows = nseq * tt
    t_idx = pl.program_id(1)

    @pl.when(t_idx == 0)
    def _():
        state_ref[...] = jnp.zeros_like(state_ref)
        prev_ref[...] = jnp.zeros_like(prev_ref)
        prevl_ref[...] = jnp.zeros_like(prevl_ref)

    esum = _head_sum_matrix().astype(BF16)
    per_seq = tt // C
    row = _iota((tt, 1), 0)

    def prepare(s):
        p = rkv_ref[s]
        lo = lora_ref[s]
        p_sh = jnp.where(row == 0, prev_ref[s:s + 1, :], pltpu.roll(p, 1, axis=0))
        lo_sh = jnp.where(row == 0, prevl_ref[s:s + 1, :], pltpu.roll(lo, 1, axis=0))
        prev_ref[s:s + 1, :] = p[tt - 1:tt, :]
        prevl_ref[s:s + 1, :] = lo[tt - 1:tt, :]
        p = p + (p_sh - p) * mu_ref[...]
        lo = lo + (lo_sh - lo) * mul_ref[...]
        r = p[:, 0:GROUP]
        k = p[:, GROUP:2 * GROUP]
        v = p[:, 2 * GROUP:3 * GROUP]
        lw = -DECAY_SCALE * jax.nn.sigmoid(
            w0_ref[...] + _dot3(_nn, _split(jnp.tanh(lo)), _split(w2_ref[...])))
        rate = jax.nn.sigmoid(a0_ref[...] + _dot3(_nn, _split(lo), _split(a2_ref[...])))
        kk = k * kk_ref[...]
        kk = kk / jnp.maximum(jnp.sqrt(_head_sum(kk * kk, esum)), 1e-12)
        k2 = k * (1.0 + (rate - 1.0) * ka_ref[...])
        bonus = _head_sum(r * k2 * rk_ref[...], esum) * v
        bvec = kk * rate
        avec = -kk
        rowmod = row % C
        cum = lw
        for sh in (1, 2, 4, 8, 16, 32):
            cum = cum + jnp.where(rowmod >= sh, pltpu.roll(cum, sh, axis=0), 0.0)
        tot_rows = [cum[c * C + C - 1:c * C + C, :] for c in range(per_seq)]
        tot = jnp.concatenate([jnp.broadcast_to(tr, (C, GROUP)) for tr in tot_rows], axis=0)
        e_neg = jnp.exp(-cum)
        e_end = jnp.exp(tot - cum)
        whole = dict(r_t=r * jnp.exp(cum), a_t=avec * jnp.exp(cum - lw), k_t=k2 * e_neg, b_t=bvec * e_neg,
                     k_h=k2 * e_end, b_h=bvec * e_end, v_s=v)
        pairs = {name: _split(x) for name, x in whole.items()}
        per_chunk = {name: [_psl(pr, slice(c * C, (c + 1) * C), slice(None)) for c in range(per_seq)]
                     for name, pr in pairs.items()}
        return per_chunk, tot_rows, bonus, [v[c * C:(c + 1) * C, :] for c in range(per_seq)]

    same_head = (_iota((GROUP, GROUP), 0) // HDIM) == (_iota((GROUP, GROUP), 1) // HDIM)

    def bd(pr):
        return tuple(jnp.where(same_head, jnp.concatenate([part] * HEADS, axis=0), jnp.zeros((), part.dtype))
                     for part in pr)

    t_i = _iota((2 * C, GROUP), 0)
    s_i = _iota((2 * C, GROUP), 1) % C
    tri = ((t_i < C) & (s_i < t_i)) | ((t_i >= C) & (s_i <= t_i - C))
    eye = jnp.where(_iota((C, GROUP), 0) == _iota((C, GROUP), 1) % C, 1.0, 0.0).astype(F32)

    chunks = range(nseq * per_seq)
    every = slice(None)
    top = lambda pr: _psl(pr, slice(0, C), every)
    bottom = lambda pr: _psl(pr, slice(C, 2 * C), every)

    ops = {name: [] for name in ("r_t", "a_t", "k_t", "b_t", "k_h", "b_h", "v_s")}
    tot_rows, bonus, ab_rb, ak_rk, v_f = [], [], [], [], []
    for s in range(nseq):
        per_chunk, tots, bonus_s, v_chunks = prepare(s)
        for name in ops:
            ops[name] += per_chunk[name]
        tot_rows += tots
        v_f += v_chunks
        bonus.append(bonus_s)
        for c in range(s * per_seq, (s + 1) * per_seq):
            ar = _pcat([ops["a_t"][c], ops["r_t"][c]], 0)
            ab_rb.append(jnp.where(tri, _dot3(_nt, ar, bd(ops["b_t"][c])), 0.0))
            ak_rk.append(_split(jnp.where(tri, _dot3(_nt, ar, bd(ops["k_t"][c])), 0.0)))
    rows = lambda name, c: ops[name][c]
    bonus = jnp.concatenate(bonus, axis=0)
    bd_v = [bd(rows("v_s", c)) for c in chunks]
    a_ab = [ab_rb[c][0:C] for c in chunks]
    ab_rb = [_split(x) for x in ab_rb]
    t_row = _iota((C, GROUP), 0)
    s_col = _iota((C, GROUP), 1) % C

    def below_diagonal(n):
        return ((t_row // (2 * n)) == (s_col // (2 * n))) & ((t_row % (2 * n)) >= n) & ((s_col % (2 * n)) < n)

    inv = [eye + jnp.where(below_diagonal(1), a_ab[c], 0.0) for c in chunks]
    n = 2
    while n < C:
        off = below_diagonal(n)
        inv_s = [_split(inv[c]) for c in chunks]
        left = [_split(_dot3(_nn, inv_s[c], bd(_split(jnp.where(off, a_ab[c], 0.0))))) for c in chunks]
        inv = [inv[c] + _dot3(_nn, left[c], bd(inv_s[c])) for c in chunks]
        n *= 2
    inv = [_split(x) for x in inv]
    t_ak = [_split(_dot3(_nn, inv[c], bd(top(ak_rk[c])))) for c in chunks]
    t_a_f = [_dot3(_nn, inv[c], bd(rows("a_t", c))) for c in chunks]
    t_a = [_split(x) for x in t_a_f]
    free = [_dot3(_nn, _pcat([t_ak[c], bottom(ak_rk[c])], 0), bd_v[c]) for c in chunks]

    def head_transpose(x):
        xt = x.T
        return jnp.concatenate([xt[h * HDIM:(h + 1) * HDIM, :] for h in range(HEADS)], axis=1)

    bd_bh = [bd(rows("b_h", c)) for c in chunks]
    m_s = [bd(_split(_dot3(_nn, _split(head_transpose(t_a_f[c])), bd_bh[c]))) for c in chunks]
    g = [_dot3(_nn, _pcat([_split(head_transpose(v_f[c])), _split(head_transpose(free[c][0:C]))], 1),
               _pcat([bd(rows("k_h", c)), bd_bh[c]], 0)) for c in chunks]
    state = [state_ref[s] for s in range(nseq)]
    from_state = {}
    for step in range(per_seq):
        for s in range(nseq):
            c = s * per_seq + step
            st_s = _split(state[s])
            state[s] = state[s] * jnp.exp(tot_rows[c]) + _dot3(_nn, st_s, m_s[c]) + g[c]
            from_state[c] = _dot3(_nt, _pcat([t_a[c], rows("r_t", c)], 0), bd(st_s))
    for s in range(nseq):
        state_ref[s] = state[s]
    y = jnp.concatenate(
        [free[c][C:2 * C] + from_state[c][C:2 * C]
         + _dot3(_nn, bottom(ab_rb[c]), bd(_split(free[c][0:C] + from_state[c][0:C]))) for c in chunks], axis=0)
    mean = _head_sum(y, esum) * (1.0 / HDIM)
    yc = y - mean
    var = _head_sum(yc * yc, esum) * (1.0 / HDIM)
    y = yc * lax.rsqrt(var + LNX_EPS) * lnw_ref[...] + lnb_ref[...] + bonus
    o_ref[...] = (y * _silu(gate_ref[...].reshape(n_rows, GROUP))).astype(BF16).reshape(nseq, tt, GROUP)


def _rwkv(p3, mu, w0, w2, a0, a2, k_k, k_a, r_k, lnx_w, lnx_b, tt=256, nseq=4):
    b, t, _ = p3.shape
    assert b % nseq == 0 and t % tt == 0 and nseq <= 8
    mu_main = mu[:3 * GROUP].reshape(1, 3 * GROUP)
    mu_lora = jnp.concatenate([mu[3 * GROUP:], jnp.zeros((128 - 2 * LORA,), F32)]).reshape(1, 128)
    w2p = jnp.zeros((128, GROUP), F32).at[0:LORA].set(w2)
    a2p = jnp.zeros((128, GROUP), F32).at[LORA:2 * LORA].set(a2)
    row = lambda a: a.reshape(1, GROUP)
    vec = pl.BlockSpec((1, GROUP), lambda i, j: (0, 0))
    return pl.pallas_call(
        functools.partial(_rwkv_kernel, tt=tt, nseq=nseq),
        out_shape=jax.ShapeDtypeStruct((b, t, GROUP), BF16),
        grid=(b // nseq, t // tt),
        in_specs=[pl.BlockSpec((nseq, tt, 3 * GROUP), lambda i, j: (i, j, 0)),
                  pl.BlockSpec((nseq, tt, GROUP), lambda i, j: (i, j, 3)),
                  pl.BlockSpec((nseq, tt, 128), lambda i, j: (i, j, LORA_COL // 128)),
                  pl.BlockSpec((1, 3 * GROUP), lambda i, j: (0, 0)),
                  pl.BlockSpec((1, 128), lambda i, j: (0, 0)),
                  vec,
                  pl.BlockSpec((128, GROUP), lambda i, j: (0, 0)),
                  vec,
                  pl.BlockSpec((128, GROUP), lambda i, j: (0, 0)),
                  vec, vec, vec, vec, vec],
        out_specs=pl.BlockSpec((nseq, tt, GROUP), lambda i, j: (i, j, 0)),
        scratch_shapes=[pltpu.VMEM((nseq, HDIM, GROUP), F32),
                        pltpu.VMEM((8, 3 * GROUP), F32),
                        pltpu.VMEM((8, 128), F32)],
        compiler_params=pltpu.CompilerParams(
            dimension_semantics=("arbitrary", "arbitrary"), vmem_limit_bytes=VMEM_LIMIT_V7X),
        name="rwkv7",
    )(p3, p3, p3, mu_main, mu_lora, row(w0), w2p, row(a0), a2p,
      row(k_k), row(k_a), row(r_k), row(lnx_w), row(lnx_b))


def _t5_bucket_np(dist):
    max_exact = REL_BUCKETS // 2
    d_f = np.maximum(dist, 1).astype(np.float32)
    large = max_exact + (np.log(d_f / np.float32(max_exact)) / np.float32(math.log(REL_MAX_DIST / max_exact))
                         * np.float32(REL_BUCKETS - max_exact)).astype(np.int32)
    large = np.minimum(large, REL_BUCKETS - 1)
    return np.where(dist < max_exact, dist, large).astype(np.int32)


MASKED_BUCKET = REL_BUCKETS


def _moba_bucket_table():
    keys = np.arange(MOBA_BLOCK)[:, None]
    queries = np.arange(MOBA_BLOCK)[None, :]
    prev = _t5_bucket_np(queries + MOBA_BLOCK - keys)
    own = np.where(keys <= queries, _t5_bucket_np(np.maximum(queries - keys, 0)), MASKED_BUCKET)
    return np.stack([prev, own]).astype(np.int32)


def _bias_kernel(idx_ref, rb_ref, o_ref):
    h = pl.program_id(0)
    idx = idx_ref[...]
    acc = jnp.full(idx.shape, -jnp.inf, F32)
    for bkt in range(REL_BUCKETS):
        acc = jnp.where(idx == bkt, rb_ref[bkt, h], acc)
    o_ref[0] = acc * LOG2E


def _moba_bias_tiles(rel_bias):
    idx = jnp.asarray(_moba_bucket_table())
    shp = (2, MOBA_BLOCK, MOBA_BLOCK)
    return pl.pallas_call(
        _bias_kernel,
        out_shape=jax.ShapeDtypeStruct((HEADS,) + shp, F32),
        grid=(HEADS,),
        in_specs=[pl.BlockSpec(shp, lambda h: (0, 0, 0)),
                  pl.BlockSpec(memory_space=pltpu.SMEM)],
        out_specs=pl.BlockSpec((1,) + shp, lambda h: (h, 0, 0, 0)),
        name="moba_bias",
    )(idx, rel_bias)


def _moba_kernel(q_ref, k_ref, v_ref, gate_ref, bias_ref, rb_ref, o_ref,
                 kmean_ref, kbf_ref, vt_ref, sel_ref, m_ref, l_ref, acc_ref, sc_ref, *, nb, nseq):
    BLK = MOBA_BLOCK
    ib = pl.program_id(1)
    heads = range(nseq * HEADS)
    seq_of = lambda u: u // HEADS
    rows_of = lambda u: slice((u % HEADS) * HDIM, (u % HEADS + 1) * HDIM)
    neg_inf = -jnp.inf
    not_selected = -1e30

    @pl.when(ib == 0)
    def _():
        kmean_ref[...] = jnp.zeros_like(kmean_ref)
        for s in range(nseq):
            for n in range(nb):
                blk = slice(n * BLK, (n + 1) * BLK)
                kblk = k_ref[s, blk, :]
                kmean_ref[s, n:n + 1, :] = jnp.mean(kblk, axis=0, keepdims=True)
                kbf_ref[s, n] = kblk.astype(BF16)
                vt_ref[s, n, 0:GROUP, :] = v_ref[s, blk, :].T.astype(BF16)
                vt_ref[s, n, GROUP:GROUP + BF16_ROWS, :] = jnp.ones((BF16_ROWS, BLK), BF16)

    lane_head = _iota((1, GROUP), 1) // HDIM
    nb_pad = BF16_ROWS
    blk_id = _iota((nb_pad, BLK), 0)
    q_heads = []
    for s in range(nseq):
        q = q_ref[s] * (HDIM ** -0.5 * LOG2E)
        q_bf = q.astype(BF16)
        q_s = _split(q)
        q_heads += [jnp.where(lane_head == h, q_bf, jnp.zeros_like(q_bf)) for h in range(HEADS)]
        km = kmean_ref[s]
        gates = _dot3(_nt, _split(jnp.concatenate([jnp.where(lane_head == h, km, 0.0) for h in range(HEADS)],
                                                  axis=0)), q_s)
        for h in range(HEADS):
            g = jnp.where(blk_id < ib, gates[nb_pad * h:nb_pad * (h + 1)], neg_inf)
            ahead = jnp.zeros((nb_pad, BLK), F32)
            for m in range(nb):
                g_m = g[m:m + 1, :]
                ahead = ahead + jnp.where((g_m > g) | ((g_m == g) & (blk_id > m)), 1.0, 0.0)
            sel_ref[s * HEADS + h] = jnp.where((ahead < MOBA_TOPK) & (g > neg_inf), 1.0, 0.0)

    def raw_scores(n, u):
        return _nt(kbf_ref[seq_of(u), n], q_heads[u])

    def attend(blocks, first, scores_slot=None, prefetch=None):
        if scores_slot is None:
            scores = [[raw_scores(n, h) for n, _, _ in blocks] for h in heads]
        else:
            scores = [[sc_ref[scores_slot, j, h] for j in range(len(blocks))] for h in heads]
        if prefetch is not None:
            slot, nxt = prefetch
            for j, n in enumerate(nxt):
                for h in heads:
                    sc_ref[slot, j, h] = raw_scores(n, h)
        probs, alphas = [], []
        for h in heads:
            ss, tops = [], []
            for s, (_, tile, row) in zip(scores[h], blocks):
                s = s if tile is None else s + tile[h]
                top = jnp.max(s, axis=0, keepdims=True)
                ss.append(s)
                tops.append(top if row is None else top + row[h])
            m_new = functools.reduce(jnp.maximum, tops)
            if not first:
                m_old = m_ref[h]
                m_new = jnp.maximum(m_old, m_new)
                alphas.append(jnp.exp2(m_old - m_new))
            m_ref[h] = m_new
            probs.append([jnp.exp2(s + ((-m_new) if row is None else (row[h] - m_new))).astype(BF16)
                          for s, (_, _, row) in zip(ss, blocks)])
        for h in heads:
            pv = None
            for p, (n, _, _) in zip(probs[h], blocks):
                vt = vt_ref.at[seq_of(h), n]
                lhs = jnp.concatenate([vt[rows_of(h), :], vt[GROUP:GROUP + BF16_ROWS, :]], axis=0)
                part = _nn(lhs, p)
                pv = part if pv is None else pv + part
            acc = acc_ref.at[seq_of(h)]
            if first:
                acc[rows_of(h), :] = pv[0:HDIM]
                l_ref[h] = pv[HDIM:HDIM + 1]
            else:
                acc[rows_of(h), :] = alphas[h] * acc[rows_of(h), :] + pv[0:HDIM]
                l_ref[h] = alphas[h] * l_ref[h] + pv[HDIM:HDIM + 1]

    def selected_row(h, n, valid):
        row = jnp.where(sel_ref[h, pl.ds(n, 1), :] > 0.5, 0.0, not_selected)
        return jnp.where(valid, row, not_selected)

    prev = jnp.maximum(ib - 1, 0)
    far_pairs = [[n] if n + 1 >= nb - 2 else [n, n + 1] for n in range(0, nb - 2, 2)]
    attend([(ib, [bias_ref[h % HEADS, 1] for h in heads], None),
            (prev, [bias_ref[h % HEADS, 0] for h in heads], [selected_row(h, prev, ib >= 1) for h in heads])],
           first=True)

    n_far = ib - 1
    for k, pair in enumerate(far_pairs):
        @pl.when(pair[0] < n_far)
        def _(k=k, pair=pair):
            attend([(j, None, [(selected_row(h, j, j < n_far) + rb_ref[REL_BUCKETS - 1, h % HEADS]) * LOG2E
                               for h in heads])
                    for j in pair], first=False, scores_slot=(k % 2) if k > 0 else None,
                   prefetch=((k + 1) % 2, far_pairs[k + 1]) if k + 1 < len(far_pairs) else None)

    for h in heads:
        acc = acc_ref.at[seq_of(h)]
        acc[rows_of(h), :] = acc[rows_of(h), :] / l_ref[h]
    for s in range(nseq):
        o_ref[s] = (acc_ref[s].T * _silu(gate_ref[s])).astype(BF16)


def _moba(p3, bias_tiles, rel_bias, nseq=2):
    b, t, _ = p3.shape
    BLK = MOBA_BLOCK
    nb = t // BLK
    assert 2 <= nb <= BF16_ROWS and t % BLK == 0 and b % nseq == 0
    base = 1 * 4
    tile = lambda c: pl.BlockSpec((nseq, BLK, GROUP), lambda i, j, c=c: (i, j, c))
    seq = lambda c: pl.BlockSpec((nseq, t, GROUP), lambda i, j, c=c: (i, 0, c))
    pairs = nseq * HEADS
    return pl.pallas_call(
        functools.partial(_moba_kernel, nb=nb, nseq=nseq),
        out_shape=jax.ShapeDtypeStruct((b, t, GROUP), BF16),
        grid=(b // nseq, nb),
        in_specs=[tile(base), seq(base + 1), seq(base + 2), tile(base + 3),
                  pl.BlockSpec(bias_tiles.shape, lambda i, j: (0, 0, 0, 0)),
                  pl.BlockSpec(memory_space=pltpu.SMEM)],
        out_specs=pl.BlockSpec((nseq, BLK, GROUP), lambda i, j: (i, j, 0)),
        scratch_shapes=[pltpu.VMEM((nseq, BF16_ROWS, GROUP), F32),
                        pltpu.VMEM((nseq, nb, BLK, GROUP), BF16),
                        pltpu.VMEM((nseq, nb, GROUP + BF16_ROWS, BLK), BF16),
                        pltpu.VMEM((pairs, BF16_ROWS, BLK), F32),
                        pltpu.VMEM((pairs, 1, BLK), F32),
                        pltpu.VMEM((pairs, 1, BLK), F32),
                        pltpu.VMEM((nseq, GROUP, BLK), F32),
                        pltpu.VMEM((2, 2, pairs, BLK, BLK), F32)],
        compiler_params=pltpu.CompilerParams(
            dimension_semantics=("arbitrary", "arbitrary"), vmem_limit_bytes=VMEM_LIMIT_V7X),
        name="moba",
    )(p3, p3, p3, p3, bias_tiles, rel_bias)


def _ret_tables(t):
    half = HDIM // 2
    theta = 1.0 / (10000.0 ** np.linspace(0.0, 1.0, half))
    pos = np.arange(t, dtype=np.float64)
    ang = pos[:, None] * theta[None, :]
    sin, cos = np.sin(ang), np.cos(ang)
    cos2 = np.tile(np.repeat(cos, 2, axis=1), (1, HEADS))
    sin2 = np.tile(np.stack([-sin, sin], axis=-1).reshape(t, HDIM), (1, HEADS))
    log_gamma = np.log(1.0 - 2.0 ** (-5.0 - np.arange(HEADS, dtype=np.float64)))
    idx = np.arange(RET_CHUNK, dtype=np.float64)
    diff = idx[:, None] - idx[None, :]
    decay_intra = np.where(diff >= 0, np.exp(log_gamma[:, None, None] * np.maximum(diff, 0.0)), 0.0)
    decay_intra = np.concatenate(list(decay_intra), axis=1)
    q_decay = np.exp(log_gamma[:, None] * (idx + 1.0))
    k_decay = np.exp(log_gamma[:, None] * (RET_CHUNK - 1.0 - idx))
    chunk_decay = np.exp(log_gamma * RET_CHUNK)
    lanes = lambda a: np.repeat(a.T, HDIM, axis=1)
    cd = np.repeat(chunk_decay, HDIM).reshape(1, GROUP)
    return tuple(jnp.asarray(a, F32) for a in (cos2, sin2, decay_intra, lanes(q_decay), lanes(k_decay), cd))


def _ret_kernel(q_ref, k_ref, v_ref, gate_ref, cos_ref, sin_ref, di_ref, qd_ref, kd_ref, cd_ref,
                o_ref, state_ref, *, tt):
    C = RET_CHUNK
    n_chunks = tt // C
    chunks = range(n_chunks)
    heads = range(HEADS)
    b = pl.program_id(1)

    @pl.when(pl.program_id(0) == 0)
    def _():
        state_ref[b] = jnp.zeros((GROUP, GROUP), F32)

    even = (_iota((tt, GROUP), 1) % 2) == 0

    def rotate(x):
        swapped = jnp.where(even, pltpu.roll(x, GROUP - 1, axis=1), pltpu.roll(x, 1, axis=1))
        return x * cos_ref[...] + swapped * sin_ref[...]

    q = rotate(q_ref[0])
    k = rotate(k_ref[0]) * (HDIM ** -0.5)
    lane_head = _iota((1, GROUP), 1) // HDIM
    same_head = (_iota((GROUP, GROUP), 0) // HDIM) == (_iota((GROUP, GROUP), 1) // HDIM)
    rows = lambda c: slice(c * C, (c + 1) * C)

    q_b = [q[rows(c)].astype(BF16) for c in chunks]
    k_b = [k[rows(c)].astype(BF16) for c in chunks]
    v_b = [v_ref[0, rows(c), :].astype(BF16) for c in chunks]
    kd_b = [(k[rows(c)] * kd_ref[...]).astype(BF16) for c in chunks]
    qd_b = [(q[rows(c)] * qd_ref[...]).astype(BF16) for c in chunks]

    row_head_is_lane_head = (_iota((HEADS * C, GROUP), 0) // C) == (_iota((HEADS * C, GROUP), 1) // HDIM)

    def per_head_rows(x):
        return jnp.where(row_head_is_lane_head, jnp.concatenate([x] * HEADS, axis=0), jnp.zeros((), x.dtype))

    inner = [(_nt(q_b[c], per_head_rows(k_b[c])) * di_ref[...]).astype(BF16) for c in chunks]
    kv = [jnp.where(same_head, _tn(kd_b[c], v_b[c]), 0.0) for c in chunks]
    intra = [_nn(inner[c], per_head_rows(v_b[c])) for c in chunks]
    state = state_ref[b]
    states = []
    for c in chunks:
        states.append(state.astype(BF16))
        state = state * cd_ref[...] + kv[c]
    state_ref[b] = state
    y = jnp.concatenate([intra[c] + _nn(qd_b[c], states[c]) for c in chunks], axis=0)
    mean_sq = _head_sum(y * y, _head_sum_matrix().astype(BF16)) * (1.0 / HDIM)
    o_ref[0] = (y * lax.rsqrt(mean_sq + NORM_EPS) * _silu(gate_ref[0])).astype(BF16)


def _ret(p3, tables, tt=1024):
    b, t, _ = p3.shape
    assert t % tt == 0 and tt % RET_CHUNK == 0
    base = 2 * 4
    cos2, sin2, di, qd, kd, cd = tables
    blk = lambda c: pl.BlockSpec((1, tt, GROUP), lambda j, i, c=c: (i, j, c))
    full2 = lambda a: pl.BlockSpec(a.shape, lambda j, i: (0, 0))
    return pl.pallas_call(
        functools.partial(_ret_kernel, tt=tt),
        out_shape=jax.ShapeDtypeStruct((b, t, GROUP), BF16),
        grid=(t // tt, b),
        in_specs=[blk(base), blk(base + 1), blk(base + 2), blk(base + 3),
                  pl.BlockSpec((tt, GROUP), lambda j, i: (j, 0)),
                  pl.BlockSpec((tt, GROUP), lambda j, i: (j, 0)),
                  full2(di), full2(qd), full2(kd), full2(cd)],
        out_specs=pl.BlockSpec((1, tt, GROUP), lambda j, i: (i, j, 0)),
        scratch_shapes=[pltpu.VMEM((b, GROUP, GROUP), F32)],
        compiler_params=pltpu.CompilerParams(
            dimension_semantics=("arbitrary", "arbitrary"), vmem_limit_bytes=VMEM_LIMIT_V7X),
        name="retnet",
    )(p3, p3, p3, p3, cos2, sin2, di, qd, kd, cd)


def _reorder_w_in(w):
    wt = jnp.swapaxes(w, 1, 2).astype(BF16)
    g3, g4 = 3 * GROUP, 4 * GROUP
    lora_end = g3 + 2 * LORA
    conv_start = lora_end + GROUP
    pad = jnp.zeros((w.shape[0], 128 - 2 * LORA, w.shape[1]), BF16)
    return jnp.concatenate([wt[:, :g3], wt[:, lora_end:conv_start], wt[:, conv_start + g4:],
                            wt[:, g3:lora_end], pad, wt[:, conv_start:conv_start + g4]], axis=1)


def kernel(x, norm_w, w_in, w_out, rwkv_mu, rwkv_w0, rwkv_w2, rwkv_a0, rwkv_a2, rwkv_k_k, rwkv_k_a,
           rwkv_r_k, rwkv_lnx_w, rwkv_lnx_b, conv_w, rel_bias, final_norm_w):
    b, t, d = x.shape
    depth = w_in.shape[0]
    bias_tiles = _moba_bias_tiles(rel_bias)
    ret_tables = _ret_tables(t)
    x2 = x.reshape(b * t, d)
    w_in_bf = _reorder_w_in(w_in)
    w_out_bf = w_out.astype(BF16)
    for l in range(depth):
        p2, y_conv = _inproj(x2, norm_w[l], w_in_bf, conv_w[l], l, seq_len=t)
        p3 = p2.reshape(b, t, P_COLS)
        y_rwkv = _rwkv(p3, rwkv_mu[l], rwkv_w0[l], rwkv_w2[l], rwkv_a0[l], rwkv_a2[l], rwkv_k_k[l],
                       rwkv_k_a[l], rwkv_r_k[l].reshape(GROUP), rwkv_lnx_w[l], rwkv_lnx_b[l])
        y_moba = _moba(p3, bias_tiles, rel_bias)
        y_ret = _ret(p3, ret_tables)
        flat = lambda y: y.reshape(b * t, GROUP)
        x2 = _outproj(x2, [flat(y_rwkv), y_conv, flat(y_moba), flat(y_ret)], w_out_bf, l,
                      final_norm_w, final=(l == depth - 1))
    return x2.reshape(b, t, d)
```

```python
import functools
import math

import numpy as np
import jax
import jax.numpy as jnp
from jax import lax
from jax.experimental import pallas as pl
from jax.experimental.pallas import tpu as pltpu

F32 = jnp.float32
BF16 = jnp.bfloat16

GROUP = 256
HEADS = 4
HDIM = 64
LORA = 32
DECAY_SCALE = math.exp(-0.5)
LOG2E = 1.0 / math.log(2.0)
LNX_EPS = 64e-5
NORM_EPS = 1e-6
CONV_TAPS = 3
MOBA_BLOCK = 256
MOBA_TOPK = 3
RET_CHUNK = 128
REL_BUCKETS = 32
REL_MAX_DIST = 128
RWKV_CHUNK = 64

P_COLS = 3 * 4 * GROUP + 128
LORA_COL = 3 * 4 * GROUP
W_ROWS = P_COLS + 4 * GROUP

VMEM_LIMIT_V7X = 48 * 1024 * 1024
VMEM_LIMIT_INPROJ_V7X = 56 * 1024 * 1024
BF16_ROWS = 16


def _nn(a, b, precision=None):
    return lax.dot_general(a, b, (((1,), (0,)), ((), ())), precision=precision,
                           preferred_element_type=F32)


def _nt(a, b, precision=None):
    return lax.dot_general(a, b, (((1,), (1,)), ((), ())), precision=precision,
                           preferred_element_type=F32)


def _tn(a, b, precision=None):
    return lax.dot_general(a, b, (((0,), (0,)), ((), ())), precision=precision,
                           preferred_element_type=F32)


def _split(x):
    hi = x.astype(BF16)
    lo = (x - hi.astype(F32)).astype(BF16)
    return hi, lo


def _dot3(dot, a, b):
    out_axis = 1 if dot is _tn else 0
    m = a[0].shape[out_axis]
    both = dot(jnp.concatenate([a[0], a[1]], axis=out_axis), b[0])
    return both[:m] + both[m:] + dot(a[0], b[1])


def _psl(pair, rows, cols):
    return pair[0][rows, cols], pair[1][rows, cols]


def _pcat(pairs, axis):
    return (jnp.concatenate([p[0] for p in pairs], axis=axis),
            jnp.concatenate([p[1] for p in pairs], axis=axis))


def _head_sum(x, esum_bf16):
    hi, lo = _split(x)
    m = x.shape[0]
    both = _nn(jnp.concatenate([hi, lo], axis=0), esum_bf16)
    return both[:m] + both[m:]


def _iota(shape, dim):
    return lax.broadcasted_iota(jnp.int32, shape, dim)


def _head_sum_matrix():
    r = _iota((GROUP, GROUP), 0) // HDIM
    c = _iota((GROUP, GROUP), 1) // HDIM
    return jnp.where(r == c, 1.0, 0.0).astype(F32)


def _silu(x):
    return x * jax.nn.sigmoid(x)


def _gated_conv(p, cw_ref, tail_ref, first_tile):
    @pl.when(first_tile)
    def _():
        tail_ref[...] = jnp.zeros_like(tail_ref)

    n_rows = p.shape[0]
    u = p[:, GROUP:2 * GROUP] * p[:, 2 * GROUP:3 * GROUP]
    rows = _iota((n_rows, 1), 0)
    u1 = jnp.where(rows == 0, tail_ref[1:2, :], pltpu.roll(u, 1, axis=0))
    u2 = jnp.where(rows == 0, tail_ref[0:1, :],
                   jnp.where(rows == 1, tail_ref[1:2, :], pltpu.roll(u, 2, axis=0)))
    tail_ref[0:2, :] = u[n_rows - 2:n_rows, :]
    y = u2 * cw_ref[0:1, :] + u1 * cw_ref[1:2, :] + u * cw_ref[2:3, :]
    return p[:, 0:GROUP] * y * _silu(p[:, 3 * GROUP:4 * GROUP])


def _inproj_kernel(x_ref, nw_ref, w_ref, cw_ref, o_ref, yconv_ref, tail_ref, wg_ref, *, tiles_per_seq):
    @pl.when(pl.program_id(0) == 0)
    def _():
        g3, g4 = 3 * GROUP, 4 * GROUP
        lora_end = g3 + 2 * LORA
        conv_start = lora_end + GROUP
        conv_end = conv_start + g4
        wg_ref[0:g3] = w_ref[0, 0:g3]
        wg_ref[g3:g4] = w_ref[0, lora_end:conv_start]
        wg_ref[g4:LORA_COL] = w_ref[0, conv_end:conv_end + 2 * g4]
        wg_ref[LORA_COL:LORA_COL + 2 * LORA] = w_ref[0, g3:lora_end]
        wg_ref[LORA_COL + 2 * LORA:P_COLS] = jnp.zeros((P_COLS - LORA_COL - 2 * LORA, wg_ref.shape[1]), BF16)
        wg_ref[P_COLS:W_ROWS] = w_ref[0, conv_start:conv_end]

    x = x_ref[...]
    h = x * lax.rsqrt(jnp.mean(x * x, axis=-1, keepdims=True) + NORM_EPS) * nw_ref[...]
    proj = _nt(h.astype(BF16), wg_ref[...])
    o_ref[...] = proj[:, 0:P_COLS]
    y_conv = _gated_conv(proj[:, P_COLS:W_ROWS], cw_ref, tail_ref, pl.program_id(0) % tiles_per_seq == 0)
    yconv_ref[...] = y_conv.astype(BF16)


def _inproj(x2d, norm_w, w_bf16, conv_w, layer, seq_len, tm=512):
    m, d = x2d.shape
    n_feat = w_bf16.shape[1]
    assert seq_len % tm == 0 and n_feat == W_ROWS - (128 - 2 * LORA)
    return pl.pallas_call(
        functools.partial(_inproj_kernel, tiles_per_seq=seq_len // tm),
        out_shape=(jax.ShapeDtypeStruct((m, P_COLS), F32),
                   jax.ShapeDtypeStruct((m, GROUP), BF16)),
        grid=(m // tm,),
        in_specs=[pl.BlockSpec((tm, d), lambda i: (i, 0)),
                  pl.BlockSpec((1, d), lambda i: (0, 0)),
                  pl.BlockSpec((1, n_feat, d), lambda i: (layer, 0, 0)),
                  pl.BlockSpec((CONV_TAPS, GROUP), lambda i: (0, 0))],
        out_specs=(pl.BlockSpec((tm, P_COLS), lambda i: (i, 0)),
                   pl.BlockSpec((tm, GROUP), lambda i: (i, 0))),
        scratch_shapes=[pltpu.VMEM((8, GROUP), F32),
                        pltpu.VMEM((W_ROWS, d), BF16)],
        compiler_params=pltpu.CompilerParams(
            dimension_semantics=("arbitrary",), vmem_limit_bytes=VMEM_LIMIT_INPROJ_V7X),
        name="inproj",
    )(x2d, norm_w.reshape(1, d), w_bf16, conv_w)


def _outproj_kernel(x_ref, y0_ref, y1_ref, y2_ref, y3_ref, w_ref, fw_ref, o_ref, *, final):
    acc = x_ref[...]
    for g, y_ref in enumerate((y0_ref, y1_ref, y2_ref, y3_ref)):
        acc = acc + _nn(y_ref[...], w_ref[0, g * GROUP:(g + 1) * GROUP, :])
    if final:
        acc = acc * lax.rsqrt(jnp.mean(acc * acc, axis=-1, keepdims=True) + NORM_EPS) * fw_ref[...]
    o_ref[...] = acc


def _outproj(x2d, ys, w_bf16, layer, final_w, final, tm=1024):
    m, d = x2d.shape
    yspec = pl.BlockSpec((tm, GROUP), lambda i: (i, 0))
    return pl.pallas_call(
        functools.partial(_outproj_kernel, final=final),
        out_shape=jax.ShapeDtypeStruct((m, d), F32),
        grid=(m // tm,),
        in_specs=[pl.BlockSpec((tm, d), lambda i: (i, 0)), yspec, yspec, yspec, yspec,
                  pl.BlockSpec((1, 4 * GROUP, d), lambda i: (layer, 0, 0)),
                  pl.BlockSpec((1, d), lambda i: (0, 0))],
        out_specs=pl.BlockSpec((tm, d), lambda i: (i, 0)),
        compiler_params=pltpu.CompilerParams(
            dimension_semantics=("arbitrary",), vmem_limit_bytes=VMEM_LIMIT_V7X),
        name="outproj_final" if final else "outproj",
    )(x2d, *ys, w_bf16, final_w.reshape(1, d))


def _rwkv_kernel(rkv_ref, gate_ref, lora_ref, mu_ref, mul_ref, w0_ref, w2_ref, a0_ref, a2_ref,
                 kk_ref, ka_ref, rk_ref, lnw_ref, lnb_ref, o_ref,
                 state_ref, prev_ref, prevl_ref, *, tt, nseq):
    C = RWKV_CHUNK
    n_rows = nseq * tt
    t_idx = pl.program_id(1)

    @pl.when(t_idx == 0)
    def _():
        state_ref[...] = jnp.zeros_like(state_ref)
        prev_ref[...] = jnp.zeros_like(prev_ref)
        prevl_ref[...] = jnp.zeros_like(prevl_ref)

    esum = _head_sum_matrix().astype(BF16)
    per_seq = tt // C
    row = _iota((tt, 1), 0)

    def prepare(s):
        p = rkv_ref[s]
        lo = lora_ref[s]
        p_sh = jnp.where(row == 0, prev_ref[s:s + 1, :], pltpu.roll(p, 1, axis=0))
        lo_sh = jnp.where(row == 0, prevl_ref[s:s + 1, :], pltpu.roll(lo, 1, axis=0))
        prev_ref[s:s + 1, :] = p[tt - 1:tt, :]
        prevl_ref[s:s + 1, :] = lo[tt - 1:tt, :]
        p = p + (p_sh - p) * mu_ref[...]
        lo = lo + (lo_sh - lo) * mul_ref[...]
        r = p[:, 0:GROUP]
        k = p[:, GROUP:2 * GROUP]
        v = p[:, 2 * GROUP:3 * GROUP]
        lw = -DECAY_SCALE * jax.nn.sigmoid(
            w0_ref[...] + _dot3(_nn, _split(jnp.tanh(lo)), _split(w2_ref[...])))
        rate = jax.nn.sigmoid(a0_ref[...] + _dot3(_nn, _split(lo), _split(a2_ref[...])))
        kk = k * kk_ref[...]
        kk = kk / jnp.maximum(jnp.sqrt(_head_sum(kk * kk, esum)), 1e-12)
        k2 = k * (1.0 + (rate - 1.0) * ka_ref[...])
        bonus = _head_sum(r * k2 * rk_ref[...], esum) * v
        bvec = kk * rate
        avec = -kk
        rowmod = row % C
        cum = lw
        for sh in (1, 2, 4, 8, 16, 32):
            cum = cum + jnp.where(rowmod >= sh, pltpu.roll(cum, sh, axis=0), 0.0)
        tot_rows = [cum[c * C + C - 1:c * C + C, :] for c in range(per_seq)]
        tot = jnp.concatenate([jnp.broadcast_to(tr, (C, GROUP)) for tr in tot_rows], axis=0)
        e_neg = jnp.exp(-cum)
        e_end = jnp.exp(tot - cum)
        whole = dict(r_t=r * jnp.exp(cum), a_t=avec * jnp.exp(cum - lw), k_t=k2 * e_neg, b_t=bvec * e_neg,
                     k_h=k2 * e_end, b_h=bvec * e_end, v_s=v)
        pairs = {name: _split(x) for name, x in whole.items()}
        per_chunk = {name: [_psl(pr, slice(c * C, (c + 1) * C), slice(None)) for c in range(per_seq)]
                     for name, pr in pairs.items()}
        return per_chunk, tot_rows, bonus, [v[c * C:(c + 1) * C, :] for c in range(per_seq)]

    same_head = (_iota((GROUP, GROUP), 0) // HDIM) == (_iota((GROUP, GROUP), 1) // HDIM)

    def bd(pr):
        return tuple(jnp.where(same_head, jnp.concatenate([part] * HEADS, axis=0), jnp.zeros((), part.dtype))
                     for part in pr)

    t_i = _iota((2 * C, GROUP), 0)
    s_i = _iota((2 * C, GROUP), 1) % C
    tri = ((t_i < C) & (s_i < t_i)) | ((t_i >= C) & (s_i <= t_i - C))
    eye = jnp.where(_iota((C, GROUP), 0) == _iota((C, GROUP), 1) % C, 1.0, 0.0).astype(F32)

    chunks = range(nseq * per_seq)
    every = slice(None)
    top = lambda pr: _psl(pr, slice(0, C), every)
    bottom = lambda pr: _psl(pr, slice(C, 2 * C), every)

    ops = {name: [] for name in ("r_t", "a_t", "k_t", "b_t", "k_h", "b_h", "v_s")}
    tot_rows, bonus, ab_rb, ak_rk, v_f = [], [], [], [], []
    for s in range(nseq):
        per_chunk, tots, bonus_s, v_chunks = prepare(s)
        for name in ops:
            ops[name] += per_chunk[name]
        tot_rows += tots
        v_f += v_chunks
        bonus.append(bonus_s)
        for c in range(s * per_seq, (s + 1) * per_seq):
            ar = _pcat([ops["a_t"][c], ops["r_t"][c]], 0)
            ab_rb.append(jnp.where(tri, _dot3(_nt, ar, bd(ops["b_t"][c])), 0.0))
            ak_rk.append(_split(jnp.where(tri, _dot3(_nt, ar, bd(ops["k_t"][c])), 0.0)))
    rows = lambda name, c: ops[name][c]
    bonus = jnp.concatenate(bonus, axis=0)
    bd_v = [bd(rows("v_s", c)) for c in chunks]
    a_ab = [ab_rb[c][0:C] for c in chunks]
    ab_rb = [_split(x) for x in ab_rb]
    t_row = _iota((C, GROUP), 0)
    s_col = _iota((C, GROUP), 1) % C

    def below_diagonal(n):
        return ((t_row // (2 * n)) == (s_col // (2 * n))) & ((t_row % (2 * n)) >= n) & ((s_col % (2 * n)) < n)

    inv = [eye + jnp.where(below_diagonal(1), a_ab[c], 0.0) for c in chunks]
    n = 2
    while n < C:
        off = below_diagonal(n)
        inv_s = [_split(inv[c]) for c in chunks]
        left = [_split(_dot3(_nn, inv_s[c], bd(_split(jnp.where(off, a_ab[c], 0.0))))) for c in chunks]
        inv = [inv[c] + _dot3(_nn, left[c], bd(inv_s[c])) for c in chunks]
        n *= 2
    inv = [_split(x) for x in inv]
    t_ak = [_split(_dot3(_nn, inv[c], bd(top(ak_rk[c])))) for c in chunks]
    t_a_f = [_dot3(_nn, inv[c], bd(rows("a_t", c))) for c in chunks]
    t_a = [_split(x) for x in t_a_f]
    free = [_dot3(_nn, _pcat([t_ak[c], bottom(ak_rk[c])], 0), bd_v[c]) for c in chunks]

    def head_transpose(x):
        xt = x.T
        return jnp.concatenate([xt[h * HDIM:(h + 1) * HDIM, :] for h in range(HEADS)], axis=1)

    bd_bh = [bd(rows("b_h", c)) for c in chunks]
    m_s = [bd(_split(_dot3(_nn, _split(head_transpose(t_a_f[c])), bd_bh[c]))) for c in chunks]
    g = [_dot3(_nn, _pcat([_split(head_transpose(v_f[c])), _split(head_transpose(free[c][0:C]))], 1),
               _pcat([bd(rows("k_h", c)), bd_bh[c]], 0)) for c in chunks]
    state = [state_ref[s] for s in range(nseq)]
    from_state = {}
    for step in range(per_seq):
        for s in range(nseq):
            c = s * per_seq + step
            st_s = _split(state[s])
            state[s] = state[s] * jnp.exp(tot_rows[c]) + _dot3(_nn, st_s, m_s[c]) + g[c]
            from_state[c] = _dot3(_nt, _pcat([t_a[c], rows("r_t", c)], 0), bd(st_s))
    for s in range(nseq):
        state_ref[s] = state[s]
    y = jnp.concatenate(
        [free[c][C:2 * C] + from_state[c][C:2 * C]
         + _dot3(_nn, bottom(ab_rb[c]), bd(_split(free[c][0:C] + from_state[c][0:C]))) for c in chunks], axis=0)
    mean = _head_sum(y, esum) * (1.0 / HDIM)
    yc = y - mean
    var = _head_sum(yc * yc, esum) * (1.0 / HDIM)
    y = yc * lax.rsqrt(var + LNX_EPS) * lnw_ref[...] + lnb_ref[...] + bonus
    o_ref[...] = (y * _silu(gate_ref[...].reshape(n_rows, GROUP))).astype(BF16).reshape(nseq, tt, GROUP)


def _rwkv(p3, mu, w0, w2, a0, a2, k_k, k_a, r_k, lnx_w, lnx_b, tt=256, nseq=4):
    b, t, _ = p3.shape
    assert b % nseq == 0 and t % tt == 0 and nseq <= 8
    mu_main = mu[:3 * GROUP].reshape(1, 3 * GROUP)
    mu_lora = jnp.concatenate([mu[3 * GROUP:], jnp.zeros((128 - 2 * LORA,), F32)]).reshape(1, 128)
    w2p = jnp.zeros((128, GROUP), F32).at[0:LORA].set(w2)
    a2p = jnp.zeros((128, GROUP), F32).at[LORA:2 * LORA].set(a2)
    row = lambda a: a.reshape(1, GROUP)
    vec = pl.BlockSpec((1, GROUP), lambda i, j: (0, 0))
    return pl.pallas_call(
        functools.partial(_rwkv_kernel, tt=tt, nseq=nseq),
        out_shape=jax.ShapeDtypeStruct((b, t, GROUP), BF16),
        grid=(b // nseq, t // tt),
        in_specs=[pl.BlockSpec((nseq, tt, 3 * GROUP), lambda i, j: (i, j, 0)),
                  pl.BlockSpec((nseq, tt, GROUP), lambda i, j: (i, j, 3)),
                  pl.BlockSpec((nseq, tt, 128), lambda i, j: (i, j, LORA_COL // 128)),
                  pl.BlockSpec((1, 3 * GROUP), lambda i, j: (0, 0)),
                  pl.BlockSpec((1, 128), lambda i, j: (0, 0)),
                  vec,
                  pl.BlockSpec((128, GROUP), lambda i, j: (0, 0)),
                  vec,
                  pl.BlockSpec((128, GROUP), lambda i, j: (0, 0)),
                  vec, vec, vec, vec, vec],
        out_specs=pl.BlockSpec((nseq, tt, GROUP), lambda i, j: (i, j, 0)),
        scratch_shapes=[pltpu.VMEM((nseq, HDIM, GROUP), F32),
                        pltpu.VMEM((8, 3 * GROUP), F32),
                        pltpu.VMEM((8, 128), F32)],
        compiler_params=pltpu.CompilerParams(
            dimension_semantics=("arbitrary", "arbitrary"), vmem_limit_bytes=VMEM_LIMIT_V7X),
        name="rwkv7",
    )(p3, p3, p3, mu_main, mu_lora, row(w0), w2p, row(a0), a2p,
      row(k_k), row(k_a), row(r_k), row(lnx_w), row(lnx_b))


def _t5_bucket_np(dist):
    max_exact = REL_BUCKETS // 2
    d_f = np.maximum(dist, 1).astype(np.float32)
    large = max_exact + (np.log(d_f / np.float32(max_exact)) / np.float32(math.log(REL_MAX_DIST / max_exact))
                         * np.float32(REL_BUCKETS - max_exact)).astype(np.int32)
    large = np.minimum(large, REL_BUCKETS - 1)
    return np.where(dist < max_exact, dist, large).astype(np.int32)


MASKED_BUCKET = REL_BUCKETS


def _moba_bucket_table():
    keys = np.arange(MOBA_BLOCK)[:, None]
    queries = np.arange(MOBA_BLOCK)[None, :]
    prev = _t5_bucket_np(queries + MOBA_BLOCK - keys)
    own = np.where(keys <= queries, _t5_bucket_np(np.maximum(queries - keys, 0)), MASKED_BUCKET)
    return np.stack([prev, own]).astype(np.int32)


def _bias_kernel(idx_ref, rb_ref, o_ref):
    h = pl.program_id(0)
    idx = idx_ref[...]
    acc = jnp.full(idx.shape, -jnp.inf, F32)
    for bkt in range(REL_BUCKETS):
        acc = jnp.where(idx == bkt, rb_ref[bkt, h], acc)
    o_ref[0] = acc * LOG2E


def _moba_bias_tiles(rel_bias):
    idx = jnp.asarray(_moba_bucket_table())
    shp = (2, MOBA_BLOCK, MOBA_BLOCK)
    return pl.pallas_call(
        _bias_kernel,
        out_shape=jax.ShapeDtypeStruct((HEADS,) + shp, F32),
        grid=(HEADS,),
        in_specs=[pl.BlockSpec(shp, lambda h: (0, 0, 0)),
                  pl.BlockSpec(memory_space=pltpu.SMEM)],
        out_specs=pl.BlockSpec((1,) + shp, lambda h: (h, 0, 0, 0)),
        name="moba_bias",
    )(idx, rel_bias)


def _moba_kernel(q_ref, k_ref, v_ref, gate_ref, bias_ref, rb_ref, o_ref,
                 kmean_ref, kbf_ref, vt_ref, sel_ref, m_ref, l_ref, acc_ref, sc_ref, *, nb, nseq):
    BLK = MOBA_BLOCK
    ib = pl.program_id(1)
    heads = range(nseq * HEADS)
    seq_of = lambda u: u // HEADS
    rows_of = lambda u: slice((u % HEADS) * HDIM, (u % HEADS + 1) * HDIM)
    neg_inf = -jnp.inf
    not_selected = -1e30

    @pl.when(ib == 0)
    def _():
        kmean_ref[...] = jnp.zeros_like(kmean_ref)
        for s in range(nseq):
            for n in range(nb):
                blk = slice(n * BLK, (n + 1) * BLK)
                kblk = k_ref[s, blk, :]
                kmean_ref[s, n:n + 1, :] = jnp.mean(kblk, axis=0, keepdims=True)
                kbf_ref[s, n] = kblk.astype(BF16)
                vt_ref[s, n, 0:GROUP, :] = v_ref[s, blk, :].T.astype(BF16)
                vt_ref[s, n, GROUP:GROUP + BF16_ROWS, :] = jnp.ones((BF16_ROWS, BLK), BF16)

    lane_head = _iota((1, GROUP), 1) // HDIM
    nb_pad = BF16_ROWS
    blk_id = _iota((nb_pad, BLK), 0)
    q_heads = []
    for s in range(nseq):
        q = q_ref[s] * (HDIM ** -0.5 * LOG2E)
        q_bf = q.astype(BF16)
        q_s = _split(q)
        q_heads += [jnp.where(lane_head == h, q_bf, jnp.zeros_like(q_bf)) for h in range(HEADS)]
        km = kmean_ref[s]
        gates = _dot3(_nt, _split(jnp.concatenate([jnp.where(lane_head == h, km, 0.0) for h in range(HEADS)],
                                                  axis=0)), q_s)
        for h in range(HEADS):
            g = jnp.where(blk_id < ib, gates[nb_pad * h:nb_pad * (h + 1)], neg_inf)
            ahead = jnp.zeros((nb_pad, BLK), F32)
            for m in range(nb):
                g_m = g[m:m + 1, :]
                ahead = ahead + jnp.where((g_m > g) | ((g_m == g) & (blk_id > m)), 1.0, 0.0)
            sel_ref[s * HEADS + h] = jnp.where((ahead < MOBA_TOPK) & (g > neg_inf), 1.0, 0.0)

    def raw_scores(n, u):
        return _nt(kbf_ref[seq_of(u), n], q_heads[u])

    def attend(blocks, first, scores_slot=None, prefetch=None):
        if scores_slot is None:
            scores = [[raw_scores(n, h) for n, _, _ in blocks] for h in heads]
        else:
            scores = [[sc_ref[scores_slot, j, h] for j in range(len(blocks))] for h in heads]
        if prefetch is not None:
            slot, nxt = prefetch
            for j, n in enumerate(nxt):
                for h in heads:
                    sc_ref[slot, j, h] = raw_scores(n, h)
        probs, alphas = [], []
        for h in heads:
            ss, tops = [], []
            for s, (_, tile, row) in zip(scores[h], blocks):
                s = s if tile is None else s + tile[h]
                top = jnp.max(s, axis=0, keepdims=True)
                ss.append(s)
                tops.append(top if row is None else top + row[h])
            m_new = functools.reduce(jnp.maximum, tops)
            if not first:
                m_old = m_ref[h]
                m_new = jnp.maximum(m_old, m_new)
                alphas.append(jnp.exp2(m_old - m_new))
            m_ref[h] = m_new
            probs.append([jnp.exp2(s + ((-m_new) if row is None else (row[h] - m_new))).astype(BF16)
                          for s, (_, _, row) in zip(ss, blocks)])
        for h in heads:
            pv = None
            for p, (n, _, _) in zip(probs[h], blocks):
                vt = vt_ref.at[seq_of(h), n]
                lhs = jnp.concatenate([vt[rows_of(h), :], vt[GROUP:GROUP + BF16_ROWS, :]], axis=0)
                part = _nn(lhs, p)
                pv = part if pv is None else pv + part
            acc = acc_ref.at[seq_of(h)]
            if first:
                acc[rows_of(h), :] = pv[0:HDIM]
                l_ref[h] = pv[HDIM:HDIM + 1]
            else:
                acc[rows_of(h), :] = alphas[h] * acc[rows_of(h), :] + pv[0:HDIM]
                l_ref[h] = alphas[h] * l_ref[h] + pv[HDIM:HDIM + 1]

    def selected_row(h, n, valid):
        row = jnp.where(sel_ref[h, pl.ds(n, 1), :] > 0.5, 0.0, not_selected)
        return jnp.where(valid, row, not_selected)

    prev = jnp.maximum(ib - 1, 0)
    far_pairs = [[n] if n + 1 >= nb - 2 else [n, n + 1] for n in range(0, nb - 2, 2)]
    attend([(ib, [bias_ref[h % HEADS, 1] for h in heads], None),
            (prev, [bias_ref[h % HEADS, 0] for h in heads], [selected_row(h, prev, ib >= 1) for h in heads])],
           first=True)

    n_far = ib - 1
    for k, pair in enumerate(far_pairs):
        @pl.when(pair[0] < n_far)
        def _(k=k, pair=pair):
            attend([(j, None, [(selected_row(h, j, j < n_far) + rb_ref[REL_BUCKETS - 1, h % HEADS]) * LOG2E
                               for h in heads])
                    for j in pair], first=False, scores_slot=(k % 2) if k > 0 else None,
                   prefetch=((k + 1) % 2, far_pairs[k + 1]) if k + 1 < len(far_pairs) else None)

    for h in heads:
        acc = acc_ref.at[seq_of(h)]
        acc[rows_of(h), :] = acc[rows_of(h), :] / l_ref[h]
    for s in range(nseq):
        o_ref[s] = (acc_ref[s].T * _silu(gate_ref[s])).astype(BF16)


def _moba(p3, bias_tiles, rel_bias, nseq=2):
    b, t, _ = p3.shape
    BLK = MOBA_BLOCK
    nb = t // BLK
    assert 2 <= nb <= BF16_ROWS and t % BLK == 0 and b % nseq == 0
    base = 1 * 4
    tile = lambda c: pl.BlockSpec((nseq, BLK, GROUP), lambda i, j, c=c: (i, j, c))
    seq = lambda c: pl.BlockSpec((nseq, t, GROUP), lambda i, j, c=c: (i, 0, c))
    pairs = nseq * HEADS
    return pl.pallas_call(
        functools.partial(_moba_kernel, nb=nb, nseq=nseq),
        out_shape=jax.ShapeDtypeStruct((b, t, GROUP), BF16),
        grid=(b // nseq, nb),
        in_specs=[tile(base), seq(base + 1), seq(base + 2), tile(base + 3),
                  pl.BlockSpec(bias_tiles.shape, lambda i, j: (0, 0, 0, 0)),
                  pl.BlockSpec(memory_space=pltpu.SMEM)],
        out_specs=pl.BlockSpec((nseq, BLK, GROUP), lambda i, j: (i, j, 0)),
        scratch_shapes=[pltpu.VMEM((nseq, BF16_ROWS, GROUP), F32),
                        pltpu.VMEM((nseq, nb, BLK, GROUP), BF16),
                        pltpu.VMEM((nseq, nb, GROUP + BF16_ROWS, BLK), BF16),
                        pltpu.VMEM((pairs, BF16_ROWS, BLK), F32),
                        pltpu.VMEM((pairs, 1, BLK), F32),
                        pltpu.VMEM((pairs, 1, BLK), F32),
                        pltpu.VMEM((nseq, GROUP, BLK), F32),
                        pltpu.VMEM((2, 2, pairs, BLK, BLK), F32)],
        compiler_params=pltpu.CompilerParams(
            dimension_semantics=("arbitrary", "arbitrary"), vmem_limit_bytes=VMEM_LIMIT_V7X),
        name="moba",
    )(p3, p3, p3, p3, bias_tiles, rel_bias)


def _ret_tables(t):
    half = HDIM // 2
    theta = 1.0 / (10000.0 ** np.linspace(0.0, 1.0, half))
    pos = np.arange(t, dtype=np.float64)
    ang = pos[:, None] * theta[None, :]
    sin, cos = np.sin(ang), np.cos(ang)
    cos2 = np.tile(np.repeat(cos, 2, axis=1), (1, HEADS))
    sin2 = np.tile(np.stack([-sin, sin], axis=-1).reshape(t, HDIM), (1, HEADS))
    log_gamma = np.log(1.0 - 2.0 ** (-5.0 - np.arange(HEADS, dtype=np.float64)))
    idx = np.arange(RET_CHUNK, dtype=np.float64)
    diff = idx[:, None] - idx[None, :]
    decay_intra = np.where(diff >= 0, np.exp(log_gamma[:, None, None] * np.maximum(diff, 0.0)), 0.0)
    decay_intra = np.concatenate(list(decay_intra), axis=1)
    q_decay = np.exp(log_gamma[:, None] * (idx + 1.0))
    k_decay = np.exp(log_gamma[:, None] * (RET_CHUNK - 1.0 - idx))
    chunk_decay = np.exp(log_gamma * RET_CHUNK)
    lanes = lambda a: np.repeat(a.T, HDIM, axis=1)
    cd = np.repeat(chunk_decay, HDIM).reshape(1, GROUP)
    return tuple(jnp.asarray(a, F32) for a in (cos2, sin2, decay_intra, lanes(q_decay), lanes(k_decay), cd))


def _ret_kernel(q_ref, k_ref, v_ref, gate_ref, cos_ref, sin_ref, di_ref, qd_ref, kd_ref, cd_ref,
                o_ref, state_ref, *, tt):
    C = RET_CHUNK
    n_chunks = tt // C
    chunks = range(n_chunks)
    heads = range(HEADS)
    b = pl.program_id(1)

    @pl.when(pl.program_id(0) == 0)
    def _():
        state_ref[b] = jnp.zeros((GROUP, GROUP), F32)

    even = (_iota((tt, GROUP), 1) % 2) == 0

    def rotate(x):
        swapped = jnp.where(even, pltpu.roll(x, GROUP - 1, axis=1), pltpu.roll(x, 1, axis=1))
        return x * cos_ref[...] + swapped * sin_ref[...]

    q = rotate(q_ref[0])
    k = rotate(k_ref[0]) * (HDIM ** -0.5)
    lane_head = _iota((1, GROUP), 1) // HDIM
    same_head = (_iota((GROUP, GROUP), 0) // HDIM) == (_iota((GROUP, GROUP), 1) // HDIM)
    rows = lambda c: slice(c * C, (c + 1) * C)

    q_b = [q[rows(c)].astype(BF16) for c in chunks]
    k_b = [k[rows(c)].astype(BF16) for c in chunks]
    v_b = [v_ref[0, rows(c), :].astype(BF16) for c in chunks]
    kd_b = [(k[rows(c)] * kd_ref[...]).astype(BF16) for c in chunks]
    qd_b = [(q[rows(c)] * qd_ref[...]).astype(BF16) for c in chunks]

    row_head_is_lane_head = (_iota((HEADS * C, GROUP), 0) // C) == (_iota((HEADS * C, GROUP), 1) // HDIM)

    def per_head_rows(x):
        return jnp.where(row_head_is_lane_head, jnp.concatenate([x] * HEADS, axis=0), jnp.zeros((), x.dtype))

    inner = [(_nt(q_b[c], per_head_rows(k_b[c])) * di_ref[...]).astype(BF16) for c in chunks]
    kv = [jnp.where(same_head, _tn(kd_b[c], v_b[c]), 0.0) for c in chunks]
    intra = [_nn(inner[c], per_head_rows(v_b[c])) for c in chunks]
    state = state_ref[b]
    states = []
    for c in chunks:
        states.append(state.astype(BF16))
        state = state * cd_ref[...] + kv[c]
    state_ref[b] = state
    y = jnp.concatenate([intra[c] + _nn(qd_b[c], states[c]) for c in chunks], axis=0)
    mean_sq = _head_sum(y * y, _head_sum_matrix().astype(BF16)) * (1.0 / HDIM)
    o_ref[0] = (y * lax.rsqrt(mean_sq + NORM_EPS) * _silu(gate_ref[0])).astype(BF16)


def _ret(p3, tables, tt=1024):
    b, t, _ = p3.shape
    assert t % tt == 0 and tt % RET_CHUNK == 0
    base = 2 * 4
    cos2, sin2, di, qd, kd, cd = tables
    blk = lambda c: pl.BlockSpec((1, tt, GROUP), lambda j, i, c=c: (i, j, c))
    full2 = lambda a: pl.BlockSpec(a.shape, lambda j, i: (0, 0))
    return pl.pallas_call(
        functools.partial(_ret_kernel, tt=tt),
        out_shape=jax.ShapeDtypeStruct((b, t, GROUP), BF16),
        grid=(t // tt, b),
        in_specs=[blk(base), blk(base + 1), blk(base + 2), blk(base + 3),
                  pl.BlockSpec((tt, GROUP), lambda j, i: (j, 0)),
                  pl.BlockSpec((tt, GROUP), lambda j, i: (j, 0)),
                  full2(di), full2(qd), full2(kd), full2(cd)],
        out_specs=pl.BlockSpec((1, tt, GROUP), lambda j, i: (i, j, 0)),
        scratch_shapes=[pltpu.VMEM((b, GROUP, GROUP), F32)],
        compiler_params=pltpu.CompilerParams(
            dimension_semantics=("arbitrary", "arbitrary"), vmem_limit_bytes=VMEM_LIMIT_V7X),
        name="retnet",
    )(p3, p3, p3, p3, cos2, sin2, di, qd, kd, cd)


def kernel(x, norm_w, w_in, w_out, rwkv_mu, rwkv_w0, rwkv_w2, rwkv_a0, rwkv_a2, rwkv_k_k, rwkv_k_a,
           rwkv_r_k, rwkv_lnx_w, rwkv_lnx_b, conv_w, rel_bias, final_norm_w):
    b, t, d = x.shape
    depth = w_in.shape[0]
    bias_tiles = _moba_bias_tiles(rel_bias)
    ret_tables = _ret_tables(t)
    x2 = x.reshape(b * t, d)
    w_in_bf = jnp.swapaxes(w_in, 1, 2).astype(BF16)
    w_out_bf = w_out.astype(BF16)
    for l in range(depth):
        p2, y_conv = _inproj(x2, norm_w[l], w_in_bf, conv_w[l], l, seq_len=t)
        p3 = p2.reshape(b, t, P_COLS)
        y_rwkv = _rwkv(p3, rwkv_mu[l], rwkv_w0[l], rwkv_w2[l], rwkv_a0[l], rwkv_a2[l], rwkv_k_k[l],
                       rwkv_k_a[l], rwkv_r_k[l].reshape(GROUP), rwkv_lnx_w[l], rwkv_lnx_b[l])
        y_moba = _moba(p3, bias_tiles, rel_bias)
        y_ret = _ret(p3, ret_tables)
        flat = lambda y: y.reshape(b * t, GROUP)
        x2 = _outproj(x2, [flat(y_rwkv), y_conv, flat(y_moba), flat(y_ret)], w_out_bf, l,
                      final_norm_w, final=(l == depth - 1))
    return x2.reshape(b, t, d)
```
